```python
import jax, jax.numpy as jnp
from jax import lax
import numpy as np

D_MODEL = 2048
BATCH = 4
SEQ = 2048
DEPTH = 1
DEC_BATCH = 128
DEC_SEQ = 1
PAST_LEN = 16384
PAGE_SIZE = 128

N_META = 16
LRU_WIDTH = D_MODEL // 2
LRU_HEADS = 16
LRU_HEAD_DIM = LRU_WIDTH // LRU_HEADS
CONV_WIDTH = 4
LRU_C = 8.0
POOL_WIDTH = D_MODEL - LRU_WIDTH
POOL_WINDOWS = (2, 4, 8, 16)
POOL_GROUPS = len(POOL_WINDOWS)
POOL_GROUP_DIM = POOL_WIDTH // POOL_GROUPS
POOL_BUF = max(POOL_WINDOWS) - 1
IN_WIDTH = 2 * LRU_WIDTH + POOL_WIDTH
N_EXPERTS = 32
TOP_K = 4
D_FF = D_MODEL
SWIGLU_ALPHA = 1.702
SWIGLU_LIMIT = 7.0
RMS_EPS = 1e-6

kernel_name = "hymba_rglru_pool_moe_step"

F32 = jnp.float32


def rmsnorm(x, g):
    xf = x.astype(F32)
    y = xf * lax.rsqrt(jnp.mean(xf * xf, axis=-1, keepdims=True) + RMS_EPS)
    return (y * g.astype(F32)).astype(x.dtype)


def causal_conv(u, buf, w, b):
    T = u.shape[1]
    up = jnp.concatenate([buf.astype(u.dtype), u], axis=1)
    y = b + sum(up[:, k:k + T] * w[k] for k in range(CONV_WIDTH))
    return y, up[:, -(CONV_WIDTH - 1):]


def block_diag(x, w, b):
    B, T, _ = x.shape
    xh = x.reshape(B, T, LRU_HEADS, LRU_HEAD_DIM)
    return jnp.einsum('bthi,hij->bthj', xh, w).reshape(B, T, LRU_WIDTH) + b


def rglru(x, h0, wa, ba, wx, bx, lam):
    r = jax.nn.sigmoid(block_diag(x, wa, ba).astype(F32))
    i = jax.nn.sigmoid(block_diag(x, wx, bx).astype(F32))
    log_a = -LRU_C * r * jax.nn.softplus(-lam.astype(F32))
    a = jnp.exp(log_a)
    beta = jnp.sqrt(-jnp.expm1(2.0 * log_a))
    bterm = beta * i * x.astype(F32)

    def step(h, ab):
        a_t, b_t = ab
        h = a_t * h + b_t
        return h, h

    hT, hs = lax.scan(step, h0.astype(F32), (jnp.swapaxes(a, 0, 1), jnp.swapaxes(bterm, 0, 1)))
    return jnp.swapaxes(hs, 0, 1).astype(x.dtype), hT.astype(x.dtype)


def multiscale_pool(u, buf, start, w_pool, scale):
    B, T, C = u.shape
    up = jnp.concatenate([buf.astype(u.dtype), u], axis=1)
    cs = jnp.cumsum(up.astype(F32), axis=1)
    cs = jnp.pad(cs, ((0, 0), (1, 0), (0, 0)))
    pos = start + jnp.arange(T)
    outs = []
    for g, w in enumerate(POOL_WINDOWS):
        sl = slice(g * POOL_GROUP_DIM, (g + 1) * POOL_GROUP_DIM)
        hi = cs[:, POOL_BUF + 1:POOL_BUF + 1 + T, sl]
        lo = cs[:, POOL_BUF + 1 - w:POOL_BUF + 1 - w + T, sl]
        cnt = jnp.minimum(w, pos + 1).astype(F32)[None, :, None]
        outs.append((hi - lo) / cnt)
    mean = jnp.concatenate(outs, axis=-1)
    d = (mean - u.astype(F32)).astype(u.dtype).reshape(B, T, POOL_GROUPS, POOL_GROUP_DIM)
    y = jnp.einsum('btgi,gij->btgj', d, w_pool).reshape(B, T, C) * scale
    return y, up[:, -POOL_BUF:]


def token_mixers(h, conv_buf, lru_h, pool_buf, start, w_in, conv_w, conv_b, wa, ba, wx, bx, lam,
                 w_pool, pool_scale, w_out):
    proj = h @ w_in
    xa = proj[..., :LRU_WIDTH]
    ga = proj[..., LRU_WIDTH:2 * LRU_WIDTH]
    ub = proj[..., 2 * LRU_WIDTH:]
    xc, new_conv = causal_conv(xa, conv_buf, conv_w, conv_b)
    hs, new_h = rglru(xc, lru_h, wa, ba, wx, bx, lam)
    ya = hs * jax.nn.gelu(ga)
    yb, new_pool = multiscale_pool(ub, pool_buf, start, w_pool, pool_scale)
    y = jnp.concatenate([ya, yb], axis=-1) @ w_out
    return y, new_conv, new_h, new_pool


def moe(x, router_w, router_b, wg, bg, wu, bu, wd, bd):
    logits = (x @ router_w + router_b).astype(F32)
    top_v, top_i = lax.top_k(logits, TOP_K)
    probs = jax.nn.softmax(top_v, axis=-1)
    combine = jnp.sum(jax.nn.one_hot(top_i, N_EXPERTS, dtype=F32) * probs[..., None], axis=1)

    def expert(acc, p):
        wg_e, bg_e, wu_e, bu_e, wd_e, bd_e, c_e = p
        g = jnp.minimum(x @ wg_e + bg_e, SWIGLU_LIMIT)
        u = jnp.clip(x @ wu_e + bu_e, -SWIGLU_LIMIT, SWIGLU_LIMIT)
        hdn = g * jax.nn.sigmoid(SWIGLU_ALPHA * g) * (u + 1)
        return acc + c_e[:, None] * (hdn @ wd_e + bd_e).astype(F32), None

    acc, _ = lax.scan(expert, jnp.zeros(x.shape, F32), (wg, bg, wu, bu, wd, bd, combine.T))
    return acc.astype(x.dtype)


def setup_inputs(seed: int = 0) -> dict:
    key = jax.random.key(seed)
    ks = jax.random.split(key, 32)
    n = jax.random.normal
    L, D, E, F = DEPTH, D_MODEL, N_EXPERTS, D_FF
    a0 = jax.random.uniform(ks[12], (L, LRU_WIDTH), F32, 0.9, 0.999)
    p = a0 ** (1.0 / LRU_C)
    lam = jnp.log(p) - jnp.log1p(-p)
    return {
        "x_prompt": n(ks[0], (BATCH, SEQ, D), F32),
        "x_sample": n(ks[1], (DEC_BATCH, DEC_SEQ, D), F32),
        "state_conv": n(ks[2], (L, DEC_BATCH, CONV_WIDTH - 1, LRU_WIDTH), F32),
        "state_rglru": 0.5 * n(ks[3], (L, DEC_BATCH, LRU_WIDTH), F32),
        "state_pool": n(ks[4], (L, DEC_BATCH, POOL_BUF, POOL_WIDTH), F32),
        "meta_tokens": n(ks[5], (N_META, D), F32),
        "norm1": 1.0 + 0.01 * n(ks[6], (L, D), F32),
        "w_in": n(ks[7], (L, D, IN_WIDTH), F32) * D ** -0.5,
        "conv_w": n(ks[8], (L, CONV_WIDTH, LRU_WIDTH), F32) * CONV_WIDTH ** -0.5,
        "conv_b": 0.01 * n(ks[9], (L, LRU_WIDTH), F32),
        "lru_wa": n(ks[10], (L, LRU_HEADS, LRU_HEAD_DIM, LRU_HEAD_DIM), F32) * LRU_HEAD_DIM ** -0.5,
        "lru_ba": 0.01 * n(ks[11], (L, LRU_WIDTH), F32),
        "lru_wx": n(ks[13], (L, LRU_HEADS, LRU_HEAD_DIM, LRU_HEAD_DIM), F32) * LRU_HEAD_DIM ** -0.5,
        "lru_bx": 0.01 * n(ks[14], (L, LRU_WIDTH), F32),
        "lru_lambda": lam,
        "pool_w": n(ks[15], (L, POOL_GROUPS, POOL_GROUP_DIM, POOL_GROUP_DIM), F32) * POOL_GROUP_DIM ** -0.5,
        "pool_scale": 1.0 + 0.1 * n(ks[16], (L, POOL_WIDTH), F32),
        "w_out": n(ks[17], (L, D, D), F32) * D ** -0.5,
        "norm2": 1.0 + 0.01 * n(ks[18], (L, D), F32),
        "router_w": n(ks[19], (L, D, E), F32) * D ** -0.5,
        "router_b": 0.01 * n(ks[20], (L, E), F32),
        "exp_wg": n(ks[21], (L, E, D, F), F32) * D ** -0.5,
        "exp_bg": 0.01 * n(ks[22], (L, E, F), F32),
        "exp_wu": n(ks[23], (L, E, D, F), F32) * D ** -0.5,
        "exp_bu": 0.01 * n(ks[24], (L, E, F), F32),
        "exp_wd": n(ks[25], (L, E, F, D), F32) * F ** -0.5,
        "exp_bd": 0.01 * n(ks[26], (L, E, D), F32),
        "final_norm": 1.0 + 0.01 * n(ks[27], (D,), F32),
    }


def reference(x_prompt, x_sample, state_conv, state_rglru, state_pool, meta_tokens, norm1, w_in,
              conv_w, conv_b, lru_wa, lru_ba, lru_wx, lru_bx, lru_lambda, pool_w, pool_scale, w_out,
              norm2, router_w, router_b, exp_wg, exp_bg, exp_wu, exp_bu, exp_wd, exp_bd, final_norm):
    B = x_prompt.shape[0]
    dt = x_prompt.dtype
    hp = jnp.concatenate([jnp.broadcast_to(meta_tokens.astype(dt), (B, N_META, D_MODEL)), x_prompt], axis=1)
    hs = x_sample
    Tp = hp.shape[1]
    n_prompt = B * Tp
    zero_conv = jnp.zeros((B, CONV_WIDTH - 1, LRU_WIDTH), dt)
    zero_h = jnp.zeros((B, LRU_WIDTH), dt)
    zero_pool = jnp.zeros((B, POOL_BUF, POOL_WIDTH), dt)
    conv_p, lru_p, pool_p, conv_s, lru_s, pool_s = [], [], [], [], [], []
    for l in range(DEPTH):
        mix_w = (w_in[l], conv_w[l], conv_b[l], lru_wa[l], lru_ba[l], lru_wx[l], lru_bx[l],
                 lru_lambda[l], pool_w[l], pool_scale[l], w_out[l])
        yp, cp, lp, pp = token_mixers(rmsnorm(hp, norm1[l]), zero_conv, zero_h, zero_pool, 0, *mix_w)
        ys, cs_, ls, ps = token_mixers(rmsnorm(hs, norm1[l]), state_conv[l], state_rglru[l],
                                       state_pool[l], PAST_LEN, *mix_w)
        hp = hp + yp
        hs = hs + ys
        flat = jnp.concatenate([hp.reshape(-1, D_MODEL), hs.reshape(-1, D_MODEL)], axis=0)
        f = moe(rmsnorm(flat, norm2[l]), router_w[l], router_b[l], exp_wg[l], exp_bg[l],
                exp_wu[l], exp_bu[l], exp_wd[l], exp_bd[l])
        hp = hp + f[:n_prompt].reshape(hp.shape)
        hs = hs + f[n_prompt:].reshape(hs.shape)
        conv_p.append(cp); lru_p.append(lp); pool_p.append(pp)
        conv_s.append(cs_); lru_s.append(ls); pool_s.append(ps)
    y_prompt = rmsnorm(hp[:, N_META:], final_norm)
    y_sample = rmsnorm(hs, final_norm)
    return (y_prompt, y_sample, jnp.stack(conv_p), jnp.stack(lru_p), jnp.stack(pool_p),
            jnp.stack(conv_s), jnp.stack(lru_s), jnp.stack(pool_s))
```

```python
import functools

import jax
import jax.numpy as jnp
from jax import lax
from jax.experimental import pallas as pl
from jax.experimental.pallas import tpu as pltpu

F32 = jnp.float32
BF16 = jnp.bfloat16
I32 = jnp.int32
U32 = jnp.uint32

D_MODEL = 2048
N_META = 16
LRU_WIDTH = 1024
LRU_HEADS = 16
LRU_HEAD_DIM = LRU_WIDTH // LRU_HEADS
CONV_WIDTH = 4
LRU_C = 8.0
POOL_WIDTH = D_MODEL - LRU_WIDTH
POOL_WINDOWS = (2, 4, 8, 16)
POOL_GROUP_DIM = POOL_WIDTH // len(POOL_WINDOWS)
POOL_BUF = max(POOL_WINDOWS) - 1
IN_WIDTH = 2 * LRU_WIDTH + POOL_WIDTH
N_EXPERTS = 32
TOP_K = 4
D_FF = D_MODEL
SWIGLU_ALPHA = 1.702
SWIGLU_LIMIT = 7.0
RMS_EPS = 1e-6

V7X_MXU_DIM = 256
V7X_VMEM_BYTES = 64 << 20
SUBLANES = 8

ROW_TILE = 128
TIME_TILE = 256
MOE_TM = 256
MOE_TILES_PER_GROUP = 5
MOE_GROUP_ROWS = MOE_TM * MOE_TILES_PER_GROUP
MOE_TF = V7X_MXU_DIM
COMBINE_TILE = 128


def _params(vmem_mb, sem):
    return pltpu.CompilerParams(dimension_semantics=sem, vmem_limit_bytes=vmem_mb << 20)


def _rmsnorm(x, g):
    return x * lax.rsqrt(jnp.mean(x * x, axis=-1, keepdims=True) + RMS_EPS) * g


def _proj_kernel(xp_ref, xs_ref, g_ref, w_ref, o_ref, *, n_prompt_blocks):
    i = pl.program_id(0)
    x = jnp.where(i < n_prompt_blocks, xp_ref[...], xs_ref[...])
    xn = _rmsnorm(x, g_ref[...])
    o_ref[...] = jnp.dot(xn.astype(BF16), w_ref[...], preferred_element_type=F32)


def _proj(xp, xs, g, w_bf16, tile):
    npb = xp.shape[0] // tile
    if xs is None:
        xs, nsb = xp, 0
    else:
        nsb = xs.shape[0] // tile
    rows = (npb + nsb) * tile
    return pl.pallas_call(
        functools.partial(_proj_kernel, n_prompt_blocks=npb),
        grid=(npb + nsb,),
        in_specs=[
            pl.BlockSpec((tile, D_MODEL), lambda i: (jnp.minimum(i, npb - 1), 0)),
            pl.BlockSpec((tile, D_MODEL), lambda i: (jnp.maximum(i - npb, 0), 0)),
            pl.BlockSpec((1, D_MODEL), lambda i: (0, 0)),
            pl.BlockSpec((D_MODEL, IN_WIDTH), lambda i: (0, 0)),
        ],
        out_specs=pl.BlockSpec((tile, IN_WIDTH), lambda i: (i, 0)),
        out_shape=jax.ShapeDtypeStruct((rows, IN_WIDTH), F32),
        compiler_params=_params(48, ("arbitrary",)),
        name="proj",
    )(xp, xs, g, w_bf16)


def _lru_coeffs(xc, wa_ref, ba_ref, wx_ref, bx_ref, lam_ref):
    xcb = xc.astype(BF16)
    nq = LRU_WIDTH // V7X_MXU_DIM
    rs, gs = [], []
    for q in range(nq):
        blk = xcb[:, q * V7X_MXU_DIM:(q + 1) * V7X_MXU_DIM]
        rs.append(jnp.dot(blk, wa_ref[q], preferred_element_type=F32))
        gs.append(jnp.dot(blk, wx_ref[q], preferred_element_type=F32))
    r = jax.nn.sigmoid(jnp.concatenate(rs, axis=-1) + ba_ref[...])
    i = jax.nn.sigmoid(jnp.concatenate(gs, axis=-1) + bx_ref[...])
    log_a = -LRU_C * r * jax.nn.softplus(-lam_ref[...])
    a = jnp.exp(log_a)
    th = jnp.tanh(log_a)
    beta = jnp.sqrt(-2.0 * th / (1.0 - th))
    return a, beta * i * xc


def _pool_project(wins, u, inv_cnt, pw_ref, ps_ref):
    outs = []
    for g in range(len(POOL_WINDOWS)):
        sl = slice(g * POOL_GROUP_DIM, (g + 1) * POOL_GROUP_DIM)
        d = (wins[g] * inv_cnt[g] - u[:, sl]).astype(BF16)
        outs.append(jnp.dot(d, pw_ref[g], preferred_element_type=F32))
    return jnp.concatenate(outs, axis=-1) * ps_ref[...]


def _mix_seq_kernel(xa_ref, ga_ref, ub_ref, conv0_ref, h0_ref, pool0_ref,
                    cw_ref, cb_ref, wa_ref, ba_ref, wx_ref, bx_ref, lam_ref, pw_ref, ps_ref,
                    yab_ref, ht_ref, cext, pext, hcar, *, tt, start):
    t = pl.program_id(1)
    hist = 2 * SUBLANES

    @pl.when(t == 0)
    def _():
        cext[0:SUBLANES, :] = conv0_ref[...]
        pext[0:hist, :] = pool0_ref[...]
        hcar[...] = h0_ref[...]

    xa = xa_ref[...]
    cext[SUBLANES:SUBLANES + tt, :] = xa
    cw = cw_ref[...]
    xc = cb_ref[...] + cw[CONV_WIDTH - 1:CONV_WIDTH] * xa
    for k in range(1, CONV_WIDTH):
        xc = xc + cw[CONV_WIDTH - 1 - k:CONV_WIDTH - k] * cext[SUBLANES - k:SUBLANES - k + tt, :]
    cext[0:SUBLANES, :] = cext[tt:tt + SUBLANES, :]

    a, b = _lru_coeffs(xc, wa_ref, ba_ref, wx_ref, bx_ref, lam_ref)

    row = lax.broadcasted_iota(I32, (tt, LRU_WIDTH), 0)
    s = 1
    while s < tt:
        keep = row >= s
        a_prev = jnp.where(keep, pltpu.roll(a, s, 0), 1.0)
        b_prev = jnp.where(keep, pltpu.roll(b, s, 0), 0.0)
        b = b + a * b_prev
        a = a * a_prev
        s *= 2
    hs = a * hcar[SUBLANES - 1:SUBLANES, :] + b
    hcar[...] = hs[tt - SUBLANES:tt, :]
    ht_ref[...] = hs[tt - SUBLANES:tt, :]
    ya = hs * jax.nn.gelu(ga_ref[...])

    u = ub_ref[...]
    pext[hist:hist + tt, :] = u
    e = pext[...]
    wins = []
    shift = 1
    for g in range(len(POOL_WINDOWS)):
        e = e + pltpu.roll(e, shift, 0)
        wins.append(e[hist:hist + tt, :POOL_GROUP_DIM])
        if g + 1 < len(POOL_WINDOWS):
            e = e[:, POOL_GROUP_DIM:]
        shift *= 2
    pext[0:hist, :] = pext[tt:tt + hist, :]

    if start >= POOL_BUF:
        inv_cnt = [1.0 / w for w in POOL_WINDOWS]
    else:
        pos = start + t * tt + lax.broadcasted_iota(I32, (tt, 1), 0)
        inv_cnt = [1.0 / jnp.minimum(w, pos + 1).astype(F32) for w in POOL_WINDOWS]
    yb = _pool_project(wins, u, inv_cnt, pw_ref, ps_ref)

    yab_ref[:, :LRU_WIDTH] = ya.astype(BF16)
    yab_ref[:, LRU_WIDTH:] = yb.astype(BF16)


def _mix_seq(proj, row_block0, batch, seq, tt, start, conv0, h0, pool0, mixw):
    nt = seq // tt
    hist = 2 * SUBLANES
    col = lambda c: pl.BlockSpec((tt, LRU_WIDTH), lambda b, t: (row_block0 + b * nt + t, c))
    state = lambda r: pl.BlockSpec((None, r, LRU_WIDTH), lambda b, t: (b, 0, 0))
    full = lambda a: pl.BlockSpec(a.shape, lambda b, t: (0,) * a.ndim)
    return pl.pallas_call(
        functools.partial(_mix_seq_kernel, tt=tt, start=start),
        grid=(batch, nt),
        in_specs=[col(0), col(1), col(2), state(SUBLANES), state(SUBLANES), state(hist)]
        + [full(w) for w in mixw],
        out_specs=[pl.BlockSpec((tt, D_MODEL), lambda b, t: (b * nt + t, 0)), state(SUBLANES)],
        out_shape=[jax.ShapeDtypeStruct((batch * seq, D_MODEL), BF16),
                   jax.ShapeDtypeStruct((batch, SUBLANES, LRU_WIDTH), F32)],
        scratch_shapes=[pltpu.VMEM((tt + SUBLANES, LRU_WIDTH), F32),
                        pltpu.VMEM((tt + hist, POOL_WIDTH), F32),
                        pltpu.VMEM((SUBLANES, LRU_WIDTH), F32)],
        compiler_params=_params(48, ("arbitrary", "arbitrary")),
        name="mix_seq",
    )(proj, proj, proj, conv0, h0, pool0, *mixw)


def _mix_step_kernel(xa_ref, ga_ref, ub_ref, sconv_ref, sh_ref, spool_ref,
                     cw_ref, cb_ref, wa_ref, ba_ref, wx_ref, bx_ref, lam_ref, pw_ref, ps_ref,
                     yab_ref, h_ref):
    xa = xa_ref[...]
    cw = cw_ref[...]
    xc = cb_ref[...] + cw[CONV_WIDTH - 1:CONV_WIDTH] * xa
    for k in range(1, CONV_WIDTH):
        xc = xc + cw[CONV_WIDTH - 1 - k:CONV_WIDTH - k] * sconv_ref[CONV_WIDTH - 1 - k]
    a, b = _lru_coeffs(xc, wa_ref, ba_ref, wx_ref, bx_ref, lam_ref)
    h = a * sh_ref[...] + b
    h_ref[...] = h
    ya = h * jax.nn.gelu(ga_ref[...])

    u = ub_ref[...]
    wins = []
    for g, w in enumerate(POOL_WINDOWS):
        sl = slice(g * POOL_GROUP_DIM, (g + 1) * POOL_GROUP_DIM)
        acc = u[:, sl]
        for k in range(1, w):
            acc = acc + spool_ref[POOL_BUF - k, :, sl]
        wins.append(acc)
    yb = _pool_project(wins, u, [1.0 / w for w in POOL_WINDOWS], pw_ref, ps_ref)
    yab_ref[:, :LRU_WIDTH] = ya.astype(BF16)
    yab_ref[:, LRU_WIDTH:] = yb.astype(BF16)


def _mix_step(proj, row_block, rows, sconv_t, sh, spool_t, mixw):
    col = lambda c: pl.BlockSpec((rows, LRU_WIDTH), lambda i: (row_block, c))
    full = lambda a: pl.BlockSpec(a.shape, lambda i: (0,) * a.ndim)
    return pl.pallas_call(
        _mix_step_kernel,
        grid=(1,),
        in_specs=[col(0), col(1), col(2), full(sconv_t), full(sh), full(spool_t)] + [full(w) for w in mixw],
        out_specs=[pl.BlockSpec((rows, D_MODEL), lambda i: (0, 0)),
                   pl.BlockSpec((rows, LRU_WIDTH), lambda i: (0, 0))],
        out_shape=[jax.ShapeDtypeStruct((rows, D_MODEL), BF16),
                   jax.ShapeDtypeStruct((rows, LRU_WIDTH), F32)],
        compiler_params=_params(48, ("arbitrary",)),
        name="mix_step",
    )(proj, proj, proj, sconv_t, sh, spool_t, *mixw)


def _out_kernel(yp_ref, ys_ref, xp_ref, xs_ref, wo_ref, g2_ref, rwt_ref, rb_ref,
                h1_ref, xq_ref, eid_ref, wgt_ref, *, n_prompt_blocks, tile):
    i = pl.program_id(0)
    is_p = i < n_prompt_blocks
    yab = jnp.where(is_p, yp_ref[...], ys_ref[...])
    x = jnp.where(is_p, xp_ref[...], xs_ref[...])
    h1 = x + jnp.dot(yab, wo_ref[...], preferred_element_type=F32)
    h1_ref[...] = h1
    xn = _rmsnorm(h1, g2_ref[...])

    bits = pltpu.bitcast(xn.astype(BF16).astype(F32), U32)
    half = D_MODEL // 2
    xq_ref[...] = (bits[:, :half] >> 16) | (bits[:, half:] & jnp.uint32(0xFFFF0000))

    logits = lax.dot_general(rwt_ref[...], xn, (((1,), (1,)), ((), ())),
                             precision=lax.Precision.HIGHEST, preferred_element_type=F32) + rb_ref[...]
    eidx = lax.broadcasted_iota(I32, (N_EXPERTS, tile), 0)
    vals, ids = [], []
    for _ in range(TOP_K):
        m = jnp.max(logits, axis=0, keepdims=True)
        idx = jnp.min(jnp.where(logits == m, eidx, N_EXPERTS), axis=0, keepdims=True)
        vals.append(m)
        ids.append(idx)
        logits = jnp.where(eidx == idx, -jnp.inf, logits)
    ex = [jnp.exp(v - vals[0]) for v in vals]
    tot = ex[0]
    for e_ in ex[1:]:
        tot = tot + e_
    eid_ref[...] = jnp.concatenate(ids, axis=0)
    wgt_ref[...] = jnp.concatenate([e_ / tot for e_ in ex], axis=0)


def _out(yab_p, yab_s, xp, xs, wo_bf16, g2, rwt, rb, tile):
    npb = xp.shape[0] // tile
    nsb = xs.shape[0] // tile
    n = (npb + nsb) * tile
    pblk = lambda i: (jnp.minimum(i, npb - 1), 0)
    sblk = lambda i: (jnp.maximum(i - npb, 0), 0)
    const = lambda i: (0, 0)
    return pl.pallas_call(
        functools.partial(_out_kernel, n_prompt_blocks=npb, tile=tile),
        grid=(npb + nsb,),
        in_specs=[
            pl.BlockSpec((tile, D_MODEL), pblk), pl.BlockSpec((tile, D_MODEL), sblk),
            pl.BlockSpec((tile, D_MODEL), pblk), pl.BlockSpec((tile, D_MODEL), sblk),
            pl.BlockSpec((D_MODEL, D_MODEL), const), pl.BlockSpec((1, D_MODEL), const),
            pl.BlockSpec((N_EXPERTS, D_MODEL), const), pl.BlockSpec((N_EXPERTS, 1), const),
        ],
        out_specs=[
            pl.BlockSpec((tile, D_MODEL), lambda i: (i, 0)),
            pl.BlockSpec((tile, D_MODEL // 2), lambda i: (i, 0)),
            pl.BlockSpec((TOP_K, tile), lambda i: (0, i)),
            pl.BlockSpec((TOP_K, tile), lambda i: (0, i)),
        ],
        out_shape=[
            jax.ShapeDtypeStruct((n, D_MODEL), F32),
            jax.ShapeDtypeStruct((n, D_MODEL // 2), U32),
            jax.ShapeDtypeStruct((TOP_K, n), I32),
            jax.ShapeDtypeStruct((TOP_K, n), F32),
        ],
        compiler_params=_params(48, ("arbitrary",)),
        name="out_router",
    )(yab_p, yab_s, xp, xs, wo_bf16, g2, rwt, rb)


def _moe_kernel(ge_ref, gnt_ref, ng_ref, src_ref,
                xq_hbm, wg_ref, wu_ref, wd_ref, bg_ref, bu_ref, bd_ref,
                o_ref, xg, xb, wgb, wub, wdb, gsem, *, n_f):
    g = pl.program_id(0)
    f = pl.program_id(1)
    active = g < ng_ref[0]
    half = D_MODEL // 2

    def row_copy(grp, i):
        tok = src_ref[grp * MOE_GROUP_ROWS + i]
        return pltpu.make_async_copy(xq_hbm.at[pl.ds(tok, 1)], xg.at[pl.ds(i, 1)], gsem.at[0])

    def issue(grp):
        def body(i, c):
            row_copy(grp, i).start()
            return c
        lax.fori_loop(0, gnt_ref[grp] * MOE_TM, body, 0)

    def wait(grp):
        def body(i, c):
            row_copy(grp, i).wait()
            return c
        lax.fori_loop(0, gnt_ref[grp] * MOE_TM, body, 0)

    @pl.when(jnp.logical_and(active, f == 0))
    def _():
        @pl.when(g == 0)
        def _():
            issue(0)
        wait(g)

        def unpack(j, c):
            rows = pl.ds(pl.multiple_of(j * MOE_TM, MOE_TM), MOE_TM)
            w = xg[rows, :]
            xb[rows, :half] = pltpu.bitcast(w << 16, F32).astype(BF16)
            xb[rows, half:] = pltpu.bitcast(w & jnp.uint32(0xFFFF0000), F32).astype(BF16)
            return c
        lax.fori_loop(0, gnt_ref[g], unpack, 0)

        @pl.when(g + 1 < ng_ref[0])
        def _():
            issue(g + 1)
        o_ref[...] = jnp.broadcast_to(bd_ref[...], o_ref.shape)

    @pl.when(active)
    def _():
        wgb[...] = wg_ref[...].astype(BF16)
        wub[...] = wu_ref[...].astype(BF16)
        wdb[...] = wd_ref[...].astype(BF16)

        def tile_body(j, c):
            rows = pl.ds(pl.multiple_of(j * MOE_TM, MOE_TM), MOE_TM)
            x = xb[rows, :]
            gg = jnp.dot(x, wgb[...], preferred_element_type=F32) + bg_ref[...]
            uu = jnp.dot(x, wub[...], preferred_element_type=F32) + bu_ref[...]
            gg = jnp.minimum(gg, SWIGLU_LIMIT)
            uu = jnp.clip(uu, -SWIGLU_LIMIT, SWIGLU_LIMIT)
            hdn = gg * jax.nn.sigmoid(SWIGLU_ALPHA * gg) * (uu + 1.0)
            o_ref[rows, :] += jnp.dot(hdn.astype(BF16), wdb[...], preferred_element_type=F32)
            return c
        lax.fori_loop(0, gnt_ref[g], tile_body, 0)


def _moe(ge, gnt, ng, src, xq, wg, wu, wd, bg, bu, bd, n_groups_max):
    n_f = D_FF // MOE_TF

    def fidx(g, f, ng_ref):
        return jnp.where(g < ng_ref[0], f, n_f - 1)

    grid_spec = pltpu.PrefetchScalarGridSpec(
        num_scalar_prefetch=4,
        grid=(n_groups_max, n_f),
        in_specs=[
            pl.BlockSpec(memory_space=pl.ANY),
            pl.BlockSpec((None, D_MODEL, MOE_TF), lambda g, f, ge, gnt, ng, src: (ge[g], 0, fidx(g, f, ng))),
            pl.BlockSpec((None, D_MODEL, MOE_TF), lambda g, f, ge, gnt, ng, src: (ge[g], 0, fidx(g, f, ng))),
            pl.BlockSpec((None, MOE_TF, D_MODEL), lambda g, f, ge, gnt, ng, src: (ge[g], fidx(g, f, ng), 0)),
            pl.BlockSpec((None, 1, MOE_TF), lambda g, f, ge, gnt, ng, src: (ge[g], 0, fidx(g, f, ng))),
            pl.BlockSpec((None, 1, MOE_TF), lambda g, f, ge, gnt, ng, src: (ge[g], 0, fidx(g, f, ng))),
            pl.BlockSpec((None, 1, D_MODEL), lambda g, f, ge, gnt, ng, src: (ge[g], 0, 0)),
        ],
        out_specs=pl.BlockSpec((MOE_GROUP_ROWS, D_MODEL),
                               lambda g, f, ge, gnt, ng, src: (jnp.minimum(g, ng[0] - 1), 0)),
        scratch_shapes=[
            pltpu.VMEM((MOE_GROUP_ROWS, D_MODEL // 2), U32),
            pltpu.VMEM((MOE_GROUP_ROWS, D_MODEL), BF16),
            pltpu.VMEM((D_MODEL, MOE_TF), BF16),
            pltpu.VMEM((D_MODEL, MOE_TF), BF16),
            pltpu.VMEM((MOE_TF, D_MODEL), BF16),
            pltpu.SemaphoreType.DMA((1,)),
        ],
    )
    return pl.pallas_call(
        functools.partial(_moe_kernel, n_f=n_f),
        grid_spec=grid_spec,
        out_shape=jax.ShapeDtypeStruct((n_groups_max * MOE_GROUP_ROWS, D_MODEL), F32),
        compiler_params=_params(58, ("arbitrary", "arbitrary")),
        name="moe_ffn",
    )(ge, gnt, ng, src, xq, wg, wu, wd, bg, bu, bd)


def _combine_kernel(pos_ref, ys_hbm, h1_ref, w_ref, fn_ref, yp_ref, ysm_ref, buf, sem,
                    *, n_tokens, n_prompt_blocks, tile):
    i = pl.program_id(0)
    n_steps = pl.num_programs(0)
    slot = i % 2
    n_rows = TOP_K * tile

    def row_copy(step, sl, r):
        k = r // tile
        p = pos_ref[k * n_tokens + step * tile + (r - k * tile)]
        return pltpu.make_async_copy(ys_hbm.at[pl.ds(p, 1)], buf.at[sl, pl.ds(r, 1)], sem.at[sl])

    def issue(step, sl):
        def body(r, c):
            row_copy(step, sl, r).start()
            return c
        lax.fori_loop(0, n_rows, body, 0)

    @pl.when(i == 0)
    def _():
        issue(0, 0)

    @pl.when(i + 1 < n_steps)
    def _():
        issue(i + 1, 1 - slot)

    def wait_body(r, c):
        row_copy(i, slot, r).wait()
        return c
    lax.fori_loop(0, n_rows, wait_body, 0)

    acc = h1_ref[...]
    w = w_ref[...]
    for k in range(TOP_K):
        acc = acc + w[:, k:k + 1] * buf[slot, k * tile:(k + 1) * tile, :]
    y = _rmsnorm(acc, fn_ref[...])

    @pl.when(i < n_prompt_blocks)
    def _():
        yp_ref[...] = y

    @pl.when(i >= n_prompt_blocks)
    def _():
        ysm_ref[...] = y


def _combine(pos_flat, ys, h1, wgt_t, fn, n_prompt, tile):
    n = h1.shape[0]
    npb = n_prompt // tile
    nsb = (n - n_prompt) // tile
    grid_spec = pltpu.PrefetchScalarGridSpec(
        num_scalar_prefetch=1,
        grid=(npb + nsb,),
        in_specs=[
            pl.BlockSpec(memory_space=pl.ANY),
            pl.BlockSpec((tile, D_MODEL), lambda i, pos: (i, 0)),
            pl.BlockSpec((tile, TOP_K), lambda i, pos: (i, 0)),
            pl.BlockSpec((1, D_MODEL), lambda i, pos: (0, 0)),
        ],
        out_specs=[
            pl.BlockSpec((tile, D_MODEL), lambda i, pos: (jnp.minimum(i, npb - 1), 0)),
            pl.BlockSpec((tile, D_MODEL), lambda i, pos: (jnp.maximum(i - npb, 0), 0)),
        ],
        scratch_shapes=[pltpu.VMEM((2, TOP_K * tile, D_MODEL), F32), pltpu.SemaphoreType.DMA((2,))],
    )
    return pl.pallas_call(
        functools.partial(_combine_kernel, n_tokens=n, n_prompt_blocks=npb, tile=tile),
        grid_spec=grid_spec,
        out_shape=[jax.ShapeDtypeStruct((n_prompt, D_MODEL), F32),
                   jax.ShapeDtypeStruct((n - n_prompt, D_MODEL), F32)],
        compiler_params=_params(32, ("arbitrary",)),
        name="combine",
    )(pos_flat, ys, h1, wgt_t, fn)


def _routing_tables(eid, n_groups_max):
    n = eid.shape[1]
    eid_t = eid.T
    member = jnp.any(eid_t[:, :, None] == jnp.arange(N_EXPERTS, dtype=I32)[None, None, :], axis=1).astype(I32)
    before = jnp.cumsum(member, axis=0) - member
    counts = jnp.sum(member, axis=0)
    rank = jnp.take_along_axis(before, eid_t, axis=1)
    groups_e = (counts + MOE_GROUP_ROWS - 1) // MOE_GROUP_ROWS
    gend_e = jnp.cumsum(groups_e)
    gstart_e = gend_e - groups_e
    pos = gstart_e[eid_t] * MOE_GROUP_ROWS + rank
    n_groups = gend_e[-1]

    j = jnp.arange(n_groups_max, dtype=I32)
    e_j = jnp.minimum(jnp.searchsorted(gend_e, j, side="right"), N_EXPERTS - 1).astype(I32)
    e_last = jnp.minimum(jnp.searchsorted(gend_e, n_groups - 1, side="right"), N_EXPERTS - 1).astype(I32)
    act = j < n_groups
    e_j = jnp.where(act, e_j, e_last)
    rows_j = jnp.clip(counts[e_j] - (j - gstart_e[e_j]) * MOE_GROUP_ROWS, 0, MOE_GROUP_ROWS)
    ntiles_j = jnp.where(act, (rows_j + MOE_TM - 1) // MOE_TM, 0).astype(I32)

    pos_flat = pos.T.reshape(-1).astype(I32)
    tok = jnp.tile(jnp.arange(n, dtype=I32), TOP_K)
    src = jnp.zeros((n_groups_max * MOE_GROUP_ROWS,), I32).at[pos_flat].set(tok)
    return e_j, ntiles_j, n_groups.reshape(1).astype(I32), src, pos_flat


def _block_diag(w, per_block):
    h, d, _ = w.shape
    nb = h // per_block
    eye = jnp.eye(per_block, dtype=w.dtype)
    w4 = w.reshape(nb, per_block, d, d)
    out = jnp.einsum("bpij,pq->bpiqj", w4, eye)
    return out.reshape(nb, per_block * d, per_block * d)


def kernel(x_prompt, x_sample, state_conv, state_rglru, state_pool, meta_tokens, norm1, w_in, conv_w, conv_b, lru_wa, lru_ba, lru_wx, lru_bx, lru_lambda, pool_w, pool_scale, w_out, norm2, router_w, router_b, exp_wg, exp_bg, exp_wu, exp_bu, exp_wd, exp_bd, final_norm):
    batch, seq, _ = x_prompt.shape
    dec = x_sample.shape[0]
    n_prompt = batch * seq
    n_tok = n_prompt + dec
    l = 0
    row = lambda v: v.reshape(1, -1)

    xp = x_prompt.reshape(n_prompt, D_MODEL)
    xs = x_sample.reshape(dec, D_MODEL)
    w_in_b = w_in[l].astype(BF16)
    w_out_b = w_out[l].astype(BF16)
    heads_per_block = V7X_MXU_DIM // LRU_HEAD_DIM
    mixw = (conv_w[l], row(conv_b[l]),
            _block_diag(lru_wa[l], heads_per_block).astype(BF16), row(lru_ba[l]),
            _block_diag(lru_wx[l], heads_per_block).astype(BF16), row(lru_bx[l]),
            row(lru_lambda[l]), pool_w[l].astype(BF16), row(pool_scale[l]))
    g1 = row(norm1[l])

    proj_m = _proj(meta_tokens, None, g1, w_in_b, N_META)
    zeros = lambda r: jnp.zeros((1, r, LRU_WIDTH), F32)
    _, h_meta = _mix_seq(proj_m, 0, 1, N_META, N_META, 0, zeros(SUBLANES), zeros(SUBLANES),
                         zeros(2 * SUBLANES), mixw)
    conv0 = jnp.broadcast_to(proj_m[None, N_META - SUBLANES:, :LRU_WIDTH], (batch, SUBLANES, LRU_WIDTH))
    pool0 = jnp.broadcast_to(proj_m[None, :, 2 * LRU_WIDTH:], (batch, N_META, POOL_WIDTH))
    h0 = jnp.broadcast_to(h_meta, (batch, SUBLANES, LRU_WIDTH))

    proj = _proj(xp, xs, g1, w_in_b, ROW_TILE)
    yab_p, h_p = _mix_seq(proj, 0, batch, seq, TIME_TILE, N_META, conv0, h0, pool0, mixw)
    sconv_t = jnp.swapaxes(state_conv[l], 0, 1)
    spool_t = jnp.swapaxes(state_pool[l], 0, 1)
    yab_s, h_s = _mix_step(proj, n_prompt // dec, dec, sconv_t, state_rglru[l], spool_t, mixw)

    h1, xq, eid, wgt = _out(yab_p, yab_s, xp, xs, w_out_b, row(norm2[l]),
                            router_w[l].T, router_b[l].reshape(N_EXPERTS, 1), ROW_TILE)

    n_groups_max = N_EXPERTS + (TOP_K * n_tok) // MOE_GROUP_ROWS
    ge, gnt, ng, src, pos_flat = _routing_tables(eid, n_groups_max)
    ys = _moe(ge, gnt, ng, src, xq, exp_wg[l], exp_wu[l], exp_wd[l],
              exp_bg[l].reshape(N_EXPERTS, 1, D_FF), exp_bu[l].reshape(N_EXPERTS, 1, D_FF),
              exp_bd[l].reshape(N_EXPERTS, 1, D_MODEL), n_groups_max)
    y_p, y_s = _combine(pos_flat, ys, h1, wgt.T, row(final_norm), n_prompt, COMBINE_TILE)

    proj_p = proj[:n_prompt].reshape(batch, seq, IN_WIDTH)
    proj_s = proj[n_prompt:]
    conv_p = proj_p[:, seq - (CONV_WIDTH - 1):, :LRU_WIDTH]
    pool_p = proj_p[:, seq - POOL_BUF:, 2 * LRU_WIDTH:]
    conv_s = jnp.concatenate([state_conv[l][:, 1:], proj_s[:, None, :LRU_WIDTH]], axis=1)
    pool_s = jnp.concatenate([state_pool[l][:, 1:], proj_s[:, None, 2 * LRU_WIDTH:]], axis=1)
    return (y_p.reshape(batch, seq, D_MODEL), y_s.reshape(dec, 1, D_MODEL),
            conv_p[None], h_p[None, :, SUBLANES - 1], pool_p[None],
            conv_s[None], h_s[None], pool_s[None])
```

```python
import functools

import numpy as np
import jax
import jax.numpy as jnp
from jax import lax
from jax.experimental import pallas as pl
from jax.experimental.pallas import tpu as pltpu

F32 = jnp.float32
BF16 = jnp.bfloat16
I32 = jnp.int32
U32 = jnp.uint32

D_MODEL = 2048
N_META = 16
LRU_WIDTH = 1024
LRU_HEADS = 16
LRU_HEAD_DIM = LRU_WIDTH // LRU_HEADS
CONV_WIDTH = 4
LRU_C = 8.0
POOL_WIDTH = D_MODEL - LRU_WIDTH
POOL_WINDOWS = (2, 4, 8, 16)
POOL_GROUP_DIM = POOL_WIDTH // len(POOL_WINDOWS)
POOL_BUF = max(POOL_WINDOWS) - 1
IN_WIDTH = 2 * LRU_WIDTH + POOL_WIDTH
N_EXPERTS = 32
TOP_K = 4
D_FF = D_MODEL
SWIGLU_ALPHA = 1.702
SWIGLU_LIMIT = 7.0
RMS_EPS = 1e-6
HALF = D_MODEL // 2

V7X_MXU_DIM = 256
SUBLANES = 8

PROMPT_TILE = 512
TIME_TILE = 256
MOE_TM = 256
MOE_TILES_PER_GROUP = 5
MOE_GROUP_ROWS = MOE_TM * MOE_TILES_PER_GROUP
MOE_TF = V7X_MXU_DIM
PERM_CHUNK = V7X_MXU_DIM


def _params(vmem_mb, sem):
    return pltpu.CompilerParams(dimension_semantics=sem, vmem_limit_bytes=vmem_mb << 20)


def _rmsnorm(x, g):
    return x * lax.rsqrt(jnp.mean(x * x, axis=-1, keepdims=True) + RMS_EPS) * g


def _resident(shape):
    return pl.BlockSpec(shape, lambda *_: (0,) * len(shape), pipeline_mode=pl.Buffered(1))


def _pack_bf16_pairs(x):
    bits = pltpu.bitcast(x, U32)
    return (bits[:, :HALF] >> 16) | (bits[:, HALF:] & jnp.uint32(0xFFFF0000))


def _unpack_bf16_pairs(w):
    lo = pltpu.bitcast(w << 16, F32).astype(BF16)
    hi = pltpu.bitcast(w & jnp.uint32(0xFFFF0000), F32).astype(BF16)
    return lo, hi


def _proj_kernel(x_ref, g_ref, w_ref, o_ref):
    xn = _rmsnorm(x_ref[...], g_ref[...])
    o_ref[...] = jnp.dot(xn.astype(BF16), w_ref[...], preferred_element_type=F32)


def _proj(x, g, w_bf16, tile):
    rows = x.shape[0]
    return pl.pallas_call(
        _proj_kernel,
        grid=(rows // tile,),
        in_specs=[pl.BlockSpec((tile, D_MODEL), lambda i: (i, 0)),
                  _resident((1, D_MODEL)), _resident((D_MODEL, IN_WIDTH))],
        out_specs=pl.BlockSpec((tile, IN_WIDTH), lambda i: (i, 0)),
        out_shape=jax.ShapeDtypeStruct((rows, IN_WIDTH), F32),
        compiler_params=_params(48, ("arbitrary",)),
        name="proj",
    )(x, g, w_bf16)


def _lru_coeffs(xc, wa_ref, ba_ref, wx_ref, bx_ref, lam_ref):
    xcb = xc.astype(BF16)
    nq = LRU_WIDTH // V7X_MXU_DIM
    rs, gs = [], []
    for q in range(nq):
        blk = xcb[:, q * V7X_MXU_DIM:(q + 1) * V7X_MXU_DIM]
        rs.append(jnp.dot(blk, wa_ref[q], preferred_element_type=F32))
        gs.append(jnp.dot(blk, wx_ref[q], preferred_element_type=F32))
    r = jax.nn.sigmoid(jnp.concatenate(rs, axis=-1) + ba_ref[...])
    i = jax.nn.sigmoid(jnp.concatenate(gs, axis=-1) + bx_ref[...])
    log_a = -LRU_C * r * jax.nn.softplus(-lam_ref[...])
    a = jnp.exp(log_a)
    th = jnp.tanh(log_a)
    beta = jnp.sqrt(-2.0 * th / (1.0 - th))
    return a, beta * i * xc


def _pool_project(wins, u, inv_cnt, pw_ref, ps_ref):
    outs = []
    for g in range(len(POOL_WINDOWS)):
        sl = slice(g * POOL_GROUP_DIM, (g + 1) * POOL_GROUP_DIM)
        d = (wins[g] * inv_cnt[g] - u[:, sl]).astype(BF16)
        outs.append(jnp.dot(d, pw_ref[g], preferred_element_type=F32))
    return jnp.concatenate(outs, axis=-1) * ps_ref[...]


def _mix_seq_kernel(xa_ref, ga_ref, ub_ref, conv0_ref, h0_ref, pool0_ref,
                    cw_ref, cb_ref, wa_ref, ba_ref, wx_ref, bx_ref, lam_ref, pw_ref, ps_ref,
                    yab_ref, ht_ref, cext, pext, hcar, *, tt, start):
    t = pl.program_id(1)
    hist = 2 * SUBLANES

    @pl.when(t == 0)
    def _():
        cext[0:SUBLANES, :] = conv0_ref[...]
        pext[0:hist, :] = pool0_ref[...]
        hcar[...] = h0_ref[...]

    xa = xa_ref[...]
    cext[SUBLANES:SUBLANES + tt, :] = xa
    cw = cw_ref[...]
    xc = cb_ref[...] + cw[CONV_WIDTH - 1:CONV_WIDTH] * xa
    for k in range(1, CONV_WIDTH):
        xc = xc + cw[CONV_WIDTH - 1 - k:CONV_WIDTH - k] * cext[SUBLANES - k:SUBLANES - k + tt, :]
    cext[0:SUBLANES, :] = cext[tt:tt + SUBLANES, :]

    a, b = _lru_coeffs(xc, wa_ref, ba_ref, wx_ref, bx_ref, lam_ref)

    row = lax.broadcasted_iota(I32, (tt, LRU_WIDTH), 0)
    s = 1
    while s < tt:
        keep = row >= s
        a_prev = jnp.where(keep, pltpu.roll(a, s, 0), 1.0)
        b_prev = jnp.where(keep, pltpu.roll(b, s, 0), 0.0)
        b = b + a * b_prev
        a = a * a_prev
        s *= 2
    hs = a * hcar[SUBLANES - 1:SUBLANES, :] + b
    hcar[...] = hs[tt - SUBLANES:tt, :]
    ht_ref[...] = hs[tt - SUBLANES:tt, :]
    ya = hs * jax.nn.gelu(ga_ref[...])

    u = ub_ref[...]
    pext[hist:hist + tt, :] = u
    e = pext[...]
    wins = []
    shift = 1
    for g in range(len(POOL_WINDOWS)):
        e = e + pltpu.roll(e, shift, 0)
        wins.append(e[hist:hist + tt, :POOL_GROUP_DIM])
        if g + 1 < len(POOL_WINDOWS):
            e = e[:, POOL_GROUP_DIM:]
        shift *= 2
    pext[0:hist, :] = pext[tt:tt + hist, :]

    if start >= POOL_BUF:
        inv_cnt = [1.0 / w for w in POOL_WINDOWS]
    else:
        pos = start + t * tt + lax.broadcasted_iota(I32, (tt, 1), 0)
        inv_cnt = [1.0 / jnp.minimum(w, pos + 1).astype(F32) for w in POOL_WINDOWS]
    yb = _pool_project(wins, u, inv_cnt, pw_ref, ps_ref)

    yab_ref[:, :LRU_WIDTH] = ya.astype(BF16)
    yab_ref[:, LRU_WIDTH:] = yb.astype(BF16)


def _mix_seq(proj, batch, seq, tt, start, conv0, h0, pool0, mixw):
    nt = seq // tt
    hist = 2 * SUBLANES
    col = lambda c: pl.BlockSpec((tt, LRU_WIDTH), lambda b, t: (b * nt + t, c))
    state = lambda r: pl.BlockSpec((None, r, LRU_WIDTH), lambda b, t: (b, 0, 0))
    return pl.pallas_call(
        functools.partial(_mix_seq_kernel, tt=tt, start=start),
        grid=(batch, nt),
        in_specs=[col(0), col(1), col(2), state(SUBLANES), state(SUBLANES), state(hist)]
        + [_resident(w.shape) for w in mixw],
        out_specs=[pl.BlockSpec((tt, D_MODEL), lambda b, t: (b * nt + t, 0)), state(SUBLANES)],
        out_shape=[jax.ShapeDtypeStruct((batch * seq, D_MODEL), BF16),
                   jax.ShapeDtypeStruct((batch, SUBLANES, LRU_WIDTH), F32)],
        scratch_shapes=[pltpu.VMEM((tt + SUBLANES, LRU_WIDTH), F32),
                        pltpu.VMEM((tt + hist, POOL_WIDTH), F32),
                        pltpu.VMEM((SUBLANES, LRU_WIDTH), F32)],
        compiler_params=_params(48, ("arbitrary", "arbitrary")),
        name="mix_seq",
    )(proj, proj, proj, conv0, h0, pool0, *mixw)


def _mix_step_kernel(xa_ref, ga_ref, ub_ref, sconv_ref, sh_ref, spool_ref,
                     cw_ref, cb_ref, wa_ref, ba_ref, wx_ref, bx_ref, lam_ref, pw_ref, ps_ref,
                     yab_ref, h_ref):
    xa = xa_ref[...]
    cw = cw_ref[...]
    xc = cb_ref[...] + cw[CONV_WIDTH - 1:CONV_WIDTH] * xa
    for k in range(1, CONV_WIDTH):
        xc = xc + cw[CONV_WIDTH - 1 - k:CONV_WIDTH - k] * sconv_ref[CONV_WIDTH - 1 - k]
    a, b = _lru_coeffs(xc, wa_ref, ba_ref, wx_ref, bx_ref, lam_ref)
    h = a * sh_ref[...] + b
    h_ref[...] = h
    ya = h * jax.nn.gelu(ga_ref[...])

    u = ub_ref[...]
    wins = []
    for g, w in enumerate(POOL_WINDOWS):
        sl = slice(g * POOL_GROUP_DIM, (g + 1) * POOL_GROUP_DIM)
        acc = u[:, sl]
        for k in range(1, w):
            acc = acc + spool_ref[POOL_BUF - k, :, sl]
        wins.append(acc)
    yb = _pool_project(wins, u, [1.0 / w for w in POOL_WINDOWS], pw_ref, ps_ref)
    yab_ref[:, :LRU_WIDTH] = ya.astype(BF16)
    yab_ref[:, LRU_WIDTH:] = yb.astype(BF16)


def _mix_step(proj, sconv_t, sh, spool_t, mixw):
    rows = proj.shape[0]
    col = lambda c: pl.BlockSpec((rows, LRU_WIDTH), lambda i: (0, c))
    full = lambda a: pl.BlockSpec(a.shape, lambda i: (0,) * a.ndim)
    return pl.pallas_call(
        _mix_step_kernel,
        grid=(1,),
        in_specs=[col(0), col(1), col(2), full(sconv_t), full(sh), full(spool_t)] + [full(w) for w in mixw],
        out_specs=[pl.BlockSpec((rows, D_MODEL), lambda i: (0, 0)),
                   pl.BlockSpec((rows, LRU_WIDTH), lambda i: (0, 0))],
        out_shape=[jax.ShapeDtypeStruct((rows, D_MODEL), BF16),
                   jax.ShapeDtypeStruct((rows, LRU_WIDTH), F32)],
        compiler_params=_params(48, ("arbitrary",)),
        name="mix_step",
    )(proj, proj, proj, sconv_t, sh, spool_t, *mixw)


def _out_kernel(yab_ref, x_ref, wo_ref, g2_ref, rwt_ref, rb_ref, h1_ref, xn_ref, eid_ref, wgt_ref, *, tile):
    h1 = x_ref[...] + jnp.dot(yab_ref[...], wo_ref[...], preferred_element_type=F32)
    h1_ref[...] = h1
    xn = _rmsnorm(h1, g2_ref[...])
    xn_ref[...] = xn.astype(BF16)

    logits = lax.dot_general(rwt_ref[...], xn, (((1,), (1,)), ((), ())),
                             precision=lax.Precision.HIGHEST, preferred_element_type=F32) + rb_ref[...]
    eidx = lax.broadcasted_iota(I32, (N_EXPERTS, tile), 0)
    vals, ids = [], []
    for _ in range(TOP_K):
        m = jnp.max(logits, axis=0, keepdims=True)
        idx = jnp.min(jnp.where(logits == m, eidx, N_EXPERTS), axis=0, keepdims=True)
        vals.append(m)
        ids.append(idx)
        logits = jnp.where(eidx == idx, -jnp.inf, logits)
    ex = [jnp.exp(v - vals[0]) for v in vals]
    tot = ex[0]
    for e_ in ex[1:]:
        tot = tot + e_
    eid_ref[...] = jnp.concatenate(ids, axis=0)
    wgt_ref[...] = jnp.concatenate([e_ / tot for e_ in ex], axis=0)


def _out(yab, x, wo_bf16, g2, rwt, rb, tile):
    n = x.shape[0]
    rowblk = lambda w: pl.BlockSpec((tile, w), lambda i: (i, 0))
    colblk = pl.BlockSpec((TOP_K, tile), lambda i: (0, i))
    return pl.pallas_call(
        functools.partial(_out_kernel, tile=tile),
        grid=(n // tile,),
        in_specs=[rowblk(D_MODEL), rowblk(D_MODEL), _resident((D_MODEL, D_MODEL)), _resident((1, D_MODEL)),
                  _resident((N_EXPERTS, D_MODEL)), _resident((N_EXPERTS, 1))],
        out_specs=[rowblk(D_MODEL), rowblk(D_MODEL), colblk, colblk],
        out_shape=[jax.ShapeDtypeStruct((n, D_MODEL), F32), jax.ShapeDtypeStruct((n, D_MODEL), BF16),
                   jax.ShapeDtypeStruct((TOP_K, n), I32), jax.ShapeDtypeStruct((TOP_K, n), F32)],
        compiler_params=_params(48, ("arbitrary",)),
        name="out_router",
    )(yab, x, wo_bf16, g2, rwt, rb)


def _for_segments(tile, seg_len_ref, fn):
    def per_expert(e, c):
        s = tile * N_EXPERTS + e

        def per_chunk(j, c2):
            fn(s, j)
            return c2
        lax.fori_loop(0, seg_len_ref[s] // SUBLANES, per_chunk, 0)
        return c
    lax.fori_loop(0, N_EXPERTS, per_expert, 0)


def _seg_rows(ref, start, j):
    return ref.at[pl.ds(pl.multiple_of(start + j * SUBLANES, SUBLANES), SUBLANES)]


def _dispatch_kernel(seg_loc_ref, seg_dst_ref, seg_len_ref, tail_dst_ref, tail_len_ref,
                     xn_ref, dest_ref, *rest, tile0, kt, zero_tails):
    xs_hbm, cb, zb, sem, zsem = rest[-5:]
    i = pl.program_id(0)
    n_steps = pl.num_programs(0)
    slot = i % 2
    tile = tile0 + i
    tt = xn_ref.shape[0]

    def seg_copy(sl, s, j):
        return pltpu.make_async_copy(_seg_rows(cb.at[sl], seg_loc_ref[s], j),
                                     _seg_rows(xs_hbm, seg_dst_ref[s], j), sem.at[sl])

    @pl.when(i >= 2)
    def _():
        _for_segments(tile - 2, seg_len_ref, lambda s, j: seg_copy(slot, s, j).wait())

    x = xn_ref[...]
    d = [dest_ref[k:k + 1, :] for k in range(TOP_K)]
    for c0 in range(0, kt, PERM_CHUNK):
        r = c0 + lax.broadcasted_iota(I32, (PERM_CHUNK, tt), 0)
        p = jnp.zeros((PERM_CHUNK, tt), F32)
        for k in range(TOP_K):
            p = jnp.where(d[k] == r, 1.0, p)
        rows = jnp.dot(p.astype(BF16), x, preferred_element_type=F32)
        cb[slot, c0:c0 + PERM_CHUNK, :] = _pack_bf16_pairs(rows)

    _for_segments(tile, seg_len_ref, lambda s, j: seg_copy(slot, s, j).start())

    @pl.when(i == n_steps - 1)
    def _():
        @pl.when(i >= 1)
        def _():
            _for_segments(tile - 1, seg_len_ref, lambda s, j: seg_copy(1 - slot, s, j).wait())
        _for_segments(tile, seg_len_ref, lambda s, j: seg_copy(slot, s, j).wait())

    if zero_tails:
        zb[...] = jnp.zeros(zb.shape, U32)

        def tail_copy(e, j):
            return pltpu.make_async_copy(zb, _seg_rows(xs_hbm, tail_dst_ref[e], j), zsem.at[0])

        def tails(method):
            def per_expert(e, c):
                def per_chunk(j, c2):
                    getattr(tail_copy(e, j), method)()
                    return c2
                lax.fori_loop(0, tail_len_ref[e] // SUBLANES, per_chunk, 0)
                return c
            lax.fori_loop(0, N_EXPERTS, per_expert, 0)

        @pl.when(i == n_steps - 1)
        def _():
            tails("start")
            tails("wait")


def _dispatch(tables, xn, dest, xs, tile0, tt, kt, xs_rows, zero_tails):
    n = xn.shape[0]
    in_specs = [pl.BlockSpec((tt, D_MODEL), lambda i, *_: (i, 0)),
                pl.BlockSpec((TOP_K, tt), lambda i, *_: (0, i))]
    args = [*tables, xn, dest]
    aliases = {}
    if xs is not None:
        in_specs.append(pl.BlockSpec(memory_space=pl.ANY))
        aliases = {len(args): 0}
        args.append(xs)
    grid_spec = pltpu.PrefetchScalarGridSpec(
        num_scalar_prefetch=len(tables),
        grid=(n // tt,),
        in_specs=in_specs,
        out_specs=pl.BlockSpec(memory_space=pl.ANY),
        scratch_shapes=[pltpu.VMEM((2, kt, HALF), U32), pltpu.VMEM((SUBLANES, HALF), U32),
                        pltpu.SemaphoreType.DMA((2,)), pltpu.SemaphoreType.DMA((1,))],
    )
    return pl.pallas_call(
        functools.partial(_dispatch_kernel, tile0=tile0, kt=kt, zero_tails=zero_tails),
        grid_spec=grid_spec,
        out_shape=jax.ShapeDtypeStruct((xs_rows, HALF), U32),
        input_output_aliases=aliases,
        compiler_params=_params(48, ("arbitrary",)),
        name="dispatch",
    )(*args)


def _moe_kernel(ge_ref, grow_ref, gnt_ref, ng_ref,
                xs_hbm, wg_ref, wu_ref, wd_ref, bg_ref, bu_ref, bd_ref,
                ys_hbm, xg, xb, acc, ypk, wgb, wub, wdb, gsem, osem, *, n_f):
    g = pl.program_id(0)
    f = pl.program_id(1)
    n_groups = ng_ref[0]
    active = g < n_groups

    def tile_rows(j):
        return pl.ds(pl.multiple_of(j * MOE_TM, MOE_TM), MOE_TM)

    def hbm_rows(ref, grp, j):
        return ref.at[pl.ds(pl.multiple_of(grow_ref[grp] + j * MOE_TM, MOE_TM), MOE_TM)]

    def copy_in(grp, j):
        return pltpu.make_async_copy(hbm_rows(xs_hbm, grp, j), xg.at[tile_rows(j)], gsem.at[0])

    def copy_out(grp, j):
        return pltpu.make_async_copy(ypk.at[tile_rows(j)], hbm_rows(ys_hbm, grp, j), osem.at[0])

    def for_tiles(grp, fn):
        def body(j, c):
            fn(grp, j)
            return c
        lax.fori_loop(0, gnt_ref[grp], body, 0)

    @pl.when(jnp.logical_and(active, f == 0))
    def _():
        @pl.when(g == 0)
        def _():
            for_tiles(0, lambda grp, j: copy_in(grp, j).start())
        for_tiles(g, lambda grp, j: copy_in(grp, j).wait())

        def unpack(grp, j):
            lo, hi = _unpack_bf16_pairs(xg[tile_rows(j), :])
            xb[tile_rows(j), :HALF] = lo
            xb[tile_rows(j), HALF:] = hi
            acc[tile_rows(j), :] = jnp.broadcast_to(bd_ref[...], (MOE_TM, D_MODEL))
        for_tiles(g, unpack)

        @pl.when(g + 1 < n_groups)
        def _():
            for_tiles(g + 1, lambda grp, j: copy_in(grp, j).start())

    def ffn_tile(j):
        x = xb[tile_rows(j), :]
        gg = jnp.dot(x, wgb[...], preferred_element_type=F32) + bg_ref[...]
        uu = jnp.dot(x, wub[...], preferred_element_type=F32) + bu_ref[...]
        gg = jnp.minimum(gg, SWIGLU_LIMIT)
        uu = jnp.clip(uu, -SWIGLU_LIMIT, SWIGLU_LIMIT)
        hdn = gg * jax.nn.sigmoid(SWIGLU_ALPHA * gg) * (uu + 1.0)
        return jnp.dot(hdn.astype(BF16), wdb[...], preferred_element_type=F32)

    @pl.when(active)
    def _():
        wgb[...] = wg_ref[...].astype(BF16)
        wub[...] = wu_ref[...].astype(BF16)
        wdb[...] = wd_ref[...].astype(BF16)

        @pl.when(f < n_f - 1)
        def _():
            def accumulate(grp, j):
                acc[tile_rows(j), :] += ffn_tile(j)
            for_tiles(g, accumulate)

        @pl.when(f == n_f - 1)
        def _():
            @pl.when(g > 0)
            def _():
                for_tiles(g - 1, lambda grp, j: copy_out(grp, j).wait())

            def finish(grp, j):
                y = acc[tile_rows(j), :] + ffn_tile(j)
                ypk[tile_rows(j), :] = _pack_bf16_pairs(y.astype(BF16).astype(F32))
                copy_out(grp, j).start()
            for_tiles(g, finish)

    @pl.when(jnp.logical_and(g == pl.num_programs(0) - 1, f == n_f - 1))
    def _():
        for_tiles(n_groups - 1, lambda grp, j: copy_out(grp, j).wait())


def _moe(ge, grow, gnt, ng, xs, wg, wu, wd, bg, bu, bd, n_groups_max):
    n_f = D_FF // MOE_TF

    def fidx(g, f, ng_ref):
        return jnp.where(g < ng_ref[0], f, n_f - 1)

    wspec = lambda shape, imap: pl.BlockSpec(shape, imap)
    grid_spec = pltpu.PrefetchScalarGridSpec(
        num_scalar_prefetch=4,
        grid=(n_groups_max, n_f),
        in_specs=[
            pl.BlockSpec(memory_space=pl.ANY),
            wspec((None, D_MODEL, MOE_TF), lambda g, f, ge, gr, gn, ng: (ge[g], 0, fidx(g, f, ng))),
            wspec((None, D_MODEL, MOE_TF), lambda g, f, ge, gr, gn, ng: (ge[g], 0, fidx(g, f, ng))),
            wspec((None, MOE_TF, D_MODEL), lambda g, f, ge, gr, gn, ng: (ge[g], fidx(g, f, ng), 0)),
            wspec((None, 1, MOE_TF), lambda g, f, ge, gr, gn, ng: (ge[g], 0, fidx(g, f, ng))),
            wspec((None, 1, MOE_TF), lambda g, f, ge, gr, gn, ng: (ge[g], 0, fidx(g, f, ng))),
            wspec((None, 1, D_MODEL), lambda g, f, ge, gr, gn, ng: (ge[g], 0, 0)),
        ],
        out_specs=pl.BlockSpec(memory_space=pl.ANY),
        scratch_shapes=[
            pltpu.VMEM((MOE_GROUP_ROWS, HALF), U32),
            pltpu.VMEM((MOE_GROUP_ROWS, D_MODEL), BF16),
            pltpu.VMEM((MOE_GROUP_ROWS, D_MODEL), F32),
            pltpu.VMEM((MOE_GROUP_ROWS, HALF), U32),
            pltpu.VMEM((D_MODEL, MOE_TF), BF16),
            pltpu.VMEM((D_MODEL, MOE_TF), BF16),
            pltpu.VMEM((MOE_TF, D_MODEL), BF16),
            pltpu.SemaphoreType.DMA((1,)),
            pltpu.SemaphoreType.DMA((1,)),
        ],
    )
    return pl.pallas_call(
        functools.partial(_moe_kernel, n_f=n_f),
        grid_spec=grid_spec,
        out_shape=jax.ShapeDtypeStruct(xs.shape, U32),
        compiler_params=_params(56, ("arbitrary", "arbitrary")),
        name="moe_ffn",
    )(ge, grow, gnt, ng, xs, wg, wu, wd, bg, bu, bd)


def _combine_kernel(seg_loc_ref, seg_src_ref, seg_len_ref,
                    ys_hbm, h1_ref, dest_ref, w_ref, fn_ref, y_ref, sb, sem, *, tile0, kt):
    i = pl.program_id(0)
    n_steps = pl.num_programs(0)
    slot = i % 2
    tile = tile0 + i
    tt = h1_ref.shape[0]

    def seg_copy(sl, s, j):
        return pltpu.make_async_copy(_seg_rows(ys_hbm, seg_src_ref[s], j),
                                     _seg_rows(sb.at[sl], seg_loc_ref[s], j), sem.at[sl])

    @pl.when(i == 0)
    def _():
        sb[...] = jnp.zeros(sb.shape, U32)
        _for_segments(tile, seg_len_ref, lambda s, j: seg_copy(slot, s, j).start())

    @pl.when(i + 1 < n_steps)
    def _():
        _for_segments(tile + 1, seg_len_ref, lambda s, j: seg_copy(1 - slot, s, j).start())

    _for_segments(tile, seg_len_ref, lambda s, j: seg_copy(slot, s, j).wait())

    h1 = h1_ref[...]
    acc_lo = h1[:, :HALF]
    acc_hi = h1[:, HALF:]
    d = [jnp.broadcast_to(dest_ref[:, k:k + 1], (tt, PERM_CHUNK)) for k in range(TOP_K)]
    w = [jnp.broadcast_to(w_ref[:, k:k + 1], (tt, PERM_CHUNK)) for k in range(TOP_K)]
    for c0 in range(0, kt, PERM_CHUNK):
        col = c0 + lax.broadcasted_iota(I32, (tt, PERM_CHUNK), 1)
        wm = jnp.zeros((tt, PERM_CHUNK), F32)
        for k in range(TOP_K):
            wm = jnp.where(d[k] == col, w[k], wm)
        wm = wm.astype(BF16)
        lo, hi = _unpack_bf16_pairs(sb[slot, c0:c0 + PERM_CHUNK, :])
        acc_lo = acc_lo + jnp.dot(wm, lo, preferred_element_type=F32)
        acc_hi = acc_hi + jnp.dot(wm, hi, preferred_element_type=F32)
    ms = (jnp.sum(acc_lo * acc_lo, axis=-1, keepdims=True)
          + jnp.sum(acc_hi * acc_hi, axis=-1, keepdims=True)) * (1.0 / D_MODEL)
    scale = lax.rsqrt(ms + RMS_EPS)
    fn = fn_ref[...]
    y_ref[:, :HALF] = acc_lo * scale * fn[:, :HALF]
    y_ref[:, HALF:] = acc_hi * scale * fn[:, HALF:]


def _combine(tables, ys, h1, dest_t, wgt_t, fn, tile0, tt, kt):
    n = h1.shape[0]
    grid_spec = pltpu.PrefetchScalarGridSpec(
        num_scalar_prefetch=3,
        grid=(n // tt,),
        in_specs=[pl.BlockSpec(memory_space=pl.ANY),
                  pl.BlockSpec((tt, D_MODEL), lambda i, *_: (i, 0)),
                  pl.BlockSpec((tt, TOP_K), lambda i, *_: (i, 0)),
                  pl.BlockSpec((tt, TOP_K), lambda i, *_: (i, 0)),
                  pl.BlockSpec((1, D_MODEL), lambda i, *_: (0, 0))],
        out_specs=pl.BlockSpec((tt, D_MODEL), lambda i, *_: (i, 0)),
        scratch_shapes=[pltpu.VMEM((2, kt, HALF), U32), pltpu.SemaphoreType.DMA((2,))],
    )
    return pl.pallas_call(
        functools.partial(_combine_kernel, tile0=tile0, kt=kt),
        grid_spec=grid_spec,
        out_shape=jax.ShapeDtypeStruct((n, D_MODEL), F32),
        compiler_params=_params(56, ("arbitrary",)),
        name="combine",
    )(*tables, ys, h1, dest_t, wgt_t, fn)


def _ceil_to(x, m):
    return (x + m - 1) // m * m


def _routing_tables(eid, tile_lens, n_groups_max):
    n = eid.shape[1]
    nt = len(tile_lens)
    starts = np.concatenate([[0], np.cumsum(tile_lens)[:-1]]).astype(np.int64)
    onehot = (eid[:, :, None] == jnp.arange(N_EXPERTS, dtype=I32)[None, None, :]).astype(I32)
    member = jnp.sum(onehot, axis=0)
    incl = jnp.cumsum(member, axis=0)
    before = incl - member
    rank0 = jnp.stack([before[int(s)] for s in starts])
    cnt = jnp.stack([incl[int(s) + int(l) - 1] for s, l in zip(starts, tile_lens)]) - rank0
    c8 = _ceil_to(cnt, SUBLANES)
    seg_loc = jnp.cumsum(c8, axis=1) - c8
    rows_e = jnp.sum(c8, axis=0)
    region_e = _ceil_to(rows_e, MOE_TM)
    start_e = jnp.cumsum(region_e) - region_e
    seg_pos = start_e[None, :] + jnp.cumsum(c8, axis=0) - c8

    def per_token(tab):
        return jnp.concatenate([jnp.broadcast_to(tab[t], (int(l), N_EXPERTS)) for t, l in enumerate(tile_lens)])
    base = per_token(seg_loc - rank0) + before
    dest = jnp.sum(onehot * base[None], axis=2).astype(I32)

    ntiles_e = region_e // MOE_TM
    groups_e = (ntiles_e + MOE_TILES_PER_GROUP - 1) // MOE_TILES_PER_GROUP
    gend_e = jnp.cumsum(groups_e)
    gstart_e = gend_e - groups_e
    n_groups = gend_e[-1]
    j = jnp.arange(n_groups_max, dtype=I32)
    j_act = jnp.minimum(j, n_groups - 1)
    e_j = jnp.minimum(jnp.sum((gend_e[None, :] <= j_act[:, None]).astype(I32), axis=1), N_EXPERTS - 1)
    sel = (e_j[:, None] == jnp.arange(N_EXPERTS, dtype=I32)[None, :]).astype(I32)
    pick = lambda v: jnp.sum(sel * v[None, :], axis=1)
    local = j_act - pick(gstart_e)
    grow = pick(start_e) + local * MOE_GROUP_ROWS
    gnt = jnp.where(j < n_groups, jnp.clip(pick(ntiles_e) - local * MOE_TILES_PER_GROUP, 0, MOE_TILES_PER_GROUP), 0)

    flat = lambda a: a.reshape(-1).astype(I32)
    seg_tabs = (flat(seg_loc), flat(seg_pos), flat(c8))
    tail_tabs = ((start_e + rows_e).astype(I32), (region_e - rows_e).astype(I32))
    group_tabs = (e_j.astype(I32), grow.astype(I32), gnt.astype(I32), n_groups.reshape(1).astype(I32))
    return dest, seg_tabs, tail_tabs, group_tabs


def _block_diag(w, per_block):
    h, d, _ = w.shape
    nb = h // per_block
    eye = jnp.eye(per_block, dtype=w.dtype)
    w4 = w.reshape(nb, per_block, d, d)
    out = jnp.einsum("bpij,pq->bpiqj", w4, eye)
    return out.reshape(nb, per_block * d, per_block * d)


def kernel(x_prompt, x_sample, state_conv, state_rglru, state_pool, meta_tokens, norm1, w_in, conv_w, conv_b, lru_wa, lru_ba, lru_wx, lru_bx, lru_lambda, pool_w, pool_scale, w_out, norm2, router_w, router_b, exp_wg, exp_bg, exp_wu, exp_bu, exp_wd, exp_bd, final_norm):
    batch, seq, _ = x_prompt.shape
    dec = x_sample.shape[0]
    n_prompt = batch * seq
    n_tok = n_prompt + dec
    l = 0
    row = lambda v: v.reshape(1, -1)

    xp = x_prompt.reshape(n_prompt, D_MODEL)
    xs_tok = x_sample.reshape(dec, D_MODEL)
    w_in_b = w_in[l].astype(BF16)
    w_out_b = w_out[l].astype(BF16)
    heads_per_block = V7X_MXU_DIM // LRU_HEAD_DIM
    mixw = (conv_w[l], row(conv_b[l]),
            _block_diag(lru_wa[l], heads_per_block).astype(BF16), row(lru_ba[l]),
            _block_diag(lru_wx[l], heads_per_block).astype(BF16), row(lru_bx[l]),
            row(lru_lambda[l]), pool_w[l].astype(BF16), row(pool_scale[l]))
    g1 = row(norm1[l])

    proj_m = _proj(meta_tokens, g1, w_in_b, N_META)
    zeros = lambda r: jnp.zeros((1, r, LRU_WIDTH), F32)
    _, h_meta = _mix_seq(proj_m, 1, N_META, N_META, 0, zeros(SUBLANES), zeros(SUBLANES), zeros(2 * SUBLANES), mixw)
    conv0 = jnp.broadcast_to(proj_m[None, N_META - SUBLANES:, :LRU_WIDTH], (batch, SUBLANES, LRU_WIDTH))
    pool0 = jnp.broadcast_to(proj_m[None, :, 2 * LRU_WIDTH:], (batch, N_META, POOL_WIDTH))
    h0 = jnp.broadcast_to(h_meta, (batch, SUBLANES, LRU_WIDTH))

    proj_p = _proj(xp, g1, w_in_b, PROMPT_TILE)
    proj_s = _proj(xs_tok, g1, w_in_b, dec)
    yab_p, h_p = _mix_seq(proj_p, batch, seq, TIME_TILE, N_META, conv0, h0, pool0, mixw)
    yab_s, h_s = _mix_step(proj_s, jnp.swapaxes(state_conv[l], 0, 1), state_rglru[l],
                           jnp.swapaxes(state_pool[l], 0, 1), mixw)

    out_w = (w_out_b, row(norm2[l]), router_w[l].T, router_b[l].reshape(N_EXPERTS, 1))
    h1_p, xn_p, eid_p, wgt_p = _out(yab_p, xp, *out_w, PROMPT_TILE)
    h1_s, xn_s, eid_s, wgt_s = _out(yab_s, xs_tok, *out_w, dec)

    n_ptiles = n_prompt // PROMPT_TILE
    tile_lens = [PROMPT_TILE] * n_ptiles + [dec]
    seg_pad = (SUBLANES - 1) * N_EXPERTS
    kt_p = _ceil_to(TOP_K * PROMPT_TILE + seg_pad, PERM_CHUNK)
    kt_s = _ceil_to(TOP_K * dec + seg_pad, PERM_CHUNK)
    tiles_max = (TOP_K * n_tok + len(tile_lens) * seg_pad) // MOE_TM + N_EXPERTS
    xs_rows = tiles_max * MOE_TM
    n_groups_max = (tiles_max + N_EXPERTS * (MOE_TILES_PER_GROUP - 1)) // MOE_TILES_PER_GROUP

    eid = jnp.concatenate([eid_p, eid_s], axis=1)
    dest, seg_tabs, tail_tabs, group_tabs = _routing_tables(eid, tile_lens, n_groups_max)
    disp_tabs = seg_tabs + tail_tabs
    dest_p, dest_s = dest[:, :n_prompt], dest[:, n_prompt:]

    xs = _dispatch(disp_tabs, xn_p, dest_p, None, 0, PROMPT_TILE, kt_p, xs_rows, False)
    xs = _dispatch(disp_tabs, xn_s, dest_s, xs, n_ptiles, dec, kt_s, xs_rows, True)
    ys = _moe(*group_tabs, xs, exp_wg[l], exp_wu[l], exp_wd[l],
              exp_bg[l].reshape(N_EXPERTS, 1, D_FF), exp_bu[l].reshape(N_EXPERTS, 1, D_FF),
              exp_bd[l].reshape(N_EXPERTS, 1, D_MODEL), n_groups_max)
    fn = row(final_norm)
    y_p = _combine(seg_tabs, ys, h1_p, dest_p.T, wgt_p.T, fn, 0, PROMPT_TILE, kt_p)
    y_s = _combine(seg_tabs, ys, h1_s, dest_s.T, wgt_s.T, fn, n_ptiles, dec, kt_s)

    proj_p3 = proj_p.reshape(batch, seq, IN_WIDTH)
    conv_p = proj_p3[:, seq - (CONV_WIDTH - 1):, :LRU_WIDTH]
    pool_p = proj_p3[:, seq - POOL_BUF:, 2 * LRU_WIDTH:]
    conv_s = jnp.concatenate([state_conv[l][:, 1:], proj_s[:, None, :LRU_WIDTH]], axis=1)
    pool_s = jnp.concatenate([state_pool[l][:, 1:], proj_s[:, None, 2 * LRU_WIDTH:]], axis=1)
    return (y_p.reshape(batch, seq, D_MODEL), y_s.reshape(dec, 1, D_MODEL),
            conv_p[None], h_p[None, :, SUBLANES - 1], pool_p[None],
            conv_s[None], h_s[None], pool_s[None])
```

```python
import functools

import numpy as np
import jax
import jax.numpy as jnp
from jax import lax
from jax.experimental import pallas as pl
from jax.experimental.pallas import tpu as pltpu

F32 = jnp.float32
BF16 = jnp.bfloat16
I32 = jnp.int32
U32 = jnp.uint32

D_MODEL = 2048
N_META = 16
LRU_WIDTH = 1024
LRU_HEADS = 16
LRU_HEAD_DIM = LRU_WIDTH // LRU_HEADS
CONV_WIDTH = 4
LRU_C = 8.0
POOL_WIDTH = D_MODEL - LRU_WIDTH
POOL_WINDOWS = (2, 4, 8, 16)
POOL_GROUP_DIM = POOL_WIDTH // len(POOL_WINDOWS)
POOL_BUF = max(POOL_WINDOWS) - 1
IN_WIDTH = 2 * LRU_WIDTH + POOL_WIDTH
N_EXPERTS = 32
TOP_K = 4
D_FF = D_MODEL
SWIGLU_ALPHA = 1.702
SWIGLU_LIMIT = 7.0
RMS_EPS = 1e-6

V7X_MXU_DIM = 256
SUBLANES = 8
BF16_ROWS = 16

PROMPT_TILE = 512
TIME_TILE = 256
MOE_UNIT = 128
MOE_TM_UNITS = 4
MOE_TM = MOE_TM_UNITS * MOE_UNIT
MOE_GROUP_UNITS = 11
MOE_GROUP_ROWS = MOE_UNIT * MOE_GROUP_UNITS
MOE_TF = V7X_MXU_DIM
PERM_CHUNK = V7X_MXU_DIM


def _params(vmem_mb, sem):
    return pltpu.CompilerParams(dimension_semantics=sem, vmem_limit_bytes=vmem_mb << 20)


def _rmsnorm(x, g):
    return x * lax.rsqrt(jnp.mean(x * x, axis=-1, keepdims=True) + RMS_EPS) * g


def _resident(shape):
    return pl.BlockSpec(shape, lambda *_: (0,) * len(shape), pipeline_mode=pl.Buffered(1))


def _proj_kernel(x_ref, g_ref, w_ref, o_ref):
    xn = _rmsnorm(x_ref[...], g_ref[...])
    o_ref[...] = jnp.dot(xn.astype(BF16), w_ref[...], preferred_element_type=F32)


def _proj(x, g, w_bf16, tile):
    rows = x.shape[0]
    return pl.pallas_call(
        _proj_kernel,
        grid=(rows // tile,),
        in_specs=[pl.BlockSpec((tile, D_MODEL), lambda i: (i, 0)),
                  _resident((1, D_MODEL)), _resident((D_MODEL, IN_WIDTH))],
        out_specs=pl.BlockSpec((tile, IN_WIDTH), lambda i: (i, 0)),
        out_shape=jax.ShapeDtypeStruct((rows, IN_WIDTH), F32),
        compiler_params=_params(48, ("arbitrary",)),
        name="proj",
    )(x, g, w_bf16)


def _lru_coeffs(xc, wa_ref, ba_ref, wx_ref, bx_ref, lam_ref):
    xcb = xc.astype(BF16)
    nq = LRU_WIDTH // V7X_MXU_DIM
    rs, gs = [], []
    for q in range(nq):
        blk = xcb[:, q * V7X_MXU_DIM:(q + 1) * V7X_MXU_DIM]
        rs.append(jnp.dot(blk, wa_ref[q], preferred_element_type=F32))
        gs.append(jnp.dot(blk, wx_ref[q], preferred_element_type=F32))
    r = jax.nn.sigmoid(jnp.concatenate(rs, axis=-1) + ba_ref[...])
    i = jax.nn.sigmoid(jnp.concatenate(gs, axis=-1) + bx_ref[...])
    log_a = -LRU_C * r * jax.nn.softplus(-lam_ref[...])
    a = jnp.exp(log_a)
    th = jnp.tanh(log_a)
    beta = jnp.sqrt(-2.0 * th / (1.0 - th))
    return a, beta * i * xc


def _pool_project(wins, u, inv_cnt, pw_ref, ps_ref):
    outs = []
    for g in range(len(POOL_WINDOWS)):
        sl = slice(g * POOL_GROUP_DIM, (g + 1) * POOL_GROUP_DIM)
        d = (wins[g] * inv_cnt[g] - u[:, sl]).astype(BF16)
        outs.append(jnp.dot(d, pw_ref[g], preferred_element_type=F32))
    return jnp.concatenate(outs, axis=-1) * ps_ref[...]


def _mix_seq_kernel(xa_ref, ga_ref, ub_ref, conv0_ref, h0_ref, pool0_ref,
                    cw_ref, cb_ref, wa_ref, ba_ref, wx_ref, bx_ref, lam_ref, pw_ref, ps_ref,
                    yab_ref, ht_ref, cext, pext, hcar, *, tt, start):
    t = pl.program_id(1)
    hist = 2 * SUBLANES

    @pl.when(t == 0)
    def _():
        cext[0:SUBLANES, :] = conv0_ref[...]
        pext[0:hist, :] = pool0_ref[...]
        hcar[...] = h0_ref[...]

    xa = xa_ref[...]
    cext[SUBLANES:SUBLANES + tt, :] = xa
    cw = cw_ref[...]
    xc = cb_ref[...] + cw[CONV_WIDTH - 1:CONV_WIDTH] * xa
    for k in range(1, CONV_WIDTH):
        xc = xc + cw[CONV_WIDTH - 1 - k:CONV_WIDTH - k] * cext[SUBLANES - k:SUBLANES - k + tt, :]
    cext[0:SUBLANES, :] = cext[tt:tt + SUBLANES, :]

    a, b = _lru_coeffs(xc, wa_ref, ba_ref, wx_ref, bx_ref, lam_ref)

    row = lax.broadcasted_iota(I32, (tt, LRU_WIDTH), 0)
    s = 1
    while s < tt:
        keep = row >= s
        a_prev = jnp.where(keep, pltpu.roll(a, s, 0), 1.0)
        b_prev = jnp.where(keep, pltpu.roll(b, s, 0), 0.0)
        b = b + a * b_prev
        a = a * a_prev
        s *= 2
    hs = a * hcar[SUBLANES - 1:SUBLANES, :] + b
    hcar[...] = hs[tt - SUBLANES:tt, :]
    ht_ref[...] = hs[tt - SUBLANES:tt, :]
    ya = hs * jax.nn.gelu(ga_ref[...])

    u = ub_ref[...]
    pext[hist:hist + tt, :] = u
    e = pext[...]
    wins = []
    shift = 1
    for g in range(len(POOL_WINDOWS)):
        e = e + pltpu.roll(e, shift, 0)
        wins.append(e[hist:hist + tt, :POOL_GROUP_DIM])
        if g + 1 < len(POOL_WINDOWS):
            e = e[:, POOL_GROUP_DIM:]
        shift *= 2
    pext[0:hist, :] = pext[tt:tt + hist, :]

    if start >= POOL_BUF:
        inv_cnt = [1.0 / w for w in POOL_WINDOWS]
    else:
        pos = start + t * tt + lax.broadcasted_iota(I32, (tt, 1), 0)
        inv_cnt = [1.0 / jnp.minimum(w, pos + 1).astype(F32) for w in POOL_WINDOWS]
    yb = _pool_project(wins, u, inv_cnt, pw_ref, ps_ref)

    yab_ref[:, :LRU_WIDTH] = ya.astype(BF16)
    yab_ref[:, LRU_WIDTH:] = yb.astype(BF16)


def _mix_seq(proj, batch, seq, tt, start, conv0, h0, pool0, mixw):
    nt = seq // tt
    hist = 2 * SUBLANES
    col = lambda c: pl.BlockSpec((tt, LRU_WIDTH), lambda b, t: (b * nt + t, c))
    state = lambda r: pl.BlockSpec((None, r, LRU_WIDTH), lambda b, t: (b, 0, 0))
    return pl.pallas_call(
        functools.partial(_mix_seq_kernel, tt=tt, start=start),
        grid=(batch, nt),
        in_specs=[col(0), col(1), col(2), state(SUBLANES), state(SUBLANES), state(hist)]
        + [_resident(w.shape) for w in mixw],
        out_specs=[pl.BlockSpec((tt, D_MODEL), lambda b, t: (b * nt + t, 0)), state(SUBLANES)],
        out_shape=[jax.ShapeDtypeStruct((batch * seq, D_MODEL), BF16),
                   jax.ShapeDtypeStruct((batch, SUBLANES, LRU_WIDTH), F32)],
        scratch_shapes=[pltpu.VMEM((tt + SUBLANES, LRU_WIDTH), F32),
                        pltpu.VMEM((tt + hist, POOL_WIDTH), F32),
                        pltpu.VMEM((SUBLANES, LRU_WIDTH), F32)],
        compiler_params=_params(48, ("arbitrary", "arbitrary")),
        name="mix_seq",
    )(proj, proj, proj, conv0, h0, pool0, *mixw)


def _mix_step_kernel(xa_ref, ga_ref, ub_ref, sconv_ref, sh_ref, spool_ref,
                     cw_ref, cb_ref, wa_ref, ba_ref, wx_ref, bx_ref, lam_ref, pw_ref, ps_ref,
                     yab_ref, h_ref):
    xa = xa_ref[...]
    cw = cw_ref[...]
    xc = cb_ref[...] + cw[CONV_WIDTH - 1:CONV_WIDTH] * xa
    for k in range(1, CONV_WIDTH):
        xc = xc + cw[CONV_WIDTH - 1 - k:CONV_WIDTH - k] * sconv_ref[CONV_WIDTH - 1 - k]
    a, b = _lru_coeffs(xc, wa_ref, ba_ref, wx_ref, bx_ref, lam_ref)
    h = a * sh_ref[...] + b
    h_ref[...] = h
    ya = h * jax.nn.gelu(ga_ref[...])

    u = ub_ref[...]
    wins = []
    for g, w in enumerate(POOL_WINDOWS):
        sl = slice(g * POOL_GROUP_DIM, (g + 1) * POOL_GROUP_DIM)
        acc = u[:, sl]
        for k in range(1, w):
            acc = acc + spool_ref[POOL_BUF - k, :, sl]
        wins.append(acc)
    yb = _pool_project(wins, u, [1.0 / w for w in POOL_WINDOWS], pw_ref, ps_ref)
    yab_ref[:, :LRU_WIDTH] = ya.astype(BF16)
    yab_ref[:, LRU_WIDTH:] = yb.astype(BF16)


def _mix_step(proj, sconv_t, sh, spool_t, mixw):
    rows = proj.shape[0]
    col = lambda c: pl.BlockSpec((rows, LRU_WIDTH), lambda i: (0, c))
    full = lambda a: pl.BlockSpec(a.shape, lambda i: (0,) * a.ndim)
    return pl.pallas_call(
        _mix_step_kernel,
        grid=(1,),
        in_specs=[col(0), col(1), col(2), full(sconv_t), full(sh), full(spool_t)] + [full(w) for w in mixw],
        out_specs=[pl.BlockSpec((rows, D_MODEL), lambda i: (0, 0)),
                   pl.BlockSpec((rows, LRU_WIDTH), lambda i: (0, 0))],
        out_shape=[jax.ShapeDtypeStruct((rows, D_MODEL), BF16),
                   jax.ShapeDtypeStruct((rows, LRU_WIDTH), F32)],
        compiler_params=_params(48, ("arbitrary",)),
        name="mix_step",
    )(proj, proj, proj, sconv_t, sh, spool_t, *mixw)


def _out_kernel(yab_ref, x_ref, wo_ref, g2_ref, rwt_ref, rb_ref, h1_ref, xn_ref, eid_ref, wgt_ref, *, tile):
    h1 = x_ref[...] + jnp.dot(yab_ref[...], wo_ref[...], preferred_element_type=F32)
    h1_ref[...] = h1
    xn = _rmsnorm(h1, g2_ref[...])
    xn_ref[...] = xn.astype(BF16)

    logits = lax.dot_general(rwt_ref[...], xn, (((1,), (1,)), ((), ())),
                             precision=lax.Precision.HIGHEST, preferred_element_type=F32) + rb_ref[...]
    eidx = lax.broadcasted_iota(I32, (N_EXPERTS, tile), 0)
    vals, ids = [], []
    for _ in range(TOP_K):
        m = jnp.max(logits, axis=0, keepdims=True)
        idx = jnp.min(jnp.where(logits == m, eidx, N_EXPERTS), axis=0, keepdims=True)
        vals.append(m)
        ids.append(idx)
        logits = jnp.where(eidx == idx, -jnp.inf, logits)
    ex = [jnp.exp(v - vals[0]) for v in vals]
    tot = ex[0]
    for e_ in ex[1:]:
        tot = tot + e_
    eid_ref[...] = jnp.concatenate(ids, axis=0)
    wgt_ref[...] = jnp.concatenate([e_ / tot for e_ in ex], axis=0)


def _out(yab, x, wo_bf16, g2, rwt, rb, tile):
    n = x.shape[0]
    rowblk = lambda w: pl.BlockSpec((tile, w), lambda i: (i, 0))
    colblk = pl.BlockSpec((TOP_K, tile), lambda i: (0, i))
    return pl.pallas_call(
        functools.partial(_out_kernel, tile=tile),
        grid=(n // tile,),
        in_specs=[rowblk(D_MODEL), rowblk(D_MODEL), _resident((D_MODEL, D_MODEL)), _resident((1, D_MODEL)),
                  _resident((N_EXPERTS, D_MODEL)), _resident((N_EXPERTS, 1))],
        out_specs=[rowblk(D_MODEL), rowblk(D_MODEL), colblk, colblk],
        out_shape=[jax.ShapeDtypeStruct((n, D_MODEL), F32), jax.ShapeDtypeStruct((n, D_MODEL), BF16),
                   jax.ShapeDtypeStruct((TOP_K, n), I32), jax.ShapeDtypeStruct((TOP_K, n), F32)],
        compiler_params=_params(48, ("arbitrary",)),
        name="out_router",
    )(yab, x, wo_bf16, g2, rwt, rb)


def _for_segments(tile, seg_len_ref, fn):
    def per_expert(e, c):
        s = tile * N_EXPERTS + e

        def per_chunk(j, c2):
            fn(s, j)
            return c2
        lax.fori_loop(0, seg_len_ref[s] // BF16_ROWS, per_chunk, 0)
        return c
    lax.fori_loop(0, N_EXPERTS, per_expert, 0)


def _seg_rows(ref, start, j):
    return ref.at[pl.ds(pl.multiple_of(start + j * BF16_ROWS, BF16_ROWS), BF16_ROWS)]


def _dispatch_kernel(seg_loc_ref, seg_dst_ref, seg_len_ref, tail_dst_ref, tail_len_ref,
                     xn_ref, dest_ref, *rest, tile0, kt, zero_tails):
    xs_hbm, cb, zb, sem, zsem = rest[-5:]
    i = pl.program_id(0)
    n_steps = pl.num_programs(0)
    slot = i % 2
    tile = tile0 + i
    tt = xn_ref.shape[0]

    def seg_copy(sl, s, j):
        return pltpu.make_async_copy(_seg_rows(cb.at[sl], seg_loc_ref[s], j),
                                     _seg_rows(xs_hbm, seg_dst_ref[s], j), sem.at[sl])

    @pl.when(i >= 2)
    def _():
        _for_segments(tile - 2, seg_len_ref, lambda s, j: seg_copy(slot, s, j).wait())

    x = xn_ref[...]
    d = [dest_ref[k:k + 1, :] for k in range(TOP_K)]
    for c0 in range(0, kt, PERM_CHUNK):
        r = c0 + lax.broadcasted_iota(I32, (PERM_CHUNK, tt), 0)
        p = jnp.zeros((PERM_CHUNK, tt), F32)
        for k in range(TOP_K):
            p = jnp.where(d[k] == r, 1.0, p)
        rows = jnp.dot(p.astype(BF16), x, preferred_element_type=F32)
        cb[slot, c0:c0 + PERM_CHUNK, :] = rows.astype(BF16)

    _for_segments(tile, seg_len_ref, lambda s, j: seg_copy(slot, s, j).start())

    @pl.when(i == n_steps - 1)
    def _():
        @pl.when(i >= 1)
        def _():
            _for_segments(tile - 1, seg_len_ref, lambda s, j: seg_copy(1 - slot, s, j).wait())
        _for_segments(tile, seg_len_ref, lambda s, j: seg_copy(slot, s, j).wait())

    if zero_tails:
        zb[...] = jnp.zeros(zb.shape, BF16)

        def tail_copy(e, j):
            return pltpu.make_async_copy(zb, _seg_rows(xs_hbm, tail_dst_ref[e], j), zsem.at[0])

        def tails(method):
            def per_expert(e, c):
                def per_chunk(j, c2):
                    getattr(tail_copy(e, j), method)()
                    return c2
                lax.fori_loop(0, tail_len_ref[e] // BF16_ROWS, per_chunk, 0)
                return c
            lax.fori_loop(0, N_EXPERTS, per_expert, 0)

        @pl.when(i == n_steps - 1)
        def _():
            tails("start")
            tails("wait")


def _dispatch(tables, xn, dest, xs, tile0, tt, kt, xs_rows, zero_tails):
    n = xn.shape[0]
    in_specs = [pl.BlockSpec((tt, D_MODEL), lambda i, *_: (i, 0)),
                pl.BlockSpec((TOP_K, tt), lambda i, *_: (0, i))]
    args = [*tables, xn, dest]
    aliases = {}
    if xs is not None:
        in_specs.append(pl.BlockSpec(memory_space=pl.ANY))
        aliases = {len(args): 0}
        args.append(xs)
    grid_spec = pltpu.PrefetchScalarGridSpec(
        num_scalar_prefetch=len(tables),
        grid=(n // tt,),
        in_specs=in_specs,
        out_specs=pl.BlockSpec(memory_space=pl.ANY),
        scratch_shapes=[pltpu.VMEM((2, kt, D_MODEL), BF16), pltpu.VMEM((BF16_ROWS, D_MODEL), BF16),
                        pltpu.SemaphoreType.DMA((2,)), pltpu.SemaphoreType.DMA((1,))],
    )
    return pl.pallas_call(
        functools.partial(_dispatch_kernel, tile0=tile0, kt=kt, zero_tails=zero_tails),
        grid_spec=grid_spec,
        out_shape=jax.ShapeDtypeStruct((xs_rows, D_MODEL), BF16),
        input_output_aliases=aliases,
        compiler_params=_params(48, ("arbitrary",)),
        name="dispatch",
    )(*args)


def _moe_kernel(ge_ref, grow_ref, gnu_ref, ng_ref,
                xs_hbm, wg_hbm, wu_hbm, wd_hbm, bg_ref, bu_ref, bd_ref,
                ys_hbm, xb, acc, ypk, wgf, wuf, wdf, xsem, wsem, osem, *, n_f):
    g = pl.program_id(0)
    n_groups = ng_ref[0]
    xslot = g % 2

    def unit_rows(j):
        return pl.ds(pl.multiple_of(j * MOE_UNIT, MOE_UNIT), MOE_UNIT)

    def hbm_unit(ref, grp, j):
        return ref.at[pl.ds(pl.multiple_of(grow_ref[grp] + j * MOE_UNIT, MOE_UNIT), MOE_UNIT)]

    def copy_in(grp, sl, j):
        return pltpu.make_async_copy(hbm_unit(xs_hbm, grp, j), xb.at[sl, unit_rows(j)], xsem.at[sl])

    def copy_out(grp, j):
        return pltpu.make_async_copy(ypk.at[unit_rows(j)], hbm_unit(ys_hbm, grp, j), osem.at[0])

    def for_units(grp, fn):
        def body(j, c):
            fn(grp, j)
            return c
        lax.fori_loop(0, gnu_ref[grp], body, 0)

    def weight_copies(grp, f, ws):
        e = ge_ref[grp]
        cols = pl.ds(pl.multiple_of(f * MOE_TF, MOE_TF), MOE_TF)
        return (pltpu.make_async_copy(wg_hbm.at[e, :, cols], wgf.at[ws], wsem.at[ws, 0]),
                pltpu.make_async_copy(wu_hbm.at[e, :, cols], wuf.at[ws], wsem.at[ws, 1]),
                pltpu.make_async_copy(wd_hbm.at[e, cols, :], wdf.at[ws], wsem.at[ws, 2]))

    @pl.when(g < n_groups)
    def _():
        @pl.when(g == 0)
        def _():
            for_units(0, lambda grp, j: copy_in(grp, 0, j).start())
            for cp in weight_copies(0, 0, 0):
                cp.start()
        for_units(g, lambda grp, j: copy_in(grp, xslot, j).wait())

        @pl.when(g + 1 < n_groups)
        def _():
            for_units(g + 1, lambda grp, j: copy_in(grp, 1 - xslot, j).start())

        def init(grp, j):
            acc[unit_rows(j), :] = jnp.broadcast_to(bd_ref[...], (MOE_UNIT, D_MODEL))
        for_units(g, init)
        n_units = gnu_ref[g]

        def chunk(f, c):
            ws = f % 2
            for cp in weight_copies(g, f, ws):
                cp.wait()

            @pl.when(f + 1 < n_f)
            def _():
                for cp in weight_copies(g, f + 1, 1 - ws):
                    cp.start()

            @pl.when(jnp.logical_and(f + 1 == n_f, g + 1 < n_groups))
            def _():
                for cp in weight_copies(g + 1, 0, 1 - ws):
                    cp.start()

            bg = bg_ref[pl.ds(f, 1), :]
            bu = bu_ref[pl.ds(f, 1), :]

            def ffn(start, size):
                x = xb[xslot, pl.ds(start, size), :]
                gg = jnp.dot(x, wgf[ws].astype(BF16), preferred_element_type=F32) + bg
                uu = jnp.dot(x, wuf[ws].astype(BF16), preferred_element_type=F32) + bu
                gg = jnp.minimum(gg, SWIGLU_LIMIT)
                uu = jnp.clip(uu, -SWIGLU_LIMIT, SWIGLU_LIMIT)
                hdn = gg * jax.nn.sigmoid(SWIGLU_ALPHA * gg) * (uu + 1.0)
                acc[pl.ds(start, size), :] += jnp.dot(hdn.astype(BF16), wdf[ws].astype(BF16),
                                                      preferred_element_type=F32)

            def full_tile(j, c2):
                ffn(pl.multiple_of(j * MOE_TM, MOE_TM), MOE_TM)
                return c2
            lax.fori_loop(0, n_units // MOE_TM_UNITS, full_tile, 0)
            part = MOE_TM_UNITS // 2
            while part >= 1:
                @pl.when((n_units // part) % 2 == 1)
                def _(part=part):
                    done = n_units // (2 * part) * (2 * part)
                    ffn(pl.multiple_of(done * MOE_UNIT, MOE_UNIT), part * MOE_UNIT)
                part //= 2
            return c
        lax.fori_loop(0, n_f, chunk, 0)

        @pl.when(g > 0)
        def _():
            for_units(g - 1, lambda grp, j: copy_out(grp, j).wait())

        def finish(grp, j):
            ypk[unit_rows(j), :] = acc[unit_rows(j), :].astype(BF16)
            copy_out(grp, j).start()
        for_units(g, finish)

    @pl.when(g == pl.num_programs(0) - 1)
    def _():
        for_units(n_groups - 1, lambda grp, j: copy_out(grp, j).wait())


def _moe(ge, grow, gnu, ng, xs, wg, wu, wd, bg, bu, bd, n_groups_max):
    n_f = D_FF // MOE_TF
    assert n_f % 2 == 0
    per_expert = lambda shape: pl.BlockSpec((None,) + shape, lambda g, ge, gr, gn, ng: (ge[g], 0, 0))
    any_spec = pl.BlockSpec(memory_space=pl.ANY)
    grid_spec = pltpu.PrefetchScalarGridSpec(
        num_scalar_prefetch=4,
        grid=(n_groups_max,),
        in_specs=[any_spec, any_spec, any_spec, any_spec,
                  per_expert((n_f, MOE_TF)), per_expert((n_f, MOE_TF)), per_expert((1, D_MODEL))],
        out_specs=any_spec,
        scratch_shapes=[
            pltpu.VMEM((2, MOE_GROUP_ROWS, D_MODEL), BF16),
            pltpu.VMEM((MOE_GROUP_ROWS, D_MODEL), F32),
            pltpu.VMEM((MOE_GROUP_ROWS, D_MODEL), BF16),
            pltpu.VMEM((2, D_MODEL, MOE_TF), F32),
            pltpu.VMEM((2, D_MODEL, MOE_TF), F32),
            pltpu.VMEM((2, MOE_TF, D_MODEL), F32),
            pltpu.SemaphoreType.DMA((2,)),
            pltpu.SemaphoreType.DMA((2, 3)),
            pltpu.SemaphoreType.DMA((1,)),
        ],
    )
    return pl.pallas_call(
        functools.partial(_moe_kernel, n_f=n_f),
        grid_spec=grid_spec,
        out_shape=jax.ShapeDtypeStruct(xs.shape, BF16),
        compiler_params=_params(56, ("arbitrary",)),
        name="moe_ffn",
    )(ge, grow, gnu, ng, xs, wg, wu, wd,
      bg.reshape(N_EXPERTS, n_f, MOE_TF), bu.reshape(N_EXPERTS, n_f, MOE_TF), bd.reshape(N_EXPERTS, 1, D_MODEL))


def _combine_kernel(seg_loc_ref, seg_src_ref, seg_len_ref,
                    ys_hbm, h1_ref, dest_ref, w_ref, fn_ref, y_ref, sb, sem, *, tile0, kt):
    i = pl.program_id(0)
    n_steps = pl.num_programs(0)
    slot = i % 2
    tile = tile0 + i
    tt = h1_ref.shape[0]

    def seg_copy(sl, s, j):
        return pltpu.make_async_copy(_seg_rows(ys_hbm, seg_src_ref[s], j),
                                     _seg_rows(sb.at[sl], seg_loc_ref[s], j), sem.at[sl])

    @pl.when(i == 0)
    def _():
        sb[...] = jnp.zeros(sb.shape, BF16)
        _for_segments(tile, seg_len_ref, lambda s, j: seg_copy(slot, s, j).start())

    @pl.when(i + 1 < n_steps)
    def _():
        _for_segments(tile + 1, seg_len_ref, lambda s, j: seg_copy(1 - slot, s, j).start())

    _for_segments(tile, seg_len_ref, lambda s, j: seg_copy(slot, s, j).wait())

    acc = h1_ref[...]
    d = [jnp.broadcast_to(dest_ref[:, k:k + 1], (tt, PERM_CHUNK)) for k in range(TOP_K)]
    w = [jnp.broadcast_to(w_ref[:, k:k + 1], (tt, PERM_CHUNK)) for k in range(TOP_K)]
    for c0 in range(0, kt, PERM_CHUNK):
        col = c0 + lax.broadcasted_iota(I32, (tt, PERM_CHUNK), 1)
        wm = jnp.zeros((tt, PERM_CHUNK), F32)
        for k in range(TOP_K):
            wm = jnp.where(d[k] == col, w[k], wm)
        acc = acc + jnp.dot(wm.astype(BF16), sb[slot, c0:c0 + PERM_CHUNK, :], preferred_element_type=F32)
    y_ref[...] = _rmsnorm(acc, fn_ref[...])


def _combine(tables, ys, h1, dest_t, wgt_t, fn, tile0, tt, kt):
    n = h1.shape[0]
    grid_spec = pltpu.PrefetchScalarGridSpec(
        num_scalar_prefetch=3,
        grid=(n // tt,),
        in_specs=[pl.BlockSpec(memory_space=pl.ANY),
                  pl.BlockSpec((tt, D_MODEL), lambda i, *_: (i, 0)),
                  pl.BlockSpec((tt, TOP_K), lambda i, *_: (i, 0)),
                  pl.BlockSpec((tt, TOP_K), lambda i, *_: (i, 0)),
                  pl.BlockSpec((1, D_MODEL), lambda i, *_: (0, 0))],
        out_specs=pl.BlockSpec((tt, D_MODEL), lambda i, *_: (i, 0)),
        scratch_shapes=[pltpu.VMEM((2, kt, D_MODEL), BF16), pltpu.SemaphoreType.DMA((2,))],
    )
    return pl.pallas_call(
        functools.partial(_combine_kernel, tile0=tile0, kt=kt),
        grid_spec=grid_spec,
        out_shape=jax.ShapeDtypeStruct((n, D_MODEL), F32),
        compiler_params=_params(56, ("arbitrary",)),
        name="combine",
    )(*tables, ys, h1, dest_t, wgt_t, fn)


def _ceil_to(x, m):
    return (x + m - 1) // m * m


def _routing_tables(eid, tile_lens, n_groups_max):
    n = eid.shape[1]
    nt = len(tile_lens)
    starts = np.concatenate([[0], np.cumsum(tile_lens)[:-1]]).astype(np.int64)
    onehot = (eid[:, :, None] == jnp.arange(N_EXPERTS, dtype=I32)[None, None, :]).astype(I32)
    member = jnp.sum(onehot, axis=0)
    incl = jnp.cumsum(member, axis=0)
    before = incl - member
    rank0 = jnp.stack([before[int(s)] for s in starts])
    cnt = jnp.stack([incl[int(s) + int(l) - 1] for s, l in zip(starts, tile_lens)]) - rank0
    c8 = _ceil_to(cnt, BF16_ROWS)
    seg_loc = jnp.cumsum(c8, axis=1) - c8
    rows_e = jnp.sum(c8, axis=0)
    region_e = _ceil_to(rows_e, MOE_UNIT)
    start_e = jnp.cumsum(region_e) - region_e
    seg_pos = start_e[None, :] + jnp.cumsum(c8, axis=0) - c8

    def per_token(tab):
        return jnp.concatenate([jnp.broadcast_to(tab[t], (int(l), N_EXPERTS)) for t, l in enumerate(tile_lens)])
    base = per_token(seg_loc - rank0) + before
    dest = jnp.sum(onehot * base[None], axis=2).astype(I32)

    ntiles_e = region_e // MOE_UNIT
    groups_e = (ntiles_e + MOE_GROUP_UNITS - 1) // MOE_GROUP_UNITS
    gend_e = jnp.cumsum(groups_e)
    gstart_e = gend_e - groups_e
    n_groups = gend_e[-1]
    j = jnp.arange(n_groups_max, dtype=I32)
    j_act = jnp.minimum(j, n_groups - 1)
    e_j = jnp.minimum(jnp.sum((gend_e[None, :] <= j_act[:, None]).astype(I32), axis=1), N_EXPERTS - 1)
    sel = (e_j[:, None] == jnp.arange(N_EXPERTS, dtype=I32)[None, :]).astype(I32)
    pick = lambda v: jnp.sum(sel * v[None, :], axis=1)
    local = j_act - pick(gstart_e)
    grow = pick(start_e) + local * MOE_GROUP_ROWS
    gnt = jnp.where(j < n_groups, jnp.clip(pick(ntiles_e) - local * MOE_GROUP_UNITS, 0, MOE_GROUP_UNITS), 0)

    flat = lambda a: a.reshape(-1).astype(I32)
    seg_tabs = (flat(seg_loc), flat(seg_pos), flat(c8))
    tail_tabs = ((start_e + rows_e).astype(I32), (region_e - rows_e).astype(I32))
    group_tabs = (e_j.astype(I32), grow.astype(I32), gnt.astype(I32), n_groups.reshape(1).astype(I32))
    return dest, seg_tabs, tail_tabs, group_tabs


def _block_diag(w, per_block):
    h, d, _ = w.shape
    nb = h // per_block
    eye = jnp.eye(per_block, dtype=w.dtype)
    w4 = w.reshape(nb, per_block, d, d)
    out = jnp.einsum("bpij,pq->bpiqj", w4, eye)
    return out.reshape(nb, per_block * d, per_block * d)


def kernel(x_prompt, x_sample, state_conv, state_rglru, state_pool, meta_tokens, norm1, w_in, conv_w, conv_b, lru_wa, lru_ba, lru_wx, lru_bx, lru_lambda, pool_w, pool_scale, w_out, norm2, router_w, router_b, exp_wg, exp_bg, exp_wu, exp_bu, exp_wd, exp_bd, final_norm):
    batch, seq, _ = x_prompt.shape
    dec = x_sample.shape[0]
    n_prompt = batch * seq
    n_tok = n_prompt + dec
    l = 0
    row = lambda v: v.reshape(1, -1)

    xp = x_prompt.reshape(n_prompt, D_MODEL)
    xs_tok = x_sample.reshape(dec, D_MODEL)
    w_in_b = w_in[l].astype(BF16)
    w_out_b = w_out[l].astype(BF16)
    heads_per_block = V7X_MXU_DIM // LRU_HEAD_DIM
    mixw = (conv_w[l], row(conv_b[l]),
            _block_diag(lru_wa[l], heads_per_block).astype(BF16), row(lru_ba[l]),
            _block_diag(lru_wx[l], heads_per_block).astype(BF16), row(lru_bx[l]),
            row(lru_lambda[l]), pool_w[l].astype(BF16), row(pool_scale[l]))
    g1 = row(norm1[l])

    proj_m = _proj(meta_tokens, g1, w_in_b, N_META)
    zeros = lambda r: jnp.zeros((1, r, LRU_WIDTH), F32)
    _, h_meta = _mix_seq(proj_m, 1, N_META, N_META, 0, zeros(SUBLANES), zeros(SUBLANES), zeros(2 * SUBLANES), mixw)
    conv0 = jnp.broadcast_to(proj_m[None, N_META - SUBLANES:, :LRU_WIDTH], (batch, SUBLANES, LRU_WIDTH))
    pool0 = jnp.broadcast_to(proj_m[None, :, 2 * LRU_WIDTH:], (batch, N_META, POOL_WIDTH))
    h0 = jnp.broadcast_to(h_meta, (batch, SUBLANES, LRU_WIDTH))

    proj_p = _proj(xp, g1, w_in_b, PROMPT_TILE)
    proj_s = _proj(xs_tok, g1, w_in_b, dec)
    yab_p, h_p = _mix_seq(proj_p, batch, seq, TIME_TILE, N_META, conv0, h0, pool0, mixw)
    yab_s, h_s = _mix_step(proj_s, jnp.swapaxes(state_conv[l], 0, 1), state_rglru[l],
                           jnp.swapaxes(state_pool[l], 0, 1), mixw)

    out_w = (w_out_b, row(norm2[l]), router_w[l].T, router_b[l].reshape(N_EXPERTS, 1))
    h1_p, xn_p, eid_p, wgt_p = _out(yab_p, xp, *out_w, PROMPT_TILE)
    h1_s, xn_s, eid_s, wgt_s = _out(yab_s, xs_tok, *out_w, dec)

    n_ptiles = n_prompt // PROMPT_TILE
    tile_lens = [PROMPT_TILE] * n_ptiles + [dec]
    seg_pad = (BF16_ROWS - 1) * N_EXPERTS
    kt_p = _ceil_to(TOP_K * PROMPT_TILE + seg_pad, PERM_CHUNK)
    kt_s = _ceil_to(TOP_K * dec + seg_pad, PERM_CHUNK)
    units_max = (TOP_K * n_tok + len(tile_lens) * seg_pad) // MOE_UNIT + N_EXPERTS
    xs_rows = units_max * MOE_UNIT
    n_groups_max = (units_max + N_EXPERTS * (MOE_GROUP_UNITS - 1)) // MOE_GROUP_UNITS

    eid = jnp.concatenate([eid_p, eid_s], axis=1)
    dest, seg_tabs, tail_tabs, group_tabs = _routing_tables(eid, tile_lens, n_groups_max)
    disp_tabs = seg_tabs + tail_tabs
    dest_p, dest_s = dest[:, :n_prompt], dest[:, n_prompt:]

    xs = _dispatch(disp_tabs, xn_p, dest_p, None, 0, PROMPT_TILE, kt_p, xs_rows, False)
    xs = _dispatch(disp_tabs, xn_s, dest_s, xs, n_ptiles, dec, kt_s, xs_rows, True)
    ys = _moe(*group_tabs, xs, exp_wg[l], exp_wu[l], exp_wd[l], exp_bg[l], exp_bu[l], exp_bd[l], n_groups_max)
    fn = row(final_norm)
    y_p = _combine(seg_tabs, ys, h1_p, dest_p.T, wgt_p.T, fn, 0, PROMPT_TILE, kt_p)
    y_s = _combine(seg_tabs, ys, h1_s, dest_s.T, wgt_s.T, fn, n_ptiles, dec, kt_s)

    proj_p3 = proj_p.reshape(batch, seq, IN_WIDTH)
    conv_p = proj_p3[:, seq - (CONV_WIDTH - 1):, :LRU_WIDTH]
    pool_p = proj_p3[:, seq - POOL_BUF:, 2 * LRU_WIDTH:]
    conv_s = jnp.concatenate([state_conv[l][:, 1:], proj_s[:, None, :LRU_WIDTH]], axis=1)
    pool_s = jnp.concatenate([state_pool[l][:, 1:], proj_s[:, None, 2 * LRU_WIDTH:]], axis=1)
    return (y_p.reshape(batch, seq, D_MODEL), y_s.reshape(dec, 1, D_MODEL),
            conv_p[None], h_p[None, :, SUBLANES - 1], pool_p[None],
            conv_s[None], h_s[None], pool_s[None])
```

```python
import functools

import numpy as np
import jax
import jax.numpy as jnp
from jax import lax
from jax.experimental import pallas as pl
from jax.experimental.pallas import tpu as pltpu

F32 = jnp.float32
BF16 = jnp.bfloat16
I32 = jnp.int32
U32 = jnp.uint32

D_MODEL = 2048
N_META = 16
LRU_WIDTH = 1024
LRU_HEADS = 16
LRU_HEAD_DIM = LRU_WIDTH // LRU_HEADS
CONV_WIDTH = 4
LRU_C = 8.0
POOL_WIDTH = D_MODEL - LRU_WIDTH
POOL_WINDOWS = (2, 4, 8, 16)
POOL_GROUP_DIM = POOL_WIDTH // len(POOL_WINDOWS)
POOL_BUF = max(POOL_WINDOWS) - 1
IN_WIDTH = 2 * LRU_WIDTH + POOL_WIDTH
N_EXPERTS = 32
TOP_K = 4
D_FF = D_MODEL
SWIGLU_ALPHA = 1.702
SWIGLU_LIMIT = 7.0
RMS_EPS = 1e-6

V7X_MXU_DIM = 256
SUBLANES = 8
BF16_ROWS = 16

PROMPT_TILE = 512
TIME_TILE = 256
MOE_UNIT = 128
MOE_TM_UNITS = 8
MOE_TM = MOE_TM_UNITS * MOE_UNIT
MOE_GROUP_UNITS = 11
MOE_GROUP_ROWS = MOE_UNIT * MOE_GROUP_UNITS
MOE_TF = V7X_MXU_DIM
PERM_CHUNK = V7X_MXU_DIM
ROUTE_CHUNK = 512


def _params(vmem_mb, sem):
    return pltpu.CompilerParams(dimension_semantics=sem, vmem_limit_bytes=vmem_mb << 20)


def _rmsnorm(x, g):
    return x * lax.rsqrt(jnp.mean(x * x, axis=-1, keepdims=True) + RMS_EPS) * g


def _resident(shape):
    return pl.BlockSpec(shape, lambda *_: (0,) * len(shape), pipeline_mode=pl.Buffered(1))


def _proj_kernel(x_ref, g_ref, w_ref, o_ref):
    xn = _rmsnorm(x_ref[...], g_ref[...])
    o_ref[...] = jnp.dot(xn.astype(BF16), w_ref[...], preferred_element_type=F32)


def _proj(x, g, w_bf16, tile):
    rows = x.shape[0]
    return pl.pallas_call(
        _proj_kernel,
        grid=(rows // tile,),
        in_specs=[pl.BlockSpec((tile, D_MODEL), lambda i: (i, 0)),
                  _resident((1, D_MODEL)), _resident((D_MODEL, IN_WIDTH))],
        out_specs=pl.BlockSpec((tile, IN_WIDTH), lambda i: (i, 0)),
        out_shape=jax.ShapeDtypeStruct((rows, IN_WIDTH), F32),
        compiler_params=_params(48, ("arbitrary",)),
        name="proj",
    )(x, g, w_bf16)


def _lru_coeffs(xc, wa_ref, ba_ref, wx_ref, bx_ref, lam_ref):
    xcb = xc.astype(BF16)
    nq = LRU_WIDTH // V7X_MXU_DIM
    rs, gs = [], []
    for q in range(nq):
        blk = xcb[:, q * V7X_MXU_DIM:(q + 1) * V7X_MXU_DIM]
        rs.append(jnp.dot(blk, wa_ref[q], preferred_element_type=F32))
        gs.append(jnp.dot(blk, wx_ref[q], preferred_element_type=F32))
    r = jax.nn.sigmoid(jnp.concatenate(rs, axis=-1) + ba_ref[...])
    i = jax.nn.sigmoid(jnp.concatenate(gs, axis=-1) + bx_ref[...])
    log_a = -LRU_C * r * jax.nn.softplus(-lam_ref[...])
    a = jnp.exp(log_a)
    th = jnp.tanh(log_a)
    beta = jnp.sqrt(-2.0 * th / (1.0 - th))
    return a, beta * i * xc


def _pool_project(wins, u, inv_cnt, pw_ref, ps_ref):
    outs = []
    for g in range(len(POOL_WINDOWS)):
        sl = slice(g * POOL_GROUP_DIM, (g + 1) * POOL_GROUP_DIM)
        d = (wins[g] * inv_cnt[g] - u[:, sl]).astype(BF16)
        outs.append(jnp.dot(d, pw_ref[g], preferred_element_type=F32))
    return jnp.concatenate(outs, axis=-1) * ps_ref[...]


def _mix_seq_kernel(xa_ref, ga_ref, ub_ref, conv0_ref, h0_ref, pool0_ref,
                    cw_ref, cb_ref, wa_ref, ba_ref, wx_ref, bx_ref, lam_ref, pw_ref, ps_ref,
                    yab_ref, ht_ref, cext, pext, hcar, *, tt, start):
    t = pl.program_id(1)
    hist = 2 * SUBLANES

    @pl.when(t == 0)
    def _():
        cext[0:SUBLANES, :] = conv0_ref[...]
        pext[0:hist, :] = pool0_ref[...]
        hcar[...] = h0_ref[...]

    xa = xa_ref[...]
    cext[SUBLANES:SUBLANES + tt, :] = xa
    cw = cw_ref[...]
    xc = cb_ref[...] + cw[CONV_WIDTH - 1:CONV_WIDTH] * xa
    for k in range(1, CONV_WIDTH):
        xc = xc + cw[CONV_WIDTH - 1 - k:CONV_WIDTH - k] * cext[SUBLANES - k:SUBLANES - k + tt, :]
    cext[0:SUBLANES, :] = cext[tt:tt + SUBLANES, :]

    a, b = _lru_coeffs(xc, wa_ref, ba_ref, wx_ref, bx_ref, lam_ref)

    row = lax.broadcasted_iota(I32, (tt, LRU_WIDTH), 0)
    s = 1
    while s < tt:
        keep = row >= s
        a_prev = jnp.where(keep, pltpu.roll(a, s, 0), 1.0)
        b_prev = jnp.where(keep, pltpu.roll(b, s, 0), 0.0)
        b = b + a * b_prev
        a = a * a_prev
        s *= 2
    hs = a * hcar[SUBLANES - 1:SUBLANES, :] + b
    hcar[...] = hs[tt - SUBLANES:tt, :]
    ht_ref[...] = hs[tt - SUBLANES:tt, :]
    ya = hs * jax.nn.gelu(ga_ref[...])

    u = ub_ref[...]
    pext[hist:hist + tt, :] = u
    e = pext[...]
    wins = []
    shift = 1
    for g in range(len(POOL_WINDOWS)):
        e = e + pltpu.roll(e, shift, 0)
        wins.append(e[hist:hist + tt, :POOL_GROUP_DIM])
        if g + 1 < len(POOL_WINDOWS):
            e = e[:, POOL_GROUP_DIM:]
        shift *= 2
    pext[0:hist, :] = pext[tt:tt + hist, :]

    if start >= POOL_BUF:
        inv_cnt = [1.0 / w for w in POOL_WINDOWS]
    else:
        pos = start + t * tt + lax.broadcasted_iota(I32, (tt, 1), 0)
        inv_cnt = [1.0 / jnp.minimum(w, pos + 1).astype(F32) for w in POOL_WINDOWS]
    yb = _pool_project(wins, u, inv_cnt, pw_ref, ps_ref)

    yab_ref[:, :LRU_WIDTH] = ya.astype(BF16)
    yab_ref[:, LRU_WIDTH:] = yb.astype(BF16)


def _mix_seq(proj, batch, seq, tt, start, conv0, h0, pool0, mixw):
    nt = seq // tt
    hist = 2 * SUBLANES
    col = lambda c: pl.BlockSpec((tt, LRU_WIDTH), lambda b, t: (b * nt + t, c))
    state = lambda r: pl.BlockSpec((None, r, LRU_WIDTH), lambda b, t: (b, 0, 0))
    return pl.pallas_call(
        functools.partial(_mix_seq_kernel, tt=tt, start=start),
        grid=(batch, nt),
        in_specs=[col(0), col(1), col(2), state(SUBLANES), state(SUBLANES), state(hist)]
        + [_resident(w.shape) for w in mixw],
        out_specs=[pl.BlockSpec((tt, D_MODEL), lambda b, t: (b * nt + t, 0)), state(SUBLANES)],
        out_shape=[jax.ShapeDtypeStruct((batch * seq, D_MODEL), BF16),
                   jax.ShapeDtypeStruct((batch, SUBLANES, LRU_WIDTH), F32)],
        scratch_shapes=[pltpu.VMEM((tt + SUBLANES, LRU_WIDTH), F32),
                        pltpu.VMEM((tt + hist, POOL_WIDTH), F32),
                        pltpu.VMEM((SUBLANES, LRU_WIDTH), F32)],
        compiler_params=_params(48, ("arbitrary", "arbitrary")),
        name="mix_seq",
    )(proj, proj, proj, conv0, h0, pool0, *mixw)


def _mix_step_kernel(xa_ref, ga_ref, ub_ref, sconv_ref, sh_ref, spool_ref,
                     cw_ref, cb_ref, wa_ref, ba_ref, wx_ref, bx_ref, lam_ref, pw_ref, ps_ref,
                     yab_ref, h_ref):
    xa = xa_ref[...]
    cw = cw_ref[...]
    xc = cb_ref[...] + cw[CONV_WIDTH - 1:CONV_WIDTH] * xa
    for k in range(1, CONV_WIDTH):
        xc = xc + cw[CONV_WIDTH - 1 - k:CONV_WIDTH - k] * sconv_ref[CONV_WIDTH - 1 - k]
    a, b = _lru_coeffs(xc, wa_ref, ba_ref, wx_ref, bx_ref, lam_ref)
    h = a * sh_ref[...] + b
    h_ref[...] = h
    ya = h * jax.nn.gelu(ga_ref[...])

    u = ub_ref[...]
    wins = []
    for g, w in enumerate(POOL_WINDOWS):
        sl = slice(g * POOL_GROUP_DIM, (g + 1) * POOL_GROUP_DIM)
        acc = u[:, sl]
        for k in range(1, w):
            acc = acc + spool_ref[POOL_BUF - k, :, sl]
        wins.append(acc)
    yb = _pool_project(wins, u, [1.0 / w for w in POOL_WINDOWS], pw_ref, ps_ref)
    yab_ref[:, :LRU_WIDTH] = ya.astype(BF16)
    yab_ref[:, LRU_WIDTH:] = yb.astype(BF16)


def _mix_step(proj, sconv_t, sh, spool_t, mixw):
    rows = proj.shape[0]
    col = lambda c: pl.BlockSpec((rows, LRU_WIDTH), lambda i: (0, c))
    full = lambda a: pl.BlockSpec(a.shape, lambda i: (0,) * a.ndim)
    return pl.pallas_call(
        _mix_step_kernel,
        grid=(1,),
        in_specs=[col(0), col(1), col(2), full(sconv_t), full(sh), full(spool_t)] + [full(w) for w in mixw],
        out_specs=[pl.BlockSpec((rows, D_MODEL), lambda i: (0, 0)),
                   pl.BlockSpec((rows, LRU_WIDTH), lambda i: (0, 0))],
        out_shape=[jax.ShapeDtypeStruct((rows, D_MODEL), BF16),
                   jax.ShapeDtypeStruct((rows, LRU_WIDTH), F32)],
        compiler_params=_params(48, ("arbitrary",)),
        name="mix_step",
    )(proj, proj, proj, sconv_t, sh, spool_t, *mixw)


def _out_kernel(yab_ref, x_ref, wo_ref, g2_ref, rw_both_ref, rw_hi_ref, rb_ref,
                h1_ref, xn_ref, eid_ref, wgt_ref, *, tile):
    h1_ref[...] = x_ref[...] + jnp.dot(yab_ref[...], wo_ref[...], preferred_element_type=F32)

    chunk = min(ROUTE_CHUNK, tile)

    def route(c, carry):
        rows = pl.ds(pl.multiple_of(c * chunk, chunk), chunk)
        xn = _rmsnorm(h1_ref[rows, :], g2_ref[...])
        xn_hi = xn.astype(BF16)
        xn_ref[rows, :] = xn_hi
        xn_lo = (xn - xn_hi.astype(F32)).astype(BF16)
        nt_dims = (((1,), (1,)), ((), ()))
        both = lax.dot_general(rw_both_ref[...], xn_hi, nt_dims, preferred_element_type=F32)
        logits = (both[:N_EXPERTS] + both[N_EXPERTS:]
                  + lax.dot_general(rw_hi_ref[...], xn_lo, nt_dims, preferred_element_type=F32)) + rb_ref[...]
        eidx = lax.broadcasted_iota(I32, (N_EXPERTS, chunk), 0)
        vals, ids = [], []
        for _ in range(TOP_K):
            m = jnp.max(logits, axis=0, keepdims=True)
            idx = jnp.min(jnp.where(logits == m, eidx, N_EXPERTS), axis=0, keepdims=True)
            vals.append(m)
            ids.append(idx)
            logits = jnp.where(eidx == idx, -jnp.inf, logits)
        ex = [jnp.exp(v - vals[0]) for v in vals]
        tot = ex[0]
        for e_ in ex[1:]:
            tot = tot + e_
        eid_ref[:, rows] = jnp.concatenate(ids, axis=0)
        wgt_ref[:, rows] = jnp.concatenate([e_ / tot for e_ in ex], axis=0)
        return carry
    lax.fori_loop(0, tile // chunk, route, 0)


def _out(yab, x, wo_bf16, g2, rw, rb, tile):
    rw_hi = rw.T.astype(BF16)
    rw_lo = (rw.T - rw_hi.astype(F32)).astype(BF16)
    rw_both = jnp.concatenate([rw_hi, rw_lo], axis=0)
    n = x.shape[0]
    rowblk = lambda w: pl.BlockSpec((tile, w), lambda i: (i, 0))
    colblk = pl.BlockSpec((TOP_K, tile), lambda i: (0, i))
    return pl.pallas_call(
        functools.partial(_out_kernel, tile=tile),
        grid=(n // tile,),
        in_specs=[rowblk(D_MODEL), rowblk(D_MODEL), _resident((D_MODEL, D_MODEL)), _resident((1, D_MODEL)),
                  _resident((2 * N_EXPERTS, D_MODEL)), _resident((N_EXPERTS, D_MODEL)), _resident((N_EXPERTS, 1))],
        out_specs=[rowblk(D_MODEL), rowblk(D_MODEL), colblk, colblk],
        out_shape=[jax.ShapeDtypeStruct((n, D_MODEL), F32), jax.ShapeDtypeStruct((n, D_MODEL), BF16),
                   jax.ShapeDtypeStruct((TOP_K, n), I32), jax.ShapeDtypeStruct((TOP_K, n), F32)],
        compiler_params=_params(48, ("arbitrary",)),
        name="out_router",
    )(yab, x, wo_bf16, g2, rw_both, rw_hi, rb)


def _for_segments(tile, seg_len_ref, max_len, fn):
    def per_expert(e, c):
        s = tile * N_EXPERTS + e
        length = seg_len_ref[s]
        size = max_len
        while size >= BF16_ROWS:
            @pl.when((length // size) % 2 == 1)
            def _(size=size):
                fn(s, length // (2 * size) * (2 * size), size)
            size //= 2
        return c
    lax.fori_loop(0, N_EXPERTS, per_expert, 0)


def _seg_rows(ref, start, rows):
    return ref.at[pl.ds(pl.multiple_of(start, BF16_ROWS), rows)]


def _wait_rows(total, max_rows, copy_of_rows):
    size = 1 << (max_rows.bit_length() - 1)
    while size >= BF16_ROWS:
        @pl.when((total // size) % 2 == 1)
        def _(size=size):
            copy_of_rows(size).wait()
        size //= 2


def _dispatch_kernel(seg_loc_ref, seg_dst_ref, seg_len_ref, tile_rows_ref, tail_dst_ref, tail_len_ref,
                     xn_ref, dest_ref, *rest, tile0, kt, zero_tails):
    xs_hbm, cb, zb, sem, zsem = rest[-5:]
    i = pl.program_id(0)
    n_steps = pl.num_programs(0)
    slot = i % 2
    tile = tile0 + i
    tt = xn_ref.shape[0]

    def seg_copy(sl, s, off, rows):
        return pltpu.make_async_copy(_seg_rows(cb.at[sl], seg_loc_ref[s] + off, rows),
                                     _seg_rows(xs_hbm, seg_dst_ref[s] + off, rows), sem.at[sl])

    def start_segments(tl, sl):
        _for_segments(tl, seg_len_ref, tt, lambda s, off, rows: seg_copy(sl, s, off, rows).start())

    def wait_segments(tl, sl):
        _wait_rows(tile_rows_ref[tl], kt, lambda rows: pltpu.make_async_copy(
            cb.at[sl, pl.ds(0, rows)], xs_hbm.at[pl.ds(0, rows)], sem.at[sl]))

    @pl.when(i >= 2)
    def _():
        wait_segments(tile - 2, slot)

    x = xn_ref[...]
    d = [dest_ref[k:k + 1, :] for k in range(TOP_K)]
    for c0 in range(0, kt, PERM_CHUNK):
        r = c0 + lax.broadcasted_iota(I32, (PERM_CHUNK, tt), 0)
        p = jnp.zeros((PERM_CHUNK, tt), F32)
        for k in range(TOP_K):
            p = jnp.where(d[k] == r, 1.0, p)
        rows = jnp.dot(p.astype(BF16), x, preferred_element_type=F32)
        cb[slot, c0:c0 + PERM_CHUNK, :] = rows.astype(BF16)

    start_segments(tile, slot)

    @pl.when(i == n_steps - 1)
    def _():
        @pl.when(i >= 1)
        def _():
            wait_segments(tile - 1, 1 - slot)
        wait_segments(tile, slot)

    if zero_tails:
        zb[...] = jnp.zeros(zb.shape, BF16)

        def tail_copy(e, j):
            return pltpu.make_async_copy(zb, _seg_rows(xs_hbm, tail_dst_ref[e] + j * BF16_ROWS, BF16_ROWS),
                                         zsem.at[0])

        def tails(method):
            def per_expert(e, c):
                def per_chunk(j, c2):
                    getattr(tail_copy(e, j), method)()
                    return c2
                lax.fori_loop(0, tail_len_ref[e] // BF16_ROWS, per_chunk, 0)
                return c
            lax.fori_loop(0, N_EXPERTS, per_expert, 0)

        @pl.when(i == n_steps - 1)
        def _():
            tails("start")
            tails("wait")


def _dispatch(tables, xn, dest, xs, tile0, tt, kt, xs_rows, zero_tails):
    n = xn.shape[0]
    in_specs = [pl.BlockSpec((tt, D_MODEL), lambda i, *_: (i, 0)),
                pl.BlockSpec((TOP_K, tt), lambda i, *_: (0, i))]
    args = [*tables, xn, dest]
    aliases = {}
    if xs is not None:
        in_specs.append(pl.BlockSpec(memory_space=pl.ANY))
        aliases = {len(args): 0}
        args.append(xs)
    grid_spec = pltpu.PrefetchScalarGridSpec(
        num_scalar_prefetch=len(tables),
        grid=(n // tt,),
        in_specs=in_specs,
        out_specs=pl.BlockSpec(memory_space=pl.ANY),
        scratch_shapes=[pltpu.VMEM((2, kt, D_MODEL), BF16), pltpu.VMEM((BF16_ROWS, D_MODEL), BF16),
                        pltpu.SemaphoreType.DMA((2,)), pltpu.SemaphoreType.DMA((1,))],
    )
    return pl.pallas_call(
        functools.partial(_dispatch_kernel, tile0=tile0, kt=kt, zero_tails=zero_tails),
        grid_spec=grid_spec,
        out_shape=jax.ShapeDtypeStruct((xs_rows, D_MODEL), BF16),
        input_output_aliases=aliases,
        compiler_params=_params(48, ("arbitrary",)),
        name="dispatch",
    )(*args)


def _moe_kernel(ge_ref, grow_ref, gnu_ref, ng_ref,
                xs_hbm, wg_hbm, wu_hbm, wd_hbm, bg_ref, bu_ref, bd_ref,
                ys_hbm, xb, acc, ypk, wgf, wuf, wdf, xsem, wsem, osem, *, n_f):
    g = pl.program_id(0)
    n_groups = ng_ref[0]
    xslot = g % 2

    def unit_rows(j):
        return pl.ds(pl.multiple_of(j * MOE_UNIT, MOE_UNIT), MOE_UNIT)

    def hbm_unit(ref, grp, j):
        return ref.at[pl.ds(pl.multiple_of(grow_ref[grp] + j * MOE_UNIT, MOE_UNIT), MOE_UNIT)]

    def copy_in(grp, sl, j):
        return pltpu.make_async_copy(hbm_unit(xs_hbm, grp, j), xb.at[sl, unit_rows(j)], xsem.at[sl])

    def copy_out(grp, j):
        return pltpu.make_async_copy(ypk.at[unit_rows(j)], hbm_unit(ys_hbm, grp, j), osem.at[0])

    def for_units(grp, fn):
        def body(j, c):
            fn(grp, j)
            return c
        lax.fori_loop(0, gnu_ref[grp], body, 0)

    def weight_copies(grp, f, ws):
        e = ge_ref[grp]
        cols = pl.ds(pl.multiple_of(f * MOE_TF, MOE_TF), MOE_TF)
        return (pltpu.make_async_copy(wg_hbm.at[e, :, cols], wgf.at[ws], wsem.at[ws, 0]),
                pltpu.make_async_copy(wu_hbm.at[e, :, cols], wuf.at[ws], wsem.at[ws, 1]),
                pltpu.make_async_copy(wd_hbm.at[e, cols, :], wdf.at[ws], wsem.at[ws, 2]))

    @pl.when(g < n_groups)
    def _():
        @pl.when(g == 0)
        def _():
            for_units(0, lambda grp, j: copy_in(grp, 0, j).start())
            for cp in weight_copies(0, 0, 0):
                cp.start()
        for_units(g, lambda grp, j: copy_in(grp, xslot, j).wait())

        @pl.when(g + 1 < n_groups)
        def _():
            for_units(g + 1, lambda grp, j: copy_in(grp, 1 - xslot, j).start())

        def init(grp, j):
            acc[unit_rows(j), :] = jnp.broadcast_to(bd_ref[...], (MOE_UNIT, D_MODEL))
        for_units(g, init)
        n_units = gnu_ref[g]

        def chunk(f, c):
            ws = f % 2
            for cp in weight_copies(g, f, ws):
                cp.wait()

            @pl.when(f + 1 < n_f)
            def _():
                for cp in weight_copies(g, f + 1, 1 - ws):
                    cp.start()

            @pl.when(jnp.logical_and(f + 1 == n_f, g + 1 < n_groups))
            def _():
                for cp in weight_copies(g + 1, 0, 1 - ws):
                    cp.start()

            bg = bg_ref[pl.ds(f, 1), :]
            bu = bu_ref[pl.ds(f, 1), :]

            def ffn(start, size):
                x = xb[xslot, pl.ds(start, size), :]
                gg = jnp.dot(x, wgf[ws].astype(BF16), preferred_element_type=F32) + bg
                uu = jnp.dot(x, wuf[ws].astype(BF16), preferred_element_type=F32) + bu
                gg = jnp.minimum(gg, SWIGLU_LIMIT)
                uu = jnp.clip(uu, -SWIGLU_LIMIT, SWIGLU_LIMIT)
                hdn = gg * jax.nn.sigmoid(SWIGLU_ALPHA * gg) * (uu + 1.0)
                acc[pl.ds(start, size), :] += jnp.dot(hdn.astype(BF16), wdf[ws].astype(BF16),
                                                      preferred_element_type=F32)

            def full_tile(j, c2):
                ffn(pl.multiple_of(j * MOE_TM, MOE_TM), MOE_TM)
                return c2
            lax.fori_loop(0, n_units // MOE_TM_UNITS, full_tile, 0)
            part = MOE_TM_UNITS // 2
            while part >= 1:
                @pl.when((n_units // part) % 2 == 1)
                def _(part=part):
                    done = n_units // (2 * part) * (2 * part)
                    ffn(pl.multiple_of(done * MOE_UNIT, MOE_UNIT), part * MOE_UNIT)
                part //= 2
            return c
        lax.fori_loop(0, n_f, chunk, 0)

        @pl.when(g > 0)
        def _():
            for_units(g - 1, lambda grp, j: copy_out(grp, j).wait())

        def finish(grp, j):
            ypk[unit_rows(j), :] = acc[unit_rows(j), :].astype(BF16)
            copy_out(grp, j).start()
        for_units(g, finish)

    @pl.when(g == pl.num_programs(0) - 1)
    def _():
        for_units(n_groups - 1, lambda grp, j: copy_out(grp, j).wait())


def _moe(ge, grow, gnu, ng, xs, wg, wu, wd, bg, bu, bd, n_groups_max):
    n_f = D_FF // MOE_TF
    assert n_f % 2 == 0
    per_expert = lambda shape: pl.BlockSpec((None,) + shape, lambda g, ge, gr, gn, ng: (ge[g], 0, 0))
    any_spec = pl.BlockSpec(memory_space=pl.ANY)
    grid_spec = pltpu.PrefetchScalarGridSpec(
        num_scalar_prefetch=4,
        grid=(n_groups_max,),
        in_specs=[any_spec, any_spec, any_spec, any_spec,
                  per_expert((n_f, MOE_TF)), per_expert((n_f, MOE_TF)), per_expert((1, D_MODEL))],
        out_specs=any_spec,
        scratch_shapes=[
            pltpu.VMEM((2, MOE_GROUP_ROWS, D_MODEL), BF16),
            pltpu.VMEM((MOE_GROUP_ROWS, D_MODEL), F32),
            pltpu.VMEM((MOE_GROUP_ROWS, D_MODEL), BF16),
            pltpu.VMEM((2, D_MODEL, MOE_TF), F32),
            pltpu.VMEM((2, D_MODEL, MOE_TF), F32),
            pltpu.VMEM((2, MOE_TF, D_MODEL), F32),
            pltpu.SemaphoreType.DMA((2,)),
            pltpu.SemaphoreType.DMA((2, 3)),
            pltpu.SemaphoreType.DMA((1,)),
        ],
    )
    return pl.pallas_call(
        functools.partial(_moe_kernel, n_f=n_f),
        grid_spec=grid_spec,
        out_shape=jax.ShapeDtypeStruct(xs.shape, BF16),
        compiler_params=_params(56, ("arbitrary",)),
        name="moe_ffn",
    )(ge, grow, gnu, ng, xs, wg, wu, wd,
      bg.reshape(N_EXPERTS, n_f, MOE_TF), bu.reshape(N_EXPERTS, n_f, MOE_TF), bd.reshape(N_EXPERTS, 1, D_MODEL))


def _combine_kernel(seg_loc_ref, seg_src_ref, seg_len_ref, tile_rows_ref,
                    ys_hbm, h1_ref, dest_ref, w_ref, fn_ref, y_ref, sb, wbuf, sem, *, tile0, kt):
    i = pl.program_id(0)
    n_steps = pl.num_programs(0)
    slot = i % 2
    tile = tile0 + i
    tt = h1_ref.shape[0]

    def seg_copy(sl, s, off, rows):
        return pltpu.make_async_copy(_seg_rows(ys_hbm, seg_src_ref[s] + off, rows),
                                     _seg_rows(sb.at[sl], seg_loc_ref[s] + off, rows), sem.at[sl])

    def start_segments(tl, sl):
        _for_segments(tl, seg_len_ref, tt, lambda s, off, rows: seg_copy(sl, s, off, rows).start())

    @pl.when(i == 0)
    def _():
        sb[...] = jnp.zeros(sb.shape, BF16)
        start_segments(tile, slot)

    @pl.when(i + 1 < n_steps)
    def _():
        start_segments(tile + 1, 1 - slot)

    _wait_rows(tile_rows_ref[tile], kt, lambda rows: pltpu.make_async_copy(
        ys_hbm.at[pl.ds(0, rows)], sb.at[slot, pl.ds(0, rows)], sem.at[slot]))

    acc = h1_ref[...]
    d = [jnp.broadcast_to(dest_ref[:, k:k + 1], (tt, PERM_CHUNK)) for k in range(TOP_K)]
    w = [jnp.broadcast_to(w_ref[:, k:k + 1], (tt, PERM_CHUNK)) for k in range(TOP_K)]
    half = kt // 2
    for h in range(2):
        for c0 in range(0, half, PERM_CHUNK):
            col = h * half + c0 + lax.broadcasted_iota(I32, (tt, PERM_CHUNK), 1)
            wm = jnp.zeros((tt, PERM_CHUNK), F32)
            for k in range(TOP_K):
                wm = jnp.where(d[k] == col, w[k], wm)
            wbuf[h, :, c0:c0 + PERM_CHUNK] = wm.astype(BF16)
        acc = acc + jnp.dot(wbuf[h], sb[slot, h * half:(h + 1) * half, :], preferred_element_type=F32)
    y_ref[...] = _rmsnorm(acc, fn_ref[...])


def _combine(tables, ys, h1, dest_t, wgt_t, fn, tile0, tt, kt):
    n = h1.shape[0]
    grid_spec = pltpu.PrefetchScalarGridSpec(
        num_scalar_prefetch=len(tables),
        grid=(n // tt,),
        in_specs=[pl.BlockSpec(memory_space=pl.ANY),
                  pl.BlockSpec((tt, D_MODEL), lambda i, *_: (i, 0)),
                  pl.BlockSpec((tt, TOP_K), lambda i, *_: (i, 0)),
                  pl.BlockSpec((tt, TOP_K), lambda i, *_: (i, 0)),
                  pl.BlockSpec((1, D_MODEL), lambda i, *_: (0, 0))],
        out_specs=pl.BlockSpec((tt, D_MODEL), lambda i, *_: (i, 0)),
        scratch_shapes=[pltpu.VMEM((2, kt, D_MODEL), BF16), pltpu.VMEM((2, tt, kt // 2), BF16),
                        pltpu.SemaphoreType.DMA((2,))],
    )
    return pl.pallas_call(
        functools.partial(_combine_kernel, tile0=tile0, kt=kt),
        grid_spec=grid_spec,
        out_shape=jax.ShapeDtypeStruct((n, D_MODEL), F32),
        compiler_params=_params(56, ("arbitrary",)),
        name="combine",
    )(*tables, ys, h1, dest_t, wgt_t, fn)


def _ceil_to(x, m):
    return (x + m - 1) // m * m


def _routing_tables(eid, tile_lens, n_groups_max):
    n = eid.shape[1]
    nt = len(tile_lens)
    starts = np.concatenate([[0], np.cumsum(tile_lens)[:-1]]).astype(np.int64)
    onehot = (eid[:, :, None] == jnp.arange(N_EXPERTS, dtype=I32)[None, None, :]).astype(I32)
    member = jnp.sum(onehot, axis=0)
    incl = jnp.cumsum(member, axis=0)
    before = incl - member
    rank0 = jnp.stack([before[int(s)] for s in starts])
    cnt = jnp.stack([incl[int(s) + int(l) - 1] for s, l in zip(starts, tile_lens)]) - rank0
    c8 = _ceil_to(cnt, BF16_ROWS)
    seg_loc = jnp.cumsum(c8, axis=1) - c8
    rows_e = jnp.sum(c8, axis=0)
    region_e = _ceil_to(rows_e, MOE_UNIT)
    start_e = jnp.cumsum(region_e) - region_e
    seg_pos = start_e[None, :] + jnp.cumsum(c8, axis=0) - c8

    def per_token(tab):
        return jnp.concatenate([jnp.broadcast_to(tab[t], (int(l), N_EXPERTS)) for t, l in enumerate(tile_lens)])
    base = per_token(seg_loc - rank0) + before
    dest = jnp.sum(onehot * base[None], axis=2).astype(I32)

    ntiles_e = region_e // MOE_UNIT
    groups_e = (ntiles_e + MOE_GROUP_UNITS - 1) // MOE_GROUP_UNITS
    gend_e = jnp.cumsum(groups_e)
    gstart_e = gend_e - groups_e
    n_groups = gend_e[-1]
    j = jnp.arange(n_groups_max, dtype=I32)
    j_act = jnp.minimum(j, n_groups - 1)
    e_j = jnp.minimum(jnp.sum((gend_e[None, :] <= j_act[:, None]).astype(I32), axis=1), N_EXPERTS - 1)
    sel = (e_j[:, None] == jnp.arange(N_EXPERTS, dtype=I32)[None, :]).astype(I32)
    pick = lambda v: jnp.sum(sel * v[None, :], axis=1)
    local = j_act - pick(gstart_e)
    grow = pick(start_e) + local * MOE_GROUP_ROWS
    gnt = jnp.where(j < n_groups, jnp.clip(pick(ntiles_e) - local * MOE_GROUP_UNITS, 0, MOE_GROUP_UNITS), 0)

    flat = lambda a: a.reshape(-1).astype(I32)
    seg_tabs = (flat(seg_loc), flat(seg_pos), flat(c8), jnp.sum(c8, axis=1).astype(I32))
    tail_tabs = ((start_e + rows_e).astype(I32), (region_e - rows_e).astype(I32))
    group_tabs = (e_j.astype(I32), grow.astype(I32), gnt.astype(I32), n_groups.reshape(1).astype(I32))
    return dest, seg_tabs, tail_tabs, group_tabs


def _block_diag(w, per_block):
    h, d, _ = w.shape
    nb = h // per_block
    eye = jnp.eye(per_block, dtype=w.dtype)
    w4 = w.reshape(nb, per_block, d, d)
    out = jnp.einsum("bpij,pq->bpiqj", w4, eye)
    return out.reshape(nb, per_block * d, per_block * d)


def kernel(x_prompt, x_sample, state_conv, state_rglru, state_pool, meta_tokens, norm1, w_in, conv_w, conv_b, lru_wa, lru_ba, lru_wx, lru_bx, lru_lambda, pool_w, pool_scale, w_out, norm2, router_w, router_b, exp_wg, exp_bg, exp_wu, exp_bu, exp_wd, exp_bd, final_norm):
    batch, seq, _ = x_prompt.shape
    dec = x_sample.shape[0]
    n_prompt = batch * seq
    n_tok = n_prompt + dec
    l = 0
    row = lambda v: v.reshape(1, -1)

    xp = x_prompt.reshape(n_prompt, D_MODEL)
    xs_tok = x_sample.reshape(dec, D_MODEL)
    w_in_b = w_in[l].astype(BF16)
    w_out_b = w_out[l].astype(BF16)
    heads_per_block = V7X_MXU_DIM // LRU_HEAD_DIM
    mixw = (conv_w[l], row(conv_b[l]),
            _block_diag(lru_wa[l], heads_per_block).astype(BF16), row(lru_ba[l]),
            _block_diag(lru_wx[l], heads_per_block).astype(BF16), row(lru_bx[l]),
            row(lru_lambda[l]), pool_w[l].astype(BF16), row(pool_scale[l]))
    g1 = row(norm1[l])

    proj_m = _proj(meta_tokens, g1, w_in_b, N_META)
    zeros = lambda r: jnp.zeros((1, r, LRU_WIDTH), F32)
    _, h_meta = _mix_seq(proj_m, 1, N_META, N_META, 0, zeros(SUBLANES), zeros(SUBLANES), zeros(2 * SUBLANES), mixw)
    conv0 = jnp.broadcast_to(proj_m[None, N_META - SUBLANES:, :LRU_WIDTH], (batch, SUBLANES, LRU_WIDTH))
    pool0 = jnp.broadcast_to(proj_m[None, :, 2 * LRU_WIDTH:], (batch, N_META, POOL_WIDTH))
    h0 = jnp.broadcast_to(h_meta, (batch, SUBLANES, LRU_WIDTH))

    proj_p = _proj(xp, g1, w_in_b, PROMPT_TILE)
    proj_s = _proj(xs_tok, g1, w_in_b, dec)
    yab_p, h_p = _mix_seq(proj_p, batch, seq, TIME_TILE, N_META, conv0, h0, pool0, mixw)
    yab_s, h_s = _mix_step(proj_s, jnp.swapaxes(state_conv[l], 0, 1), state_rglru[l],
                           jnp.swapaxes(state_pool[l], 0, 1), mixw)

    out_w = (w_out_b, row(norm2[l]), router_w[l], router_b[l].reshape(N_EXPERTS, 1))
    h1_p, xn_p, eid_p, wgt_p = _out(yab_p, xp, *out_w, PROMPT_TILE)
    h1_s, xn_s, eid_s, wgt_s = _out(yab_s, xs_tok, *out_w, dec)

    n_ptiles = n_prompt // PROMPT_TILE
    tile_lens = [PROMPT_TILE] * n_ptiles + [dec]
    seg_pad = (BF16_ROWS - 1) * N_EXPERTS
    kt_p = _ceil_to(TOP_K * PROMPT_TILE + seg_pad, PERM_CHUNK)
    kt_s = _ceil_to(TOP_K * dec + seg_pad, PERM_CHUNK)
    units_max = (TOP_K * n_tok + len(tile_lens) * seg_pad) // MOE_UNIT + N_EXPERTS
    xs_rows = units_max * MOE_UNIT
    n_groups_max = (units_max + N_EXPERTS * (MOE_GROUP_UNITS - 1)) // MOE_GROUP_UNITS

    eid = jnp.concatenate([eid_p, eid_s], axis=1)
    dest, seg_tabs, tail_tabs, group_tabs = _routing_tables(eid, tile_lens, n_groups_max)
    disp_tabs = seg_tabs + tail_tabs
    dest_p, dest_s = dest[:, :n_prompt], dest[:, n_prompt:]

    xs = _dispatch(disp_tabs, xn_p, dest_p, None, 0, PROMPT_TILE, kt_p, xs_rows, False)
    xs = _dispatch(disp_tabs, xn_s, dest_s, xs, n_ptiles, dec, kt_s, xs_rows, True)
    ys = _moe(*group_tabs, xs, exp_wg[l], exp_wu[l], exp_wd[l], exp_bg[l], exp_bu[l], exp_bd[l], n_groups_max)
    fn = row(final_norm)
    y_p = _combine(seg_tabs, ys, h1_p, dest_p.T, wgt_p.T, fn, 0, PROMPT_TILE, kt_p)
    y_s = _combine(seg_tabs, ys, h1_s, dest_s.T, wgt_s.T, fn, n_ptiles, dec, kt_s)

    proj_p3 = proj_p.reshape(batch, seq, IN_WIDTH)
    conv_p = proj_p3[:, seq - (CONV_WIDTH - 1):, :LRU_WIDTH]
    pool_p = proj_p3[:, seq - POOL_BUF:, 2 * LRU_WIDTH:]
    conv_s = jnp.concatenate([state_conv[l][:, 1:], proj_s[:, None, :LRU_WIDTH]], axis=1)
    pool_s = jnp.concatenate([state_pool[l][:, 1:], proj_s[:, None, 2 * LRU_WIDTH:]], axis=1)
    return (y_p.reshape(batch, seq, D_MODEL), y_s.reshape(dec, 1, D_MODEL),
            conv_p[None], h_p[None, :, SUBLANES - 1], pool_p[None],
            conv_s[None], h_s[None], pool_s[None])
```

```python
import functools

import numpy as np
import jax
import jax.numpy as jnp
from jax import lax
from jax.experimental import pallas as pl
from jax.experimental.pallas import tpu as pltpu

F32 = jnp.float32
BF16 = jnp.bfloat16
I32 = jnp.int32
U32 = jnp.uint32

D_MODEL = 2048
N_META = 16
LRU_WIDTH = 1024
LRU_HEADS = 16
LRU_HEAD_DIM = LRU_WIDTH // LRU_HEADS
CONV_WIDTH = 4
LRU_C = 8.0
POOL_WIDTH = D_MODEL - LRU_WIDTH
POOL_WINDOWS = (2, 4, 8, 16)
POOL_GROUP_DIM = POOL_WIDTH // len(POOL_WINDOWS)
POOL_BUF = max(POOL_WINDOWS) - 1
IN_WIDTH = 2 * LRU_WIDTH + POOL_WIDTH
N_EXPERTS = 32
TOP_K = 4
D_FF = D_MODEL
SWIGLU_ALPHA = 1.702
SWIGLU_LIMIT = 7.0
RMS_EPS = 1e-6

V7X_MXU_DIM = 256
SUBLANES = 8
BF16_ROWS = 16

PROMPT_TILE = 512
TIME_TILE = 256
MOE_UNIT = 128
MOE_TM_UNITS = 8
MOE_TM = MOE_TM_UNITS * MOE_UNIT
MOE_GROUP_UNITS = 11
MOE_GROUP_ROWS = MOE_UNIT * MOE_GROUP_UNITS
MOE_WHOLE_GROUP_UNITS = (9, 10)
MOE_TF = V7X_MXU_DIM
PERM_CHUNK = V7X_MXU_DIM
ROUTE_CHUNK = 512


def _params(vmem_mb, sem):
    return pltpu.CompilerParams(dimension_semantics=sem, vmem_limit_bytes=vmem_mb << 20)


def _rmsnorm(x, g):
    return x * lax.rsqrt(jnp.mean(x * x, axis=-1, keepdims=True) + RMS_EPS) * g


def _resident(shape):
    return pl.BlockSpec(shape, lambda *_: (0,) * len(shape), pipeline_mode=pl.Buffered(1))


def _proj_kernel(x_ref, g_ref, w_ref, o_ref):
    xn = _rmsnorm(x_ref[...], g_ref[...])
    o_ref[...] = jnp.dot(xn.astype(BF16), w_ref[...], preferred_element_type=F32)


def _proj(x, g, w_bf16, tile):
    rows = x.shape[0]
    return pl.pallas_call(
        _proj_kernel,
        grid=(rows // tile,),
        in_specs=[pl.BlockSpec((tile, D_MODEL), lambda i: (i, 0)),
                  _resident((1, D_MODEL)), _resident((D_MODEL, IN_WIDTH))],
        out_specs=pl.BlockSpec((tile, IN_WIDTH), lambda i: (i, 0)),
        out_shape=jax.ShapeDtypeStruct((rows, IN_WIDTH), F32),
        compiler_params=_params(48, ("arbitrary",)),
        name="proj",
    )(x, g, w_bf16)


def _lru_coeffs(xc, wa_ref, ba_ref, wx_ref, bx_ref, lam_ref):
    xcb = xc.astype(BF16)
    nq = LRU_WIDTH // V7X_MXU_DIM
    rs, gs = [], []
    for q in range(nq):
        blk = xcb[:, q * V7X_MXU_DIM:(q + 1) * V7X_MXU_DIM]
        rs.append(jnp.dot(blk, wa_ref[q], preferred_element_type=F32))
        gs.append(jnp.dot(blk, wx_ref[q], preferred_element_type=F32))
    r = jax.nn.sigmoid(jnp.concatenate(rs, axis=-1) + ba_ref[...])
    i = jax.nn.sigmoid(jnp.concatenate(gs, axis=-1) + bx_ref[...])
    log_a = -LRU_C * r * jax.nn.softplus(-lam_ref[...])
    a = jnp.exp(log_a)
    th = jnp.tanh(log_a)
    beta = jnp.sqrt(-2.0 * th / (1.0 - th))
    return a, beta * i * xc


def _pool_project(wins, u, inv_cnt, pw_ref, ps_ref):
    outs = []
    for g in range(len(POOL_WINDOWS)):
        sl = slice(g * POOL_GROUP_DIM, (g + 1) * POOL_GROUP_DIM)
        d = (wins[g] * inv_cnt[g] - u[:, sl]).astype(BF16)
        outs.append(jnp.dot(d, pw_ref[g], preferred_element_type=F32))
    return jnp.concatenate(outs, axis=-1) * ps_ref[...]


def _mix_seq_kernel(x_ref, g1_ref, win_ref, conv0_ref, h0_ref, pool0_ref,
                    cw_ref, cb_ref, wa_ref, ba_ref, wx_ref, bx_ref, lam_ref, pw_ref, ps_ref,
                    yab_ref, ht_ref, ctail_ref, ptail_ref, cext, pext, hcar, *, tt, start):
    t = pl.program_id(1)
    hist = 2 * SUBLANES

    @pl.when(t == 0)
    def _():
        cext[0:SUBLANES, :] = conv0_ref[...]
        pext[0:hist, :] = pool0_ref[...]
        hcar[...] = h0_ref[...]

    xnb = _rmsnorm(x_ref[...], g1_ref[...]).astype(BF16)
    xa = jnp.dot(xnb, win_ref[:, :LRU_WIDTH], preferred_element_type=F32)
    ga = jnp.dot(xnb, win_ref[:, LRU_WIDTH:2 * LRU_WIDTH], preferred_element_type=F32)
    u = jnp.dot(xnb, win_ref[:, 2 * LRU_WIDTH:], preferred_element_type=F32)

    cext[SUBLANES:SUBLANES + tt, :] = xa
    cw = cw_ref[...]
    xc = cb_ref[...] + cw[CONV_WIDTH - 1:CONV_WIDTH] * xa
    for k in range(1, CONV_WIDTH):
        xc = xc + cw[CONV_WIDTH - 1 - k:CONV_WIDTH - k] * cext[SUBLANES - k:SUBLANES - k + tt, :]
    cext[0:SUBLANES, :] = cext[tt:tt + SUBLANES, :]
    ctail_ref[...] = cext[0:SUBLANES, :]

    a, b = _lru_coeffs(xc, wa_ref, ba_ref, wx_ref, bx_ref, lam_ref)

    groups = tt // SUBLANES
    a = a.reshape(groups, SUBLANES, LRU_WIDTH)
    b = b.reshape(groups, SUBLANES, LRU_WIDTH)
    sub = lax.broadcasted_iota(I32, (groups, SUBLANES, LRU_WIDTH), 1)
    s = 1
    while s < SUBLANES:
        keep = sub >= s
        a_prev = jnp.where(keep, pltpu.roll(a, s, 1), 1.0)
        b_prev = jnp.where(keep, pltpu.roll(b, s, 1), 0.0)
        b = b + a * b_prev
        a = a * a_prev
        s *= 2
    h = hcar[SUBLANES - 1:SUBLANES, :]
    hrows = []
    for g in range(groups):
        hg = a[g] * h + b[g]
        hrows.append(hg)
        h = hg[SUBLANES - 1:SUBLANES, :]
    hs = jnp.concatenate(hrows, axis=0)
    hcar[...] = hrows[-1]
    ht_ref[...] = hrows[-1]
    ya = hs * jax.nn.gelu(ga)

    pext[hist:hist + tt, :] = u
    e = pext[...]
    wins = []
    shift = 1
    for g in range(len(POOL_WINDOWS)):
        e = e + pltpu.roll(e, shift, 0)
        wins.append(e[hist:hist + tt, :POOL_GROUP_DIM])
        if g + 1 < len(POOL_WINDOWS):
            e = e[:, POOL_GROUP_DIM:]
        shift *= 2
    pext[0:hist, :] = pext[tt:tt + hist, :]
    ptail_ref[...] = pext[0:hist, :]

    if start >= POOL_BUF:
        inv_cnt = [1.0 / w for w in POOL_WINDOWS]
    else:
        pos = start + t * tt + lax.broadcasted_iota(I32, (tt, 1), 0)
        inv_cnt = [1.0 / jnp.minimum(w, pos + 1).astype(F32) for w in POOL_WINDOWS]
    yb = _pool_project(wins, u, inv_cnt, pw_ref, ps_ref)

    yab_ref[:, :LRU_WIDTH] = ya.astype(BF16)
    yab_ref[:, LRU_WIDTH:] = yb.astype(BF16)


def _mix_seq(x, g1, w_in_bf16, batch, seq, tt, start, conv0, h0, pool0, mixw):
    nt = seq // tt
    hist = 2 * SUBLANES
    rows = lambda w: pl.BlockSpec((tt, w), lambda b, t: (b * nt + t, 0))
    state = lambda r: pl.BlockSpec((None, r, LRU_WIDTH), lambda b, t: (b, 0, 0))
    tail = lambda r: jax.ShapeDtypeStruct((batch, r, LRU_WIDTH), F32)
    return pl.pallas_call(
        functools.partial(_mix_seq_kernel, tt=tt, start=start),
        grid=(batch, nt),
        in_specs=[rows(D_MODEL), _resident((1, D_MODEL)), _resident((D_MODEL, IN_WIDTH)),
                  state(SUBLANES), state(SUBLANES), state(hist)] + [_resident(w.shape) for w in mixw],
        out_specs=[rows(D_MODEL), state(SUBLANES), state(SUBLANES), state(hist)],
        out_shape=[jax.ShapeDtypeStruct((batch * seq, D_MODEL), BF16), tail(SUBLANES), tail(SUBLANES), tail(hist)],
        scratch_shapes=[pltpu.VMEM((tt + SUBLANES, LRU_WIDTH), F32),
                        pltpu.VMEM((tt + hist, POOL_WIDTH), F32),
                        pltpu.VMEM((SUBLANES, LRU_WIDTH), F32)],
        compiler_params=_params(56, ("arbitrary", "arbitrary")),
        name="mix_seq",
    )(x, g1, w_in_bf16, conv0, h0, pool0, *mixw)


def _mix_step_kernel(xa_ref, ga_ref, ub_ref, sconv_ref, sh_ref, spool_ref,
                     cw_ref, cb_ref, wa_ref, ba_ref, wx_ref, bx_ref, lam_ref, pw_ref, ps_ref,
                     yab_ref, h_ref):
    xa = xa_ref[...]
    cw = cw_ref[...]
    xc = cb_ref[...] + cw[CONV_WIDTH - 1:CONV_WIDTH] * xa
    for k in range(1, CONV_WIDTH):
        xc = xc + cw[CONV_WIDTH - 1 - k:CONV_WIDTH - k] * sconv_ref[CONV_WIDTH - 1 - k]
    a, b = _lru_coeffs(xc, wa_ref, ba_ref, wx_ref, bx_ref, lam_ref)
    h = a * sh_ref[...] + b
    h_ref[...] = h
    ya = h * jax.nn.gelu(ga_ref[...])

    u = ub_ref[...]
    wins = []
    for g, w in enumerate(POOL_WINDOWS):
        sl = slice(g * POOL_GROUP_DIM, (g + 1) * POOL_GROUP_DIM)
        acc = u[:, sl]
        for k in range(1, w):
            acc = acc + spool_ref[POOL_BUF - k, :, sl]
        wins.append(acc)
    yb = _pool_project(wins, u, [1.0 / w for w in POOL_WINDOWS], pw_ref, ps_ref)
    yab_ref[:, :LRU_WIDTH] = ya.astype(BF16)
    yab_ref[:, LRU_WIDTH:] = yb.astype(BF16)


def _mix_step(proj, sconv_t, sh, spool_t, mixw):
    rows = proj.shape[0]
    col = lambda c: pl.BlockSpec((rows, LRU_WIDTH), lambda i: (0, c))
    full = lambda a: pl.BlockSpec(a.shape, lambda i: (0,) * a.ndim)
    return pl.pallas_call(
        _mix_step_kernel,
        grid=(1,),
        in_specs=[col(0), col(1), col(2), full(sconv_t), full(sh), full(spool_t)] + [full(w) for w in mixw],
        out_specs=[pl.BlockSpec((rows, D_MODEL), lambda i: (0, 0)),
                   pl.BlockSpec((rows, LRU_WIDTH), lambda i: (0, 0))],
        out_shape=[jax.ShapeDtypeStruct((rows, D_MODEL), BF16),
                   jax.ShapeDtypeStruct((rows, LRU_WIDTH), F32)],
        compiler_params=_params(48, ("arbitrary",)),
        name="mix_step",
    )(proj, proj, proj, sconv_t, sh, spool_t, *mixw)


def _out_kernel(yab_ref, x_ref, wo_ref, g2_ref, rw_both_ref, rw_hi_ref, rb_ref,
                h1_ref, xn_ref, eid_ref, wgt_ref, *, tile):
    h1_ref[...] = x_ref[...] + jnp.dot(yab_ref[...], wo_ref[...], preferred_element_type=F32)

    chunk = min(ROUTE_CHUNK, tile)

    def route(c, carry):
        rows = pl.ds(pl.multiple_of(c * chunk, chunk), chunk)
        xn = _rmsnorm(h1_ref[rows, :], g2_ref[...])
        xn_hi = xn.astype(BF16)
        xn_ref[rows, :] = xn_hi
        xn_lo = (xn - xn_hi.astype(F32)).astype(BF16)
        nt_dims = (((1,), (1,)), ((), ()))
        both = lax.dot_general(rw_both_ref[...], xn_hi, nt_dims, preferred_element_type=F32)
        logits = (both[:N_EXPERTS] + both[N_EXPERTS:]
                  + lax.dot_general(rw_hi_ref[...], xn_lo, nt_dims, preferred_element_type=F32)) + rb_ref[...]
        eidx = lax.broadcasted_iota(I32, (N_EXPERTS, chunk), 0)
        vals, ids = [], []
        for _ in range(TOP_K):
            m = jnp.max(logits, axis=0, keepdims=True)
            idx = jnp.min(jnp.where(logits == m, eidx, N_EXPERTS), axis=0, keepdims=True)
            vals.append(m)
            ids.append(idx)
            logits = jnp.where(eidx == idx, -jnp.inf, logits)
        ex = [jnp.exp(v - vals[0]) for v in vals]
        tot = ex[0]
        for e_ in ex[1:]:
            tot = tot + e_
        eid_ref[:, rows] = jnp.concatenate(ids, axis=0)
        wgt_ref[:, rows] = jnp.concatenate([e_ / tot for e_ in ex], axis=0)
        return carry
    lax.fori_loop(0, tile // chunk, route, 0)


def _out(yab, x, wo_bf16, g2, rw, rb, tile):
    rw_hi = rw.T.astype(BF16)
    rw_lo = (rw.T - rw_hi.astype(F32)).astype(BF16)
    rw_both = jnp.concatenate([rw_hi, rw_lo], axis=0)
    n = x.shape[0]
    rowblk = lambda w: pl.BlockSpec((tile, w), lambda i: (i, 0))
    colblk = pl.BlockSpec((TOP_K, tile), lambda i: (0, i))
    return pl.pallas_call(
        functools.partial(_out_kernel, tile=tile),
        grid=(n // tile,),
        in_specs=[rowblk(D_MODEL), rowblk(D_MODEL), _resident((D_MODEL, D_MODEL)), _resident((1, D_MODEL)),
                  _resident((2 * N_EXPERTS, D_MODEL)), _resident((N_EXPERTS, D_MODEL)), _resident((N_EXPERTS, 1))],
        out_specs=[rowblk(D_MODEL), rowblk(D_MODEL), colblk, colblk],
        out_shape=[jax.ShapeDtypeStruct((n, D_MODEL), F32), jax.ShapeDtypeStruct((n, D_MODEL), BF16),
                   jax.ShapeDtypeStruct((TOP_K, n), I32), jax.ShapeDtypeStruct((TOP_K, n), F32)],
        compiler_params=_params(48, ("arbitrary",)),
        name="out_router",
    )(yab, x, wo_bf16, g2, rw_both, rw_hi, rb)


def _for_segments(tile, seg_len_ref, max_len, fn):
    def per_expert(e, c):
        s = tile * N_EXPERTS + e
        length = seg_len_ref[s]
        size = max_len
        while size >= BF16_ROWS:
            @pl.when((length // size) % 2 == 1)
            def _(size=size):
                fn(s, length // (2 * size) * (2 * size), size)
            size //= 2
        return c
    lax.fori_loop(0, N_EXPERTS, per_expert, 0)


def _seg_rows(ref, start, rows):
    return ref.at[pl.ds(pl.multiple_of(start, BF16_ROWS), rows)]


def _wait_rows(total, max_rows, copy_of_rows):
    size = 1 << (max_rows.bit_length() - 1)
    while size >= BF16_ROWS:
        @pl.when((total // size) % 2 == 1)
        def _(size=size):
            copy_of_rows(size).wait()
        size //= 2


def _dispatch_kernel(seg_loc_ref, seg_dst_ref, seg_len_ref, tile_rows_ref, tail_dst_ref, tail_len_ref,
                     xn_ref, dest_ref, *rest, tile0, kt, zero_tails):
    xs_hbm, cb, zb, sem, zsem = rest[-5:]
    i = pl.program_id(0)
    n_steps = pl.num_programs(0)
    slot = i % 2
    tile = tile0 + i
    tt = xn_ref.shape[0]

    def seg_copy(sl, s, off, rows):
        return pltpu.make_async_copy(_seg_rows(cb.at[sl], seg_loc_ref[s] + off, rows),
                                     _seg_rows(xs_hbm, seg_dst_ref[s] + off, rows), sem.at[sl])

    def start_segments(tl, sl):
        _for_segments(tl, seg_len_ref, tt, lambda s, off, rows: seg_copy(sl, s, off, rows).start())

    def wait_segments(tl, sl):
        _wait_rows(tile_rows_ref[tl], kt, lambda rows: pltpu.make_async_copy(
            cb.at[sl, pl.ds(0, rows)], xs_hbm.at[pl.ds(0, rows)], sem.at[sl]))

    @pl.when(i >= 2)
    def _():
        wait_segments(tile - 2, slot)

    x = xn_ref[...]
    d = [dest_ref[k:k + 1, :] for k in range(TOP_K)]
    for c0 in range(0, kt, PERM_CHUNK):
        r = c0 + lax.broadcasted_iota(I32, (PERM_CHUNK, tt), 0)
        p = jnp.zeros((PERM_CHUNK, tt), F32)
        for k in range(TOP_K):
            p = jnp.where(d[k] == r, 1.0, p)
        rows = jnp.dot(p.astype(BF16), x, preferred_element_type=F32)
        cb[slot, c0:c0 + PERM_CHUNK, :] = rows.astype(BF16)

    start_segments(tile, slot)

    @pl.when(i == n_steps - 1)
    def _():
        @pl.when(i >= 1)
        def _():
            wait_segments(tile - 1, 1 - slot)
        wait_segments(tile, slot)

    if zero_tails:
        zb[...] = jnp.zeros(zb.shape, BF16)

        def tail_copy(e, j):
            return pltpu.make_async_copy(zb, _seg_rows(xs_hbm, tail_dst_ref[e] + j * BF16_ROWS, BF16_ROWS),
                                         zsem.at[0])

        def tails(method):
            def per_expert(e, c):
                def per_chunk(j, c2):
                    getattr(tail_copy(e, j), method)()
                    return c2
                lax.fori_loop(0, tail_len_ref[e] // BF16_ROWS, per_chunk, 0)
                return c
            lax.fori_loop(0, N_EXPERTS, per_expert, 0)

        @pl.when(i == n_steps - 1)
        def _():
            tails("start")
            tails("wait")


def _dispatch(tables, xn, dest, xs, tile0, tt, kt, xs_rows, zero_tails):
    n = xn.shape[0]
    in_specs = [pl.BlockSpec((tt, D_MODEL), lambda i, *_: (i, 0)),
                pl.BlockSpec((TOP_K, tt), lambda i, *_: (0, i))]
    args = [*tables, xn, dest]
    aliases = {}
    if xs is not None:
        in_specs.append(pl.BlockSpec(memory_space=pl.ANY))
        aliases = {len(args): 0}
        args.append(xs)
    grid_spec = pltpu.PrefetchScalarGridSpec(
        num_scalar_prefetch=len(tables),
        grid=(n // tt,),
        in_specs=in_specs,
        out_specs=pl.BlockSpec(memory_space=pl.ANY),
        scratch_shapes=[pltpu.VMEM((2, kt, D_MODEL), BF16), pltpu.VMEM((BF16_ROWS, D_MODEL), BF16),
                        pltpu.SemaphoreType.DMA((2,)), pltpu.SemaphoreType.DMA((1,))],
    )
    return pl.pallas_call(
        functools.partial(_dispatch_kernel, tile0=tile0, kt=kt, zero_tails=zero_tails),
        grid_spec=grid_spec,
        out_shape=jax.ShapeDtypeStruct((xs_rows, D_MODEL), BF16),
        input_output_aliases=aliases,
        compiler_params=_params(48, ("arbitrary",)),
        name="dispatch",
    )(*args)


def _moe_kernel(ge_ref, grow_ref, gnu_ref, ng_ref,
                xs_hbm, wg_hbm, wu_hbm, wd_hbm, bg_ref, bu_ref, bd_ref,
                ys_hbm, xb, acc, ypk, wgf, wuf, wdf, xsem, wsem, osem, *, n_f):
    g = pl.program_id(0)
    n_groups = ng_ref[0]
    xslot = g % 2

    def unit_rows(j):
        return pl.ds(pl.multiple_of(j * MOE_UNIT, MOE_UNIT), MOE_UNIT)

    def hbm_unit(ref, grp, j):
        return ref.at[pl.ds(pl.multiple_of(grow_ref[grp] + j * MOE_UNIT, MOE_UNIT), MOE_UNIT)]

    def copy_in(grp, sl, j):
        return pltpu.make_async_copy(hbm_unit(xs_hbm, grp, j), xb.at[sl, unit_rows(j)], xsem.at[sl])

    def copy_out(grp, j):
        return pltpu.make_async_copy(ypk.at[unit_rows(j)], hbm_unit(ys_hbm, grp, j), osem.at[0])

    def for_units(grp, fn):
        def body(j, c):
            fn(grp, j)
            return c
        lax.fori_loop(0, gnu_ref[grp], body, 0)

    def weight_copies(grp, f, ws):
        e = ge_ref[grp]
        cols = pl.ds(pl.multiple_of(f * MOE_TF, MOE_TF), MOE_TF)
        return (pltpu.make_async_copy(wg_hbm.at[e, :, cols], wgf.at[ws], wsem.at[ws, 0]),
                pltpu.make_async_copy(wu_hbm.at[e, :, cols], wuf.at[ws], wsem.at[ws, 1]),
                pltpu.make_async_copy(wd_hbm.at[e, cols, :], wdf.at[ws], wsem.at[ws, 2]))

    @pl.when(g < n_groups)
    def _():
        @pl.when(g == 0)
        def _():
            for_units(0, lambda grp, j: copy_in(grp, 0, j).start())
            for cp in weight_copies(0, 0, 0):
                cp.start()
        for_units(g, lambda grp, j: copy_in(grp, xslot, j).wait())

        @pl.when(g + 1 < n_groups)
        def _():
            for_units(g + 1, lambda grp, j: copy_in(grp, 1 - xslot, j).start())

        def init(grp, j):
            acc[unit_rows(j), :] = jnp.broadcast_to(bd_ref[...], (MOE_UNIT, D_MODEL))
        for_units(g, init)
        n_units = gnu_ref[g]

        def chunk(f, c):
            ws = f % 2
            for cp in weight_copies(g, f, ws):
                cp.wait()

            @pl.when(f + 1 < n_f)
            def _():
                for cp in weight_copies(g, f + 1, 1 - ws):
                    cp.start()

            @pl.when(jnp.logical_and(f + 1 == n_f, g + 1 < n_groups))
            def _():
                for cp in weight_copies(g + 1, 0, 1 - ws):
                    cp.start()

            bg = bg_ref[pl.ds(f, 1), :]
            bu = bu_ref[pl.ds(f, 1), :]

            def ffn(start, size):
                x = xb[xslot, pl.ds(start, size), :]
                gg = jnp.dot(x, wgf[ws].astype(BF16), preferred_element_type=F32) + bg
                uu = jnp.dot(x, wuf[ws].astype(BF16), preferred_element_type=F32) + bu
                gg = jnp.minimum(gg, SWIGLU_LIMIT)
                uu = jnp.clip(uu, -SWIGLU_LIMIT, SWIGLU_LIMIT)
                hdn = gg * jax.nn.sigmoid(SWIGLU_ALPHA * gg) * (uu + 1.0)
                acc[pl.ds(start, size), :] += jnp.dot(hdn.astype(BF16), wdf[ws].astype(BF16),
                                                      preferred_element_type=F32)

            whole = jnp.bool_(False)
            for units in MOE_WHOLE_GROUP_UNITS:
                whole = jnp.logical_or(whole, n_units == units)

                @pl.when(n_units == units)
                def _(units=units):
                    ffn(0, units * MOE_UNIT)

            @pl.when(jnp.logical_not(whole))
            def _():
                def full_tile(j, c2):
                    ffn(pl.multiple_of(j * MOE_TM, MOE_TM), MOE_TM)
                    return c2
                lax.fori_loop(0, n_units // MOE_TM_UNITS, full_tile, 0)
                part = MOE_TM_UNITS // 2
                while part >= 1:
                    @pl.when((n_units // part) % 2 == 1)
                    def _(part=part):
                        done = n_units // (2 * part) * (2 * part)
                        ffn(pl.multiple_of(done * MOE_UNIT, MOE_UNIT), part * MOE_UNIT)
                    part //= 2
            return c
        lax.fori_loop(0, n_f, chunk, 0)

        @pl.when(g > 0)
        def _():
            for_units(g - 1, lambda grp, j: copy_out(grp, j).wait())

        def finish(grp, j):
            ypk[unit_rows(j), :] = acc[unit_rows(j), :].astype(BF16)
            copy_out(grp, j).start()
        for_units(g, finish)

    @pl.when(g == pl.num_programs(0) - 1)
    def _():
        for_units(n_groups - 1, lambda grp, j: copy_out(grp, j).wait())


def _moe(ge, grow, gnu, ng, xs, wg, wu, wd, bg, bu, bd, n_groups_max):
    n_f = D_FF // MOE_TF
    assert n_f % 2 == 0
    per_expert = lambda shape: pl.BlockSpec((None,) + shape, lambda g, ge, gr, gn, ng: (ge[g], 0, 0))
    any_spec = pl.BlockSpec(memory_space=pl.ANY)
    grid_spec = pltpu.PrefetchScalarGridSpec(
        num_scalar_prefetch=4,
        grid=(n_groups_max,),
        in_specs=[any_spec, any_spec, any_spec, any_spec,
                  per_expert((n_f, MOE_TF)), per_expert((n_f, MOE_TF)), per_expert((1, D_MODEL))],
        out_specs=any_spec,
        scratch_shapes=[
            pltpu.VMEM((2, MOE_GROUP_ROWS, D_MODEL), BF16),
            pltpu.VMEM((MOE_GROUP_ROWS, D_MODEL), F32),
            pltpu.VMEM((MOE_GROUP_ROWS, D_MODEL), BF16),
            pltpu.VMEM((2, D_MODEL, MOE_TF), F32),
            pltpu.VMEM((2, D_MODEL, MOE_TF), F32),
            pltpu.VMEM((2, MOE_TF, D_MODEL), F32),
            pltpu.SemaphoreType.DMA((2,)),
            pltpu.SemaphoreType.DMA((2, 3)),
            pltpu.SemaphoreType.DMA((1,)),
        ],
    )
    return pl.pallas_call(
        functools.partial(_moe_kernel, n_f=n_f),
        grid_spec=grid_spec,
        out_shape=jax.ShapeDtypeStruct(xs.shape, BF16),
        compiler_params=_params(56, ("arbitrary",)),
        name="moe_ffn",
    )(ge, grow, gnu, ng, xs, wg, wu, wd,
      bg.reshape(N_EXPERTS, n_f, MOE_TF), bu.reshape(N_EXPERTS, n_f, MOE_TF), bd.reshape(N_EXPERTS, 1, D_MODEL))


def _combine_kernel(seg_loc_ref, seg_src_ref, seg_len_ref, tile_rows_ref,
                    ys_hbm, h1_ref, dest_ref, w_ref, fn_ref, y_ref, sb, wbuf, sem, *, tile0, kt):
    i = pl.program_id(0)
    n_steps = pl.num_programs(0)
    slot = i % 2
    tile = tile0 + i
    tt = h1_ref.shape[0]

    def seg_copy(sl, s, off, rows):
        return pltpu.make_async_copy(_seg_rows(ys_hbm, seg_src_ref[s] + off, rows),
                                     _seg_rows(sb.at[sl], seg_loc_ref[s] + off, rows), sem.at[sl])

    def start_segments(tl, sl):
        _for_segments(tl, seg_len_ref, tt, lambda s, off, rows: seg_copy(sl, s, off, rows).start())

    @pl.when(i == 0)
    def _():
        sb[...] = jnp.zeros(sb.shape, BF16)
        start_segments(tile, slot)

    @pl.when(i + 1 < n_steps)
    def _():
        start_segments(tile + 1, 1 - slot)

    _wait_rows(tile_rows_ref[tile], kt, lambda rows: pltpu.make_async_copy(
        ys_hbm.at[pl.ds(0, rows)], sb.at[slot, pl.ds(0, rows)], sem.at[slot]))

    acc = h1_ref[...]
    d = [jnp.broadcast_to(dest_ref[:, k:k + 1], (tt, PERM_CHUNK)) for k in range(TOP_K)]
    w = [jnp.broadcast_to(w_ref[:, k:k + 1], (tt, PERM_CHUNK)) for k in range(TOP_K)]
    half = kt // 2
    for h in range(2):
        for c0 in range(0, half, PERM_CHUNK):
            col = h * half + c0 + lax.broadcasted_iota(I32, (tt, PERM_CHUNK), 1)
            wm = jnp.zeros((tt, PERM_CHUNK), F32)
            for k in range(TOP_K):
                wm = jnp.where(d[k] == col, w[k], wm)
            wbuf[h, :, c0:c0 + PERM_CHUNK] = wm.astype(BF16)
        acc = acc + jnp.dot(wbuf[h], sb[slot, h * half:(h + 1) * half, :], preferred_element_type=F32)
    y_ref[...] = _rmsnorm(acc, fn_ref[...])


def _combine(tables, ys, h1, dest_t, wgt_t, fn, tile0, tt, kt):
    n = h1.shape[0]
    grid_spec = pltpu.PrefetchScalarGridSpec(
        num_scalar_prefetch=len(tables),
        grid=(n // tt,),
        in_specs=[pl.BlockSpec(memory_space=pl.ANY),
                  pl.BlockSpec((tt, D_MODEL), lambda i, *_: (i, 0)),
                  pl.BlockSpec((tt, TOP_K), lambda i, *_: (i, 0)),
                  pl.BlockSpec((tt, TOP_K), lambda i, *_: (i, 0)),
                  pl.BlockSpec((1, D_MODEL), lambda i, *_: (0, 0))],
        out_specs=pl.BlockSpec((tt, D_MODEL), lambda i, *_: (i, 0)),
        scratch_shapes=[pltpu.VMEM((2, kt, D_MODEL), BF16), pltpu.VMEM((2, tt, kt // 2), BF16),
                        pltpu.SemaphoreType.DMA((2,))],
    )
    return pl.pallas_call(
        functools.partial(_combine_kernel, tile0=tile0, kt=kt),
        grid_spec=grid_spec,
        out_shape=jax.ShapeDtypeStruct((n, D_MODEL), F32),
        compiler_params=_params(56, ("arbitrary",)),
        name="combine",
    )(*tables, ys, h1, dest_t, wgt_t, fn)


def _ceil_to(x, m):
    return (x + m - 1) // m * m


def _routing_tables(eid, tile_lens, n_groups_max):
    n = eid.shape[1]
    nt = len(tile_lens)
    starts = np.concatenate([[0], np.cumsum(tile_lens)[:-1]]).astype(np.int64)
    onehot = (eid[:, :, None] == jnp.arange(N_EXPERTS, dtype=I32)[None, None, :]).astype(I32)
    member = jnp.sum(onehot, axis=0)
    incl = jnp.cumsum(member, axis=0)
    before = incl - member
    rank0 = jnp.stack([before[int(s)] for s in starts])
    cnt = jnp.stack([incl[int(s) + int(l) - 1] for s, l in zip(starts, tile_lens)]) - rank0
    c8 = _ceil_to(cnt, BF16_ROWS)
    seg_loc = jnp.cumsum(c8, axis=1) - c8
    rows_e = jnp.sum(c8, axis=0)
    region_e = _ceil_to(rows_e, MOE_UNIT)
    start_e = jnp.cumsum(region_e) - region_e
    seg_pos = start_e[None, :] + jnp.cumsum(c8, axis=0) - c8

    def per_token(tab):
        return jnp.concatenate([jnp.broadcast_to(tab[t], (int(l), N_EXPERTS)) for t, l in enumerate(tile_lens)])
    base = per_token(seg_loc - rank0) + before
    dest = jnp.sum(onehot * base[None], axis=2).astype(I32)

    ntiles_e = region_e // MOE_UNIT
    groups_e = (ntiles_e + MOE_GROUP_UNITS - 1) // MOE_GROUP_UNITS
    gend_e = jnp.cumsum(groups_e)
    gstart_e = gend_e - groups_e
    n_groups = gend_e[-1]
    j = jnp.arange(n_groups_max, dtype=I32)
    j_act = jnp.minimum(j, n_groups - 1)
    e_j = jnp.minimum(jnp.sum((gend_e[None, :] <= j_act[:, None]).astype(I32), axis=1), N_EXPERTS - 1)
    sel = (e_j[:, None] == jnp.arange(N_EXPERTS, dtype=I32)[None, :]).astype(I32)
    pick = lambda v: jnp.sum(sel * v[None, :], axis=1)
    local = j_act - pick(gstart_e)
    grow = pick(start_e) + local * MOE_GROUP_ROWS
    gnt = jnp.where(j < n_groups, jnp.clip(pick(ntiles_e) - local * MOE_GROUP_UNITS, 0, MOE_GROUP_UNITS), 0)

    flat = lambda a: a.reshape(-1).astype(I32)
    seg_tabs = (flat(seg_loc), flat(seg_pos), flat(c8), jnp.sum(c8, axis=1).astype(I32))
    tail_tabs = ((start_e + rows_e).astype(I32), (region_e - rows_e).astype(I32))
    group_tabs = (e_j.astype(I32), grow.astype(I32), gnt.astype(I32), n_groups.reshape(1).astype(I32))
    return dest, seg_tabs, tail_tabs, group_tabs


def _block_diag(w, per_block):
    h, d, _ = w.shape
    nb = h // per_block
    eye = jnp.eye(per_block, dtype=w.dtype)
    w4 = w.reshape(nb, per_block, d, d)
    out = jnp.einsum("bpij,pq->bpiqj", w4, eye)
    return out.reshape(nb, per_block * d, per_block * d)


def kernel(x_prompt, x_sample, state_conv, state_rglru, state_pool, meta_tokens, norm1, w_in, conv_w, conv_b, lru_wa, lru_ba, lru_wx, lru_bx, lru_lambda, pool_w, pool_scale, w_out, norm2, router_w, router_b, exp_wg, exp_bg, exp_wu, exp_bu, exp_wd, exp_bd, final_norm):
    batch, seq, _ = x_prompt.shape
    dec = x_sample.shape[0]
    n_prompt = batch * seq
    n_tok = n_prompt + dec
    l = 0
    row = lambda v: v.reshape(1, -1)

    xp = x_prompt.reshape(n_prompt, D_MODEL)
    xs_tok = x_sample.reshape(dec, D_MODEL)
    w_in_b = w_in[l].astype(BF16)
    w_out_b = w_out[l].astype(BF16)
    heads_per_block = V7X_MXU_DIM // LRU_HEAD_DIM
    mixw = (conv_w[l], row(conv_b[l]),
            _block_diag(lru_wa[l], heads_per_block).astype(BF16), row(lru_ba[l]),
            _block_diag(lru_wx[l], heads_per_block).astype(BF16), row(lru_bx[l]),
            row(lru_lambda[l]), pool_w[l].astype(BF16), row(pool_scale[l]))
    g1 = row(norm1[l])

    zeros = lambda r: jnp.zeros((1, r, LRU_WIDTH), F32)
    _, h_meta, conv_meta, pool_meta = _mix_seq(meta_tokens, g1, w_in_b, 1, N_META, N_META, 0,
                                               zeros(SUBLANES), zeros(SUBLANES), zeros(2 * SUBLANES), mixw)
    per_seq = lambda v: jnp.broadcast_to(v, (batch,) + v.shape[1:])
    yab_p, h_p, conv_tail, pool_tail = _mix_seq(xp, g1, w_in_b, batch, seq, TIME_TILE, N_META,
                                                per_seq(conv_meta), per_seq(h_meta), per_seq(pool_meta), mixw)
    proj_s = _proj(xs_tok, g1, w_in_b, dec)
    yab_s, h_s = _mix_step(proj_s, jnp.swapaxes(state_conv[l], 0, 1), state_rglru[l],
                           jnp.swapaxes(state_pool[l], 0, 1), mixw)

    out_w = (w_out_b, row(norm2[l]), router_w[l], router_b[l].reshape(N_EXPERTS, 1))
    h1_p, xn_p, eid_p, wgt_p = _out(yab_p, xp, *out_w, PROMPT_TILE)
    h1_s, xn_s, eid_s, wgt_s = _out(yab_s, xs_tok, *out_w, dec)

    n_ptiles = n_prompt // PROMPT_TILE
    tile_lens = [PROMPT_TILE] * n_ptiles + [dec]
    seg_pad = (BF16_ROWS - 1) * N_EXPERTS
    kt_p = _ceil_to(TOP_K * PROMPT_TILE + seg_pad, PERM_CHUNK)
    kt_s = _ceil_to(TOP_K * dec + seg_pad, PERM_CHUNK)
    units_max = (TOP_K * n_tok + len(tile_lens) * seg_pad) // MOE_UNIT + N_EXPERTS
    xs_rows = units_max * MOE_UNIT
    n_groups_max = (units_max + N_EXPERTS * (MOE_GROUP_UNITS - 1)) // MOE_GROUP_UNITS

    eid = jnp.concatenate([eid_p, eid_s], axis=1)
    dest, seg_tabs, tail_tabs, group_tabs = _routing_tables(eid, tile_lens, n_groups_max)
    disp_tabs = seg_tabs + tail_tabs
    dest_p, dest_s = dest[:, :n_prompt], dest[:, n_prompt:]

    xs = _dispatch(disp_tabs, xn_p, dest_p, None, 0, PROMPT_TILE, kt_p, xs_rows, False)
    xs = _dispatch(disp_tabs, xn_s, dest_s, xs, n_ptiles, dec, kt_s, xs_rows, True)
    ys = _moe(*group_tabs, xs, exp_wg[l], exp_wu[l], exp_wd[l], exp_bg[l], exp_bu[l], exp_bd[l], n_groups_max)
    fn = row(final_norm)
    y_p = _combine(seg_tabs, ys, h1_p, dest_p.T, wgt_p.T, fn, 0, PROMPT_TILE, kt_p)
    y_s = _combine(seg_tabs, ys, h1_s, dest_s.T, wgt_s.T, fn, n_ptiles, dec, kt_s)

    conv_p = conv_tail[:, SUBLANES - (CONV_WIDTH - 1):]
    pool_p = pool_tail[:, 2 * SUBLANES - POOL_BUF:]
    conv_s = jnp.concatenate([state_conv[l][:, 1:], proj_s[:, None, :LRU_WIDTH]], axis=1)
    pool_s = jnp.concatenate([state_pool[l][:, 1:], proj_s[:, None, 2 * LRU_WIDTH:]], axis=1)
    return (y_p.reshape(batch, seq, D_MODEL), y_s.reshape(dec, 1, D_MODEL),
            conv_p[None], h_p[None, :, SUBLANES - 1], pool_p[None],
            conv_s[None], h_s[None], pool_s[None])
```

```python
import functools

import numpy as np
import jax
import jax.numpy as jnp
from jax import lax
from jax.experimental import pallas as pl
from jax.experimental.pallas import tpu as pltpu

F32 = jnp.float32
BF16 = jnp.bfloat16
I32 = jnp.int32
U32 = jnp.uint32

D_MODEL = 2048
N_META = 16
LRU_WIDTH = 1024
LRU_HEADS = 16
LRU_HEAD_DIM = LRU_WIDTH // LRU_HEADS
CONV_WIDTH = 4
LRU_C = 8.0
POOL_WIDTH = D_MODEL - LRU_WIDTH
POOL_WINDOWS = (2, 4, 8, 16)
POOL_GROUP_DIM = POOL_WIDTH // len(POOL_WINDOWS)
POOL_BUF = max(POOL_WINDOWS) - 1
IN_WIDTH = 2 * LRU_WIDTH + POOL_WIDTH
N_EXPERTS = 32
TOP_K = 4
D_FF = D_MODEL
SWIGLU_ALPHA = 1.702
SWIGLU_LIMIT = 7.0
RMS_EPS = 1e-6

V7X_MXU_DIM = 256
SUBLANES = 8
BF16_ROWS = 16

PROMPT_TILE = 512
TIME_TILE = 256
MOE_UNIT = 128
MOE_TM_UNITS = 8
MOE_TM = MOE_TM_UNITS * MOE_UNIT
MOE_GROUP_UNITS = 11
MOE_GROUP_ROWS = MOE_UNIT * MOE_GROUP_UNITS
MOE_TILE_STEP = 64
MOE_WHOLE_GROUP_ROWS = (1152, 1216, 1280)
MOE_TF = V7X_MXU_DIM
PERM_CHUNK = V7X_MXU_DIM
ROUTE_CHUNK = 512


def _params(vmem_mb, sem):
    return pltpu.CompilerParams(dimension_semantics=sem, vmem_limit_bytes=vmem_mb << 20)


def _rmsnorm(x, g):
    return x * lax.rsqrt(jnp.mean(x * x, axis=-1, keepdims=True) + RMS_EPS) * g


def _resident(shape):
    return pl.BlockSpec(shape, lambda *_: (0,) * len(shape), pipeline_mode=pl.Buffered(1))


def _proj_kernel(x_ref, g_ref, w_ref, o_ref):
    xn = _rmsnorm(x_ref[...], g_ref[...])
    o_ref[...] = jnp.dot(xn.astype(BF16), w_ref[...], preferred_element_type=F32)


def _proj(x, g, w_bf16, tile):
    rows = x.shape[0]
    return pl.pallas_call(
        _proj_kernel,
        grid=(rows // tile,),
        in_specs=[pl.BlockSpec((tile, D_MODEL), lambda i: (i, 0)),
                  _resident((1, D_MODEL)), _resident((D_MODEL, IN_WIDTH))],
        out_specs=pl.BlockSpec((tile, IN_WIDTH), lambda i: (i, 0)),
        out_shape=jax.ShapeDtypeStruct((rows, IN_WIDTH), F32),
        compiler_params=_params(48, ("arbitrary",)),
        name="proj",
    )(x, g, w_bf16)


def _lru_coeffs(xc, wa_ref, ba_ref, wx_ref, bx_ref, lam_ref):
    xcb = xc.astype(BF16)
    nq = LRU_WIDTH // V7X_MXU_DIM
    rs, gs = [], []
    for q in range(nq):
        blk = xcb[:, q * V7X_MXU_DIM:(q + 1) * V7X_MXU_DIM]
        rs.append(jnp.dot(blk, wa_ref[q], preferred_element_type=F32))
        gs.append(jnp.dot(blk, wx_ref[q], preferred_element_type=F32))
    r = jax.nn.sigmoid(jnp.concatenate(rs, axis=-1) + ba_ref[...])
    i = jax.nn.sigmoid(jnp.concatenate(gs, axis=-1) + bx_ref[...])
    log_a = -LRU_C * r * jax.nn.softplus(-lam_ref[...])
    a = jnp.exp(log_a)
    th = jnp.tanh(log_a)
    beta = jnp.sqrt(-2.0 * th / (1.0 - th))
    return a, beta * i * xc


def _pool_project(wins, u, inv_cnt, pw_ref, ps_ref):
    outs = []
    for g in range(len(POOL_WINDOWS)):
        sl = slice(g * POOL_GROUP_DIM, (g + 1) * POOL_GROUP_DIM)
        d = (wins[g] * inv_cnt[g] - u[:, sl]).astype(BF16)
        outs.append(jnp.dot(d, pw_ref[g], preferred_element_type=F32))
    return jnp.concatenate(outs, axis=-1) * ps_ref[...]


def _mix_seq_kernel(x_ref, g1_ref, win_ref, conv0_ref, h0_ref, pool0_ref,
                    cw_ref, cb_ref, wa_ref, ba_ref, wx_ref, bx_ref, lam_ref, pw_ref, ps_ref,
                    yab_ref, ht_ref, ctail_ref, ptail_ref, cext, pext, hcar, *, tt, start):
    t = pl.program_id(1)
    hist = 2 * SUBLANES

    @pl.when(t == 0)
    def _():
        cext[0:SUBLANES, :] = conv0_ref[...]
        pext[0:hist, :] = pool0_ref[...]
        hcar[...] = h0_ref[...]

    xnb = _rmsnorm(x_ref[...], g1_ref[...]).astype(BF16)
    xa = jnp.dot(xnb, win_ref[:, :LRU_WIDTH], preferred_element_type=F32)
    ga = jnp.dot(xnb, win_ref[:, LRU_WIDTH:2 * LRU_WIDTH], preferred_element_type=F32)
    u = jnp.dot(xnb, win_ref[:, 2 * LRU_WIDTH:], preferred_element_type=F32)

    cext[SUBLANES:SUBLANES + tt, :] = xa
    cw = cw_ref[...]
    xc = cb_ref[...] + cw[CONV_WIDTH - 1:CONV_WIDTH] * xa
    for k in range(1, CONV_WIDTH):
        xc = xc + cw[CONV_WIDTH - 1 - k:CONV_WIDTH - k] * cext[SUBLANES - k:SUBLANES - k + tt, :]
    cext[0:SUBLANES, :] = cext[tt:tt + SUBLANES, :]
    ctail_ref[...] = cext[0:SUBLANES, :]

    a, b = _lru_coeffs(xc, wa_ref, ba_ref, wx_ref, bx_ref, lam_ref)

    groups = tt // SUBLANES
    a = a.reshape(groups, SUBLANES, LRU_WIDTH)
    b = b.reshape(groups, SUBLANES, LRU_WIDTH)
    sub = lax.broadcasted_iota(I32, (groups, SUBLANES, LRU_WIDTH), 1)
    s = 1
    while s < SUBLANES:
        keep = sub >= s
        a_prev = jnp.where(keep, pltpu.roll(a, s, 1), 1.0)
        b_prev = jnp.where(keep, pltpu.roll(b, s, 1), 0.0)
        b = b + a * b_prev
        a = a * a_prev
        s *= 2
    h = hcar[SUBLANES - 1:SUBLANES, :]
    hrows = []
    for g in range(groups):
        hg = a[g] * h + b[g]
        hrows.append(hg)
        h = hg[SUBLANES - 1:SUBLANES, :]
    hs = jnp.concatenate(hrows, axis=0)
    hcar[...] = hrows[-1]
    ht_ref[...] = hrows[-1]
    ya = hs * jax.nn.gelu(ga)

    pext[hist:hist + tt, :] = u
    e = pext[...]
    wins = []
    shift = 1
    for g in range(len(POOL_WINDOWS)):
        e = e + pltpu.roll(e, shift, 0)
        wins.append(e[hist:hist + tt, :POOL_GROUP_DIM])
        if g + 1 < len(POOL_WINDOWS):
            e = e[:, POOL_GROUP_DIM:]
        shift *= 2
    pext[0:hist, :] = pext[tt:tt + hist, :]
    ptail_ref[...] = pext[0:hist, :]

    if start >= POOL_BUF:
        inv_cnt = [1.0 / w for w in POOL_WINDOWS]
    else:
        pos = start + t * tt + lax.broadcasted_iota(I32, (tt, 1), 0)
        inv_cnt = [1.0 / jnp.minimum(w, pos + 1).astype(F32) for w in POOL_WINDOWS]
    yb = _pool_project(wins, u, inv_cnt, pw_ref, ps_ref)

    yab_ref[:, :LRU_WIDTH] = ya.astype(BF16)
    yab_ref[:, LRU_WIDTH:] = yb.astype(BF16)


def _mix_seq(x, g1, w_in_bf16, batch, seq, tt, start, conv0, h0, pool0, mixw):
    nt = seq // tt
    hist = 2 * SUBLANES
    rows = lambda w: pl.BlockSpec((tt, w), lambda b, t: (b * nt + t, 0))
    state = lambda r: pl.BlockSpec((None, r, LRU_WIDTH), lambda b, t: (b, 0, 0))
    tail = lambda r: jax.ShapeDtypeStruct((batch, r, LRU_WIDTH), F32)
    return pl.pallas_call(
        functools.partial(_mix_seq_kernel, tt=tt, start=start),
        grid=(batch, nt),
        in_specs=[rows(D_MODEL), _resident((1, D_MODEL)), _resident((D_MODEL, IN_WIDTH)),
                  state(SUBLANES), state(SUBLANES), state(hist)] + [_resident(w.shape) for w in mixw],
        out_specs=[rows(D_MODEL), state(SUBLANES), state(SUBLANES), state(hist)],
        out_shape=[jax.ShapeDtypeStruct((batch * seq, D_MODEL), BF16), tail(SUBLANES), tail(SUBLANES), tail(hist)],
        scratch_shapes=[pltpu.VMEM((tt + SUBLANES, LRU_WIDTH), F32),
                        pltpu.VMEM((tt + hist, POOL_WIDTH), F32),
                        pltpu.VMEM((SUBLANES, LRU_WIDTH), F32)],
        compiler_params=_params(56, ("arbitrary", "arbitrary")),
        name="mix_seq",
    )(x, g1, w_in_bf16, conv0, h0, pool0, *mixw)


def _mix_step_kernel(xa_ref, ga_ref, ub_ref, sconv_ref, sh_ref, spool_ref,
                     cw_ref, cb_ref, wa_ref, ba_ref, wx_ref, bx_ref, lam_ref, pw_ref, ps_ref,
                     yab_ref, h_ref):
    xa = xa_ref[...]
    cw = cw_ref[...]
    xc = cb_ref[...] + cw[CONV_WIDTH - 1:CONV_WIDTH] * xa
    for k in range(1, CONV_WIDTH):
        xc = xc + cw[CONV_WIDTH - 1 - k:CONV_WIDTH - k] * sconv_ref[CONV_WIDTH - 1 - k]
    a, b = _lru_coeffs(xc, wa_ref, ba_ref, wx_ref, bx_ref, lam_ref)
    h = a * sh_ref[...] + b
    h_ref[...] = h
    ya = h * jax.nn.gelu(ga_ref[...])

    u = ub_ref[...]
    wins = []
    for g, w in enumerate(POOL_WINDOWS):
        sl = slice(g * POOL_GROUP_DIM, (g + 1) * POOL_GROUP_DIM)
        acc = u[:, sl]
        for k in range(1, w):
            acc = acc + spool_ref[POOL_BUF - k, :, sl]
        wins.append(acc)
    yb = _pool_project(wins, u, [1.0 / w for w in POOL_WINDOWS], pw_ref, ps_ref)
    yab_ref[:, :LRU_WIDTH] = ya.astype(BF16)
    yab_ref[:, LRU_WIDTH:] = yb.astype(BF16)


def _mix_step(proj, sconv_t, sh, spool_t, mixw):
    rows = proj.shape[0]
    col = lambda c: pl.BlockSpec((rows, LRU_WIDTH), lambda i: (0, c))
    full = lambda a: pl.BlockSpec(a.shape, lambda i: (0,) * a.ndim)
    return pl.pallas_call(
        _mix_step_kernel,
        grid=(1,),
        in_specs=[col(0), col(1), col(2), full(sconv_t), full(sh), full(spool_t)] + [full(w) for w in mixw],
        out_specs=[pl.BlockSpec((rows, D_MODEL), lambda i: (0, 0)),
                   pl.BlockSpec((rows, LRU_WIDTH), lambda i: (0, 0))],
        out_shape=[jax.ShapeDtypeStruct((rows, D_MODEL), BF16),
                   jax.ShapeDtypeStruct((rows, LRU_WIDTH), F32)],
        compiler_params=_params(48, ("arbitrary",)),
        name="mix_step",
    )(proj, proj, proj, sconv_t, sh, spool_t, *mixw)


def _out_kernel(yab_ref, x_ref, wo_ref, g2_ref, rw_both_ref, rw_hi_ref, rb_ref,
                h1_ref, xn_ref, eid_ref, wgt_ref, *, tile):
    h1_ref[...] = x_ref[...] + jnp.dot(yab_ref[...], wo_ref[...], preferred_element_type=F32)

    chunk = min(ROUTE_CHUNK, tile)

    def route(c, carry):
        rows = pl.ds(pl.multiple_of(c * chunk, chunk), chunk)
        xn = _rmsnorm(h1_ref[rows, :], g2_ref[...])
        xn_hi = xn.astype(BF16)
        xn_ref[rows, :] = xn_hi
        xn_lo = (xn - xn_hi.astype(F32)).astype(BF16)
        nt_dims = (((1,), (1,)), ((), ()))
        both = lax.dot_general(rw_both_ref[...], xn_hi, nt_dims, preferred_element_type=F32)
        logits = (both[:N_EXPERTS] + both[N_EXPERTS:]
                  + lax.dot_general(rw_hi_ref[...], xn_lo, nt_dims, preferred_element_type=F32)) + rb_ref[...]
        eidx = lax.broadcasted_iota(I32, (N_EXPERTS, chunk), 0)
        vals, ids = [], []
        for _ in range(TOP_K):
            m = jnp.max(logits, axis=0, keepdims=True)
            idx = jnp.min(jnp.where(logits == m, eidx, N_EXPERTS), axis=0, keepdims=True)
            vals.append(m)
            ids.append(idx)
            logits = jnp.where(eidx == idx, -jnp.inf, logits)
        ex = [jnp.exp(v - vals[0]) for v in vals]
        tot = ex[0]
        for e_ in ex[1:]:
            tot = tot + e_
        eid_ref[:, rows] = jnp.concatenate(ids, axis=0)
        wgt_ref[:, rows] = jnp.concatenate([e_ / tot for e_ in ex], axis=0)
        return carry
    lax.fori_loop(0, tile // chunk, route, 0)


def _out(yab, x, wo_bf16, g2, rw, rb, tile):
    rw_hi = rw.T.astype(BF16)
    rw_lo = (rw.T - rw_hi.astype(F32)).astype(BF16)
    rw_both = jnp.concatenate([rw_hi, rw_lo], axis=0)
    n = x.shape[0]
    rowblk = lambda w: pl.BlockSpec((tile, w), lambda i: (i, 0))
    colblk = pl.BlockSpec((TOP_K, tile), lambda i: (0, i))
    return pl.pallas_call(
        functools.partial(_out_kernel, tile=tile),
        grid=(n // tile,),
        in_specs=[rowblk(D_MODEL), rowblk(D_MODEL), _resident((D_MODEL, D_MODEL)), _resident((1, D_MODEL)),
                  _resident((2 * N_EXPERTS, D_MODEL)), _resident((N_EXPERTS, D_MODEL)), _resident((N_EXPERTS, 1))],
        out_specs=[rowblk(D_MODEL), rowblk(D_MODEL), colblk, colblk],
        out_shape=[jax.ShapeDtypeStruct((n, D_MODEL), F32), jax.ShapeDtypeStruct((n, D_MODEL), BF16),
                   jax.ShapeDtypeStruct((TOP_K, n), I32), jax.ShapeDtypeStruct((TOP_K, n), F32)],
        compiler_params=_params(48, ("arbitrary",)),
        name="out_router",
    )(yab, x, wo_bf16, g2, rw_both, rw_hi, rb)


def _for_segments(tile, seg_len_ref, max_len, fn):
    def per_expert(e, c):
        s = tile * N_EXPERTS + e
        length = seg_len_ref[s]
        size = max_len
        while size >= BF16_ROWS:
            @pl.when((length // size) % 2 == 1)
            def _(size=size):
                fn(s, length // (2 * size) * (2 * size), size)
            size //= 2
        return c
    lax.fori_loop(0, N_EXPERTS, per_expert, 0)


def _seg_rows(ref, start, rows):
    return ref.at[pl.ds(pl.multiple_of(start, BF16_ROWS), rows)]


def _wait_rows(total, max_rows, copy_of_rows):
    size = 1 << (max_rows.bit_length() - 1)
    while size >= BF16_ROWS:
        @pl.when((total // size) % 2 == 1)
        def _(size=size):
            copy_of_rows(size).wait()
        size //= 2


def _dispatch_kernel(seg_loc_ref, seg_dst_ref, seg_len_ref, tile_rows_ref, tail_dst_ref, tail_len_ref,
                     xn_ref, dest_ref, *rest, tile0, kt, zero_tails):
    xs_hbm, cb, zb, sem, zsem = rest[-5:]
    i = pl.program_id(0)
    n_steps = pl.num_programs(0)
    slot = i % 2
    tile = tile0 + i
    tt = xn_ref.shape[0]

    def seg_copy(sl, s, off, rows):
        return pltpu.make_async_copy(_seg_rows(cb.at[sl], seg_loc_ref[s] + off, rows),
                                     _seg_rows(xs_hbm, seg_dst_ref[s] + off, rows), sem.at[sl])

    def start_segments(tl, sl):
        _for_segments(tl, seg_len_ref, tt, lambda s, off, rows: seg_copy(sl, s, off, rows).start())

    def wait_segments(tl, sl):
        _wait_rows(tile_rows_ref[tl], kt, lambda rows: pltpu.make_async_copy(
            cb.at[sl, pl.ds(0, rows)], xs_hbm.at[pl.ds(0, rows)], sem.at[sl]))

    @pl.when(i >= 2)
    def _():
        wait_segments(tile - 2, slot)

    x = xn_ref[...]
    d = [dest_ref[k:k + 1, :] for k in range(TOP_K)]
    for c0 in range(0, kt, PERM_CHUNK):
        r = c0 + lax.broadcasted_iota(I32, (PERM_CHUNK, tt), 0)
        p = jnp.zeros((PERM_CHUNK, tt), F32)
        for k in range(TOP_K):
            p = jnp.where(d[k] == r, 1.0, p)
        rows = jnp.dot(p.astype(BF16), x, preferred_element_type=F32)
        cb[slot, c0:c0 + PERM_CHUNK, :] = rows.astype(BF16)

    start_segments(tile, slot)

    @pl.when(i == n_steps - 1)
    def _():
        @pl.when(i >= 1)
        def _():
            wait_segments(tile - 1, 1 - slot)
        wait_segments(tile, slot)

    if zero_tails:
        zb[...] = jnp.zeros(zb.shape, BF16)

        def tail_copy(e, j):
            return pltpu.make_async_copy(zb, _seg_rows(xs_hbm, tail_dst_ref[e] + j * BF16_ROWS, BF16_ROWS),
                                         zsem.at[0])

        def tails(method):
            def per_expert(e, c):
                def per_chunk(j, c2):
                    getattr(tail_copy(e, j), method)()
                    return c2
                lax.fori_loop(0, tail_len_ref[e] // BF16_ROWS, per_chunk, 0)
                return c
            lax.fori_loop(0, N_EXPERTS, per_expert, 0)

        @pl.when(i == n_steps - 1)
        def _():
            tails("start")
            tails("wait")


def _dispatch(tables, xn, dest, xs, tile0, tt, kt, xs_rows, zero_tails):
    n = xn.shape[0]
    in_specs = [pl.BlockSpec((tt, D_MODEL), lambda i, *_: (i, 0)),
                pl.BlockSpec((TOP_K, tt), lambda i, *_: (0, i))]
    args = [*tables, xn, dest]
    aliases = {}
    if xs is not None:
        in_specs.append(pl.BlockSpec(memory_space=pl.ANY))
        aliases = {len(args): 0}
        args.append(xs)
    grid_spec = pltpu.PrefetchScalarGridSpec(
        num_scalar_prefetch=len(tables),
        grid=(n // tt,),
        in_specs=in_specs,
        out_specs=pl.BlockSpec(memory_space=pl.ANY),
        scratch_shapes=[pltpu.VMEM((2, kt, D_MODEL), BF16), pltpu.VMEM((BF16_ROWS, D_MODEL), BF16),
                        pltpu.SemaphoreType.DMA((2,)), pltpu.SemaphoreType.DMA((1,))],
    )
    return pl.pallas_call(
        functools.partial(_dispatch_kernel, tile0=tile0, kt=kt, zero_tails=zero_tails),
        grid_spec=grid_spec,
        out_shape=jax.ShapeDtypeStruct((xs_rows, D_MODEL), BF16),
        input_output_aliases=aliases,
        compiler_params=_params(48, ("arbitrary",)),
        name="dispatch",
    )(*args)


def _moe_kernel(ge_ref, grow_ref, gnu_ref, gtile_ref, ng_ref,
                xs_hbm, wg_hbm, wu_hbm, wd_hbm, bg_ref, bu_ref, bd_ref,
                ys_hbm, xb, acc, ypk, wgf, wuf, wdf, xsem, wsem, osem, *, n_f):
    g = pl.program_id(0)
    n_groups = ng_ref[0]
    xslot = g % 2

    def unit_rows(j):
        return pl.ds(pl.multiple_of(j * MOE_UNIT, MOE_UNIT), MOE_UNIT)

    def hbm_unit(ref, grp, j):
        return ref.at[pl.ds(pl.multiple_of(grow_ref[grp] + j * MOE_UNIT, MOE_UNIT), MOE_UNIT)]

    def copy_in(grp, sl, j):
        return pltpu.make_async_copy(hbm_unit(xs_hbm, grp, j), xb.at[sl, unit_rows(j)], xsem.at[sl])

    def copy_out(grp, j):
        return pltpu.make_async_copy(ypk.at[unit_rows(j)], hbm_unit(ys_hbm, grp, j), osem.at[0])

    def for_units(grp, fn):
        def body(j, c):
            fn(grp, j)
            return c
        lax.fori_loop(0, gnu_ref[grp], body, 0)

    def weight_copies(grp, f, ws):
        e = ge_ref[grp]
        cols = pl.ds(pl.multiple_of(f * MOE_TF, MOE_TF), MOE_TF)
        return (pltpu.make_async_copy(wg_hbm.at[e, :, cols], wgf.at[ws], wsem.at[ws, 0]),
                pltpu.make_async_copy(wu_hbm.at[e, :, cols], wuf.at[ws], wsem.at[ws, 1]),
                pltpu.make_async_copy(wd_hbm.at[e, cols, :], wdf.at[ws], wsem.at[ws, 2]))

    @pl.when(g < n_groups)
    def _():
        @pl.when(g == 0)
        def _():
            for_units(0, lambda grp, j: copy_in(grp, 0, j).start())
            for cp in weight_copies(0, 0, 0):
                cp.start()
        for_units(g, lambda grp, j: copy_in(grp, xslot, j).wait())

        @pl.when(g + 1 < n_groups)
        def _():
            for_units(g + 1, lambda grp, j: copy_in(grp, 1 - xslot, j).start())

        def init(grp, j):
            acc[unit_rows(j), :] = jnp.broadcast_to(bd_ref[...], (MOE_UNIT, D_MODEL))
        for_units(g, init)
        n_units = gnu_ref[g]
        tile_rows = gtile_ref[g]

        def chunk(f, c):
            ws = f % 2
            for cp in weight_copies(g, f, ws):
                cp.wait()

            @pl.when(f + 1 < n_f)
            def _():
                for cp in weight_copies(g, f + 1, 1 - ws):
                    cp.start()

            @pl.when(jnp.logical_and(f + 1 == n_f, g + 1 < n_groups))
            def _():
                for cp in weight_copies(g + 1, 0, 1 - ws):
                    cp.start()

            bg = bg_ref[pl.ds(f, 1), :]
            bu = bu_ref[pl.ds(f, 1), :]

            def ffn(start, size):
                x = xb[xslot, pl.ds(start, size), :]
                gg = jnp.dot(x, wgf[ws].astype(BF16), preferred_element_type=F32) + bg
                uu = jnp.dot(x, wuf[ws].astype(BF16), preferred_element_type=F32) + bu
                gg = jnp.minimum(gg, SWIGLU_LIMIT)
                uu = jnp.clip(uu, -SWIGLU_LIMIT, SWIGLU_LIMIT)
                hdn = gg * jax.nn.sigmoid(SWIGLU_ALPHA * gg) * (uu + 1.0)
                acc[pl.ds(start, size), :] += jnp.dot(hdn.astype(BF16), wdf[ws].astype(BF16),
                                                      preferred_element_type=F32)

            whole = jnp.bool_(False)
            for rows in MOE_WHOLE_GROUP_ROWS:
                whole = jnp.logical_or(whole, tile_rows == rows)

                @pl.when(tile_rows == rows)
                def _(rows=rows):
                    ffn(0, rows)

            @pl.when(jnp.logical_not(whole))
            def _():
                def full_tile(j, c2):
                    ffn(pl.multiple_of(j * MOE_TM, MOE_TM), MOE_TM)
                    return c2
                lax.fori_loop(0, n_units // MOE_TM_UNITS, full_tile, 0)
                part = MOE_TM_UNITS // 2
                while part >= 1:
                    @pl.when((n_units // part) % 2 == 1)
                    def _(part=part):
                        done = n_units // (2 * part) * (2 * part)
                        ffn(pl.multiple_of(done * MOE_UNIT, MOE_UNIT), part * MOE_UNIT)
                    part //= 2
            return c
        lax.fori_loop(0, n_f, chunk, 0)

        @pl.when(g > 0)
        def _():
            for_units(g - 1, lambda grp, j: copy_out(grp, j).wait())

        def finish(grp, j):
            ypk[unit_rows(j), :] = acc[unit_rows(j), :].astype(BF16)
            copy_out(grp, j).start()
        for_units(g, finish)

    @pl.when(g == pl.num_programs(0) - 1)
    def _():
        for_units(n_groups - 1, lambda grp, j: copy_out(grp, j).wait())


def _moe(ge, grow, gnu, gtile, ng, xs, wg, wu, wd, bg, bu, bd, n_groups_max):
    n_f = D_FF // MOE_TF
    assert n_f % 2 == 0
    per_expert = lambda shape: pl.BlockSpec((None,) + shape, lambda g, ge, *_: (ge[g], 0, 0))
    any_spec = pl.BlockSpec(memory_space=pl.ANY)
    grid_spec = pltpu.PrefetchScalarGridSpec(
        num_scalar_prefetch=5,
        grid=(n_groups_max,),
        in_specs=[any_spec, any_spec, any_spec, any_spec,
                  per_expert((n_f, MOE_TF)), per_expert((n_f, MOE_TF)), per_expert((1, D_MODEL))],
        out_specs=any_spec,
        scratch_shapes=[
            pltpu.VMEM((2, MOE_GROUP_ROWS, D_MODEL), BF16),
            pltpu.VMEM((MOE_GROUP_ROWS, D_MODEL), F32),
            pltpu.VMEM((MOE_GROUP_ROWS, D_MODEL), BF16),
            pltpu.VMEM((2, D_MODEL, MOE_TF), F32),
            pltpu.VMEM((2, D_MODEL, MOE_TF), F32),
            pltpu.VMEM((2, MOE_TF, D_MODEL), F32),
            pltpu.SemaphoreType.DMA((2,)),
            pltpu.SemaphoreType.DMA((2, 3)),
            pltpu.SemaphoreType.DMA((1,)),
        ],
    )
    return pl.pallas_call(
        functools.partial(_moe_kernel, n_f=n_f),
        grid_spec=grid_spec,
        out_shape=jax.ShapeDtypeStruct(xs.shape, BF16),
        compiler_params=_params(56, ("arbitrary",)),
        name="moe_ffn",
    )(ge, grow, gnu, gtile, ng, xs, wg, wu, wd,
      bg.reshape(N_EXPERTS, n_f, MOE_TF), bu.reshape(N_EXPERTS, n_f, MOE_TF), bd.reshape(N_EXPERTS, 1, D_MODEL))


def _combine_kernel(seg_loc_ref, seg_src_ref, seg_len_ref, tile_rows_ref,
                    ys_hbm, h1_ref, dest_ref, w_ref, fn_ref, y_ref, sb, wbuf, sem, *, tile0, kt):
    i = pl.program_id(0)
    n_steps = pl.num_programs(0)
    slot = i % 2
    tile = tile0 + i
    tt = h1_ref.shape[0]

    def seg_copy(sl, s, off, rows):
        return pltpu.make_async_copy(_seg_rows(ys_hbm, seg_src_ref[s] + off, rows),
                                     _seg_rows(sb.at[sl], seg_loc_ref[s] + off, rows), sem.at[sl])

    def start_segments(tl, sl):
        _for_segments(tl, seg_len_ref, tt, lambda s, off, rows: seg_copy(sl, s, off, rows).start())

    @pl.when(i == 0)
    def _():
        sb[...] = jnp.zeros(sb.shape, BF16)
        start_segments(tile, slot)

    @pl.when(i + 1 < n_steps)
    def _():
        start_segments(tile + 1, 1 - slot)

    _wait_rows(tile_rows_ref[tile], kt, lambda rows: pltpu.make_async_copy(
        ys_hbm.at[pl.ds(0, rows)], sb.at[slot, pl.ds(0, rows)], sem.at[slot]))

    acc = h1_ref[...]
    d = [jnp.broadcast_to(dest_ref[:, k:k + 1], (tt, PERM_CHUNK)) for k in range(TOP_K)]
    w = [jnp.broadcast_to(w_ref[:, k:k + 1], (tt, PERM_CHUNK)) for k in range(TOP_K)]
    half = kt // 2
    for h in range(2):
        for c0 in range(0, half, PERM_CHUNK):
            col = h * half + c0 + lax.broadcasted_iota(I32, (tt, PERM_CHUNK), 1)
            wm = jnp.zeros((tt, PERM_CHUNK), F32)
            for k in range(TOP_K):
                wm = jnp.where(d[k] == col, w[k], wm)
            wbuf[h, :, c0:c0 + PERM_CHUNK] = wm.astype(BF16)
        acc = acc + jnp.dot(wbuf[h], sb[slot, h * half:(h + 1) * half, :], preferred_element_type=F32)
    y_ref[...] = _rmsnorm(acc, fn_ref[...])


def _combine(tables, ys, h1, dest_t, wgt_t, fn, tile0, tt, kt):
    n = h1.shape[0]
    grid_spec = pltpu.PrefetchScalarGridSpec(
        num_scalar_prefetch=len(tables),
        grid=(n // tt,),
        in_specs=[pl.BlockSpec(memory_space=pl.ANY),
                  pl.BlockSpec((tt, D_MODEL), lambda i, *_: (i, 0)),
                  pl.BlockSpec((tt, TOP_K), lambda i, *_: (i, 0)),
                  pl.BlockSpec((tt, TOP_K), lambda i, *_: (i, 0)),
                  pl.BlockSpec((1, D_MODEL), lambda i, *_: (0, 0))],
        out_specs=pl.BlockSpec((tt, D_MODEL), lambda i, *_: (i, 0)),
        scratch_shapes=[pltpu.VMEM((2, kt, D_MODEL), BF16), pltpu.VMEM((2, tt, kt // 2), BF16),
                        pltpu.SemaphoreType.DMA((2,))],
    )
    return pl.pallas_call(
        functools.partial(_combine_kernel, tile0=tile0, kt=kt),
        grid_spec=grid_spec,
        out_shape=jax.ShapeDtypeStruct((n, D_MODEL), F32),
        compiler_params=_params(56, ("arbitrary",)),
        name="combine",
    )(*tables, ys, h1, dest_t, wgt_t, fn)


def _ceil_to(x, m):
    return (x + m - 1) // m * m


def _routing_tables(eid, tile_lens, n_groups_max):
    n = eid.shape[1]
    nt = len(tile_lens)
    starts = np.concatenate([[0], np.cumsum(tile_lens)[:-1]]).astype(np.int64)
    onehot = (eid[:, :, None] == jnp.arange(N_EXPERTS, dtype=I32)[None, None, :]).astype(I32)
    member = jnp.sum(onehot, axis=0)
    incl = jnp.cumsum(member, axis=0)
    before = incl - member
    rank0 = jnp.stack([before[int(s)] for s in starts])
    cnt = jnp.stack([incl[int(s) + int(l) - 1] for s, l in zip(starts, tile_lens)]) - rank0
    c8 = _ceil_to(cnt, BF16_ROWS)
    seg_loc = jnp.cumsum(c8, axis=1) - c8
    rows_e = jnp.sum(c8, axis=0)
    region_e = _ceil_to(rows_e, MOE_UNIT)
    start_e = jnp.cumsum(region_e) - region_e
    seg_pos = start_e[None, :] + jnp.cumsum(c8, axis=0) - c8

    def per_token(tab):
        return jnp.concatenate([jnp.broadcast_to(tab[t], (int(l), N_EXPERTS)) for t, l in enumerate(tile_lens)])
    base = per_token(seg_loc - rank0) + before
    dest = jnp.sum(onehot * base[None], axis=2).astype(I32)

    ntiles_e = region_e // MOE_UNIT
    groups_e = (ntiles_e + MOE_GROUP_UNITS - 1) // MOE_GROUP_UNITS
    gend_e = jnp.cumsum(groups_e)
    gstart_e = gend_e - groups_e
    n_groups = gend_e[-1]
    j = jnp.arange(n_groups_max, dtype=I32)
    j_act = jnp.minimum(j, n_groups - 1)
    e_j = jnp.minimum(jnp.sum((gend_e[None, :] <= j_act[:, None]).astype(I32), axis=1), N_EXPERTS - 1)
    sel = (e_j[:, None] == jnp.arange(N_EXPERTS, dtype=I32)[None, :]).astype(I32)
    pick = lambda v: jnp.sum(sel * v[None, :], axis=1)
    local = j_act - pick(gstart_e)
    grow = pick(start_e) + local * MOE_GROUP_ROWS
    gnt = jnp.where(j < n_groups, jnp.clip(pick(ntiles_e) - local * MOE_GROUP_UNITS, 0, MOE_GROUP_UNITS), 0)
    grows = jnp.where(j < n_groups, jnp.clip(pick(rows_e) - local * MOE_GROUP_ROWS, 0, MOE_GROUP_ROWS), 0)
    gtile = _ceil_to(grows, MOE_TILE_STEP)

    flat = lambda a: a.reshape(-1).astype(I32)
    seg_tabs = (flat(seg_loc), flat(seg_pos), flat(c8), jnp.sum(c8, axis=1).astype(I32))
    tail_tabs = ((start_e + rows_e).astype(I32), (region_e - rows_e).astype(I32))
    group_tabs = (e_j.astype(I32), grow.astype(I32), gnt.astype(I32), gtile.astype(I32),
                  n_groups.reshape(1).astype(I32))
    return dest, seg_tabs, tail_tabs, group_tabs


def _block_diag(w, per_block):
    h, d, _ = w.shape
    nb = h // per_block
    eye = jnp.eye(per_block, dtype=w.dtype)
    w4 = w.reshape(nb, per_block, d, d)
    out = jnp.einsum("bpij,pq->bpiqj", w4, eye)
    return out.reshape(nb, per_block * d, per_block * d)


def kernel(x_prompt, x_sample, state_conv, state_rglru, state_pool, meta_tokens, norm1, w_in, conv_w, conv_b, lru_wa, lru_ba, lru_wx, lru_bx, lru_lambda, pool_w, pool_scale, w_out, norm2, router_w, router_b, exp_wg, exp_bg, exp_wu, exp_bu, exp_wd, exp_bd, final_norm):
    batch, seq, _ = x_prompt.shape
    dec = x_sample.shape[0]
    n_prompt = batch * seq
    n_tok = n_prompt + dec
    l = 0
    row = lambda v: v.reshape(1, -1)

    xp = x_prompt.reshape(n_prompt, D_MODEL)
    xs_tok = x_sample.reshape(dec, D_MODEL)
    w_in_b = w_in[l].astype(BF16)
    w_out_b = w_out[l].astype(BF16)
    heads_per_block = V7X_MXU_DIM // LRU_HEAD_DIM
    mixw = (conv_w[l], row(conv_b[l]),
            _block_diag(lru_wa[l], heads_per_block).astype(BF16), row(lru_ba[l]),
            _block_diag(lru_wx[l], heads_per_block).astype(BF16), row(lru_bx[l]),
            row(lru_lambda[l]), pool_w[l].astype(BF16), row(pool_scale[l]))
    g1 = row(norm1[l])

    zeros = lambda r: jnp.zeros((1, r, LRU_WIDTH), F32)
    _, h_meta, conv_meta, pool_meta = _mix_seq(meta_tokens, g1, w_in_b, 1, N_META, N_META, 0,
                                               zeros(SUBLANES), zeros(SUBLANES), zeros(2 * SUBLANES), mixw)
    per_seq = lambda v: jnp.broadcast_to(v, (batch,) + v.shape[1:])
    yab_p, h_p, conv_tail, pool_tail = _mix_seq(xp, g1, w_in_b, batch, seq, TIME_TILE, N_META,
                                                per_seq(conv_meta), per_seq(h_meta), per_seq(pool_meta), mixw)
    proj_s = _proj(xs_tok, g1, w_in_b, dec)
    yab_s, h_s = _mix_step(proj_s, jnp.swapaxes(state_conv[l], 0, 1), state_rglru[l],
                           jnp.swapaxes(state_pool[l], 0, 1), mixw)

    out_w = (w_out_b, row(norm2[l]), router_w[l], router_b[l].reshape(N_EXPERTS, 1))
    h1_p, xn_p, eid_p, wgt_p = _out(yab_p, xp, *out_w, PROMPT_TILE)
    h1_s, xn_s, eid_s, wgt_s = _out(yab_s, xs_tok, *out_w, dec)

    n_ptiles = n_prompt // PROMPT_TILE
    tile_lens = [PROMPT_TILE] * n_ptiles + [dec]
    seg_pad = (BF16_ROWS - 1) * N_EXPERTS
    kt_p = _ceil_to(TOP_K * PROMPT_TILE + seg_pad, PERM_CHUNK)
    kt_s = _ceil_to(TOP_K * dec + seg_pad, PERM_CHUNK)
    units_max = (TOP_K * n_tok + len(tile_lens) * seg_pad) // MOE_UNIT + N_EXPERTS
    xs_rows = units_max * MOE_UNIT
    n_groups_max = (units_max + N_EXPERTS * (MOE_GROUP_UNITS - 1)) // MOE_GROUP_UNITS

    eid = jnp.concatenate([eid_p, eid_s], axis=1)
    dest, seg_tabs, tail_tabs, group_tabs = _routing_tables(eid, tile_lens, n_groups_max)
    disp_tabs = seg_tabs + tail_tabs
    dest_p, dest_s = dest[:, :n_prompt], dest[:, n_prompt:]

    xs = _dispatch(disp_tabs, xn_p, dest_p, None, 0, PROMPT_TILE, kt_p, xs_rows, False)
    xs = _dispatch(disp_tabs, xn_s, dest_s, xs, n_ptiles, dec, kt_s, xs_rows, True)
    ys = _moe(*group_tabs, xs, exp_wg[l], exp_wu[l], exp_wd[l], exp_bg[l], exp_bu[l], exp_bd[l], n_groups_max)
    fn = row(final_norm)
    y_p = _combine(seg_tabs, ys, h1_p, dest_p.T, wgt_p.T, fn, 0, PROMPT_TILE, kt_p)
    y_s = _combine(seg_tabs, ys, h1_s, dest_s.T, wgt_s.T, fn, n_ptiles, dec, kt_s)

    conv_p = conv_tail[:, SUBLANES - (CONV_WIDTH - 1):]
    pool_p = pool_tail[:, 2 * SUBLANES - POOL_BUF:]
    conv_s = jnp.concatenate([state_conv[l][:, 1:], proj_s[:, None, :LRU_WIDTH]], axis=1)
    pool_s = jnp.concatenate([state_pool[l][:, 1:], proj_s[:, None, 2 * LRU_WIDTH:]], axis=1)
    return (y_p.reshape(batch, seq, D_MODEL), y_s.reshape(dec, 1, D_MODEL),
            conv_p[None], h_p[None, :, SUBLANES - 1], pool_p[None],
            conv_s[None], h_s[None], pool_s[None])
```

```python
import functools

import numpy as np
import jax
import jax.numpy as jnp
from jax import lax
from jax.experimental import pallas as pl
from jax.experimental.pallas import tpu as pltpu

F32 = jnp.float32
BF16 = jnp.bfloat16
I32 = jnp.int32
U32 = jnp.uint32

D_MODEL = 2048
N_META = 16
LRU_WIDTH = 1024
LRU_HEADS = 16
LRU_HEAD_DIM = LRU_WIDTH // LRU_HEADS
CONV_WIDTH = 4
LRU_C = 8.0
POOL_WIDTH = D_MODEL - LRU_WIDTH
POOL_WINDOWS = (2, 4, 8, 16)
POOL_GROUP_DIM = POOL_WIDTH // len(POOL_WINDOWS)
POOL_BUF = max(POOL_WINDOWS) - 1
IN_WIDTH = 2 * LRU_WIDTH + POOL_WIDTH
N_EXPERTS = 32
TOP_K = 4
D_FF = D_MODEL
SWIGLU_ALPHA = 1.702
SWIGLU_LIMIT = 7.0
RMS_EPS = 1e-6

V7X_MXU_DIM = 256
SUBLANES = 8
BF16_ROWS = 16

PROMPT_TILE = 512
TIME_TILE = 256
MOE_UNIT = 128
MOE_TM_UNITS = 8
MOE_TM = MOE_TM_UNITS * MOE_UNIT
MOE_GROUP_UNITS = 11
MOE_GROUP_ROWS = MOE_UNIT * MOE_GROUP_UNITS
MOE_TILE_STEP = 64
MOE_WHOLE_GROUP_ROWS = (1088, 1152, 1216, 1280, 1344, 1408)
MOE_TF = V7X_MXU_DIM
PERM_CHUNK = V7X_MXU_DIM
ROUTE_CHUNK = 512


def _params(vmem_mb, sem):
    return pltpu.CompilerParams(dimension_semantics=sem, vmem_limit_bytes=vmem_mb << 20)


def _rmsnorm(x, g):
    return x * lax.rsqrt(jnp.mean(x * x, axis=-1, keepdims=True) + RMS_EPS) * g


def _resident(shape):
    return pl.BlockSpec(shape, lambda *_: (0,) * len(shape), pipeline_mode=pl.Buffered(1))


def _proj_kernel(x_ref, g_ref, w_ref, o_ref):
    xn = _rmsnorm(x_ref[...], g_ref[...])
    o_ref[...] = jnp.dot(xn.astype(BF16), w_ref[...], preferred_element_type=F32)


def _proj(x, g, w_bf16, tile):
    rows = x.shape[0]
    return pl.pallas_call(
        _proj_kernel,
        grid=(rows // tile,),
        in_specs=[pl.BlockSpec((tile, D_MODEL), lambda i: (i, 0)),
                  _resident((1, D_MODEL)), _resident((D_MODEL, IN_WIDTH))],
        out_specs=pl.BlockSpec((tile, IN_WIDTH), lambda i: (i, 0)),
        out_shape=jax.ShapeDtypeStruct((rows, IN_WIDTH), F32),
        compiler_params=_params(48, ("arbitrary",)),
        name="proj",
    )(x, g, w_bf16)


def _lru_coeffs(xc, wa_ref, ba_ref, wx_ref, bx_ref, lam_ref):
    xcb = xc.astype(BF16)
    nq = LRU_WIDTH // V7X_MXU_DIM
    rs, gs = [], []
    for q in range(nq):
        blk = xcb[:, q * V7X_MXU_DIM:(q + 1) * V7X_MXU_DIM]
        rs.append(jnp.dot(blk, wa_ref[q], preferred_element_type=F32))
        gs.append(jnp.dot(blk, wx_ref[q], preferred_element_type=F32))
    r = jax.nn.sigmoid(jnp.concatenate(rs, axis=-1) + ba_ref[...])
    i = jax.nn.sigmoid(jnp.concatenate(gs, axis=-1) + bx_ref[...])
    log_a = -LRU_C * r * jax.nn.softplus(-lam_ref[...])
    a = jnp.exp(log_a)
    th = jnp.tanh(log_a)
    beta = jnp.sqrt(-2.0 * th / (1.0 - th))
    return a, beta * i * xc


def _pool_project(wins, u, inv_cnt, pw_ref, ps_ref):
    outs = []
    for g in range(len(POOL_WINDOWS)):
        sl = slice(g * POOL_GROUP_DIM, (g + 1) * POOL_GROUP_DIM)
        d = (wins[g] * inv_cnt[g] - u[:, sl]).astype(BF16)
        outs.append(jnp.dot(d, pw_ref[g], preferred_element_type=F32))
    return jnp.concatenate(outs, axis=-1) * ps_ref[...]


def _mix_seq_kernel(x_ref, g1_ref, win_ref, conv0_ref, h0_ref, pool0_ref,
                    cw_ref, cb_ref, wa_ref, ba_ref, wx_ref, bx_ref, lam_ref, pw_ref, ps_ref,
                    yab_ref, ht_ref, ctail_ref, ptail_ref, cext, pext, hcar, *, tt, start):
    t = pl.program_id(1)
    hist = 2 * SUBLANES

    @pl.when(t == 0)
    def _():
        cext[0:SUBLANES, :] = conv0_ref[...]
        pext[0:hist, :] = pool0_ref[...]
        hcar[...] = h0_ref[...]

    xnb = _rmsnorm(x_ref[...], g1_ref[...]).astype(BF16)
    xa = jnp.dot(xnb, win_ref[:, :LRU_WIDTH], preferred_element_type=F32)
    ga = jnp.dot(xnb, win_ref[:, LRU_WIDTH:2 * LRU_WIDTH], preferred_element_type=F32)
    u = jnp.dot(xnb, win_ref[:, 2 * LRU_WIDTH:], preferred_element_type=F32)

    cext[SUBLANES:SUBLANES + tt, :] = xa
    cw = cw_ref[...]
    xc = cb_ref[...] + cw[CONV_WIDTH - 1:CONV_WIDTH] * xa
    for k in range(1, CONV_WIDTH):
        xc = xc + cw[CONV_WIDTH - 1 - k:CONV_WIDTH - k] * cext[SUBLANES - k:SUBLANES - k + tt, :]
    cext[0:SUBLANES, :] = cext[tt:tt + SUBLANES, :]
    ctail_ref[...] = cext[0:SUBLANES, :]

    a, b = _lru_coeffs(xc, wa_ref, ba_ref, wx_ref, bx_ref, lam_ref)

    groups = tt // SUBLANES
    a = a.reshape(groups, SUBLANES, LRU_WIDTH)
    b = b.reshape(groups, SUBLANES, LRU_WIDTH)
    sub = lax.broadcasted_iota(I32, (groups, SUBLANES, LRU_WIDTH), 1)
    s = 1
    while s < SUBLANES:
        keep = sub >= s
        a_prev = jnp.where(keep, pltpu.roll(a, s, 1), 1.0)
        b_prev = jnp.where(keep, pltpu.roll(b, s, 1), 0.0)
        b = b + a * b_prev
        a = a * a_prev
        s *= 2
    h = hcar[SUBLANES - 1:SUBLANES, :]
    hrows = []
    for g in range(groups):
        hg = a[g] * h + b[g]
        hrows.append(hg)
        h = hg[SUBLANES - 1:SUBLANES, :]
    hs = jnp.concatenate(hrows, axis=0)
    hcar[...] = hrows[-1]
    ht_ref[...] = hrows[-1]
    ya = hs * jax.nn.gelu(ga)

    pext[hist:hist + tt, :] = u
    e = pext[...]
    wins = []
    shift = 1
    for g in range(len(POOL_WINDOWS)):
        e = e + pltpu.roll(e, shift, 0)
        wins.append(e[hist:hist + tt, :POOL_GROUP_DIM])
        if g + 1 < len(POOL_WINDOWS):
            e = e[:, POOL_GROUP_DIM:]
        shift *= 2
    pext[0:hist, :] = pext[tt:tt + hist, :]
    ptail_ref[...] = pext[0:hist, :]

    if start >= POOL_BUF:
        inv_cnt = [1.0 / w for w in POOL_WINDOWS]
    else:
        pos = start + t * tt + lax.broadcasted_iota(I32, (tt, 1), 0)
        inv_cnt = [1.0 / jnp.minimum(w, pos + 1).astype(F32) for w in POOL_WINDOWS]
    yb = _pool_project(wins, u, inv_cnt, pw_ref, ps_ref)

    yab_ref[:, :LRU_WIDTH] = ya.astype(BF16)
    yab_ref[:, LRU_WIDTH:] = yb.astype(BF16)


def _mix_seq(x, g1, w_in_bf16, batch, seq, tt, start, conv0, h0, pool0, mixw):
    nt = seq // tt
    hist = 2 * SUBLANES
    rows = lambda w: pl.BlockSpec((tt, w), lambda b, t: (b * nt + t, 0))
    state = lambda r: pl.BlockSpec((None, r, LRU_WIDTH), lambda b, t: (b, 0, 0))
    tail = lambda r: jax.ShapeDtypeStruct((batch, r, LRU_WIDTH), F32)
    return pl.pallas_call(
        functools.partial(_mix_seq_kernel, tt=tt, start=start),
        grid=(batch, nt),
        in_specs=[rows(D_MODEL), _resident((1, D_MODEL)), _resident((D_MODEL, IN_WIDTH)),
                  state(SUBLANES), state(SUBLANES), state(hist)] + [_resident(w.shape) for w in mixw],
        out_specs=[rows(D_MODEL), state(SUBLANES), state(SUBLANES), state(hist)],
        out_shape=[jax.ShapeDtypeStruct((batch * seq, D_MODEL), BF16), tail(SUBLANES), tail(SUBLANES), tail(hist)],
        scratch_shapes=[pltpu.VMEM((tt + SUBLANES, LRU_WIDTH), F32),
                        pltpu.VMEM((tt + hist, POOL_WIDTH), F32),
                        pltpu.VMEM((SUBLANES, LRU_WIDTH), F32)],
        compiler_params=_params(56, ("arbitrary", "arbitrary")),
        name="mix_seq",
    )(x, g1, w_in_bf16, conv0, h0, pool0, *mixw)


def _mix_step_kernel(xa_ref, ga_ref, ub_ref, sconv_ref, sh_ref, spool_ref,
                     cw_ref, cb_ref, wa_ref, ba_ref, wx_ref, bx_ref, lam_ref, pw_ref, ps_ref,
                     yab_ref, h_ref):
    xa = xa_ref[...]
    cw = cw_ref[...]
    xc = cb_ref[...] + cw[CONV_WIDTH - 1:CONV_WIDTH] * xa
    for k in range(1, CONV_WIDTH):
        xc = xc + cw[CONV_WIDTH - 1 - k:CONV_WIDTH - k] * sconv_ref[CONV_WIDTH - 1 - k]
    a, b = _lru_coeffs(xc, wa_ref, ba_ref, wx_ref, bx_ref, lam_ref)
    h = a * sh_ref[...] + b
    h_ref[...] = h
    ya = h * jax.nn.gelu(ga_ref[...])

    u = ub_ref[...]
    wins = []
    for g, w in enumerate(POOL_WINDOWS):
        sl = slice(g * POOL_GROUP_DIM, (g + 1) * POOL_GROUP_DIM)
        acc = u[:, sl]
        for k in range(1, w):
            acc = acc + spool_ref[POOL_BUF - k, :, sl]
        wins.append(acc)
    yb = _pool_project(wins, u, [1.0 / w for w in POOL_WINDOWS], pw_ref, ps_ref)
    yab_ref[:, :LRU_WIDTH] = ya.astype(BF16)
    yab_ref[:, LRU_WIDTH:] = yb.astype(BF16)


def _mix_step(proj, sconv_t, sh, spool_t, mixw):
    rows = proj.shape[0]
    col = lambda c: pl.BlockSpec((rows, LRU_WIDTH), lambda i: (0, c))
    full = lambda a: pl.BlockSpec(a.shape, lambda i: (0,) * a.ndim)
    return pl.pallas_call(
        _mix_step_kernel,
        grid=(1,),
        in_specs=[col(0), col(1), col(2), full(sconv_t), full(sh), full(spool_t)] + [full(w) for w in mixw],
        out_specs=[pl.BlockSpec((rows, D_MODEL), lambda i: (0, 0)),
                   pl.BlockSpec((rows, LRU_WIDTH), lambda i: (0, 0))],
        out_shape=[jax.ShapeDtypeStruct((rows, D_MODEL), BF16),
                   jax.ShapeDtypeStruct((rows, LRU_WIDTH), F32)],
        compiler_params=_params(48, ("arbitrary",)),
        name="mix_step",
    )(proj, proj, proj, sconv_t, sh, spool_t, *mixw)


def _out_kernel(yab_ref, x_ref, wo_ref, g2_ref, rw_both_ref, rw_hi_ref, rb_ref,
                h1_ref, xn_ref, eid_ref, wgt_ref, *, tile):
    h1_ref[...] = x_ref[...] + jnp.dot(yab_ref[...], wo_ref[...], preferred_element_type=F32)

    chunk = min(ROUTE_CHUNK, tile)

    def route(c, carry):
        rows = pl.ds(pl.multiple_of(c * chunk, chunk), chunk)
        xn = _rmsnorm(h1_ref[rows, :], g2_ref[...])
        xn_hi = xn.astype(BF16)
        xn_ref[rows, :] = xn_hi
        xn_lo = (xn - xn_hi.astype(F32)).astype(BF16)
        nt_dims = (((1,), (1,)), ((), ()))
        both = lax.dot_general(rw_both_ref[...], xn_hi, nt_dims, preferred_element_type=F32)
        logits = (both[:N_EXPERTS] + both[N_EXPERTS:]
                  + lax.dot_general(rw_hi_ref[...], xn_lo, nt_dims, preferred_element_type=F32)) + rb_ref[...]
        eidx = lax.broadcasted_iota(I32, (N_EXPERTS, chunk), 0)
        vals, ids = [], []
        for _ in range(TOP_K):
            m = jnp.max(logits, axis=0, keepdims=True)
            idx = jnp.min(jnp.where(logits == m, eidx, N_EXPERTS), axis=0, keepdims=True)
            vals.append(m)
            ids.append(idx)
            logits = jnp.where(eidx == idx, -jnp.inf, logits)
        ex = [jnp.exp(v - vals[0]) for v in vals]
        tot = ex[0]
        for e_ in ex[1:]:
            tot = tot + e_
        eid_ref[:, rows] = jnp.concatenate(ids, axis=0)
        wgt_ref[:, rows] = jnp.concatenate([e_ / tot for e_ in ex], axis=0)
        return carry
    lax.fori_loop(0, tile // chunk, route, 0)


def _out(yab, x, wo_bf16, g2, rw, rb, tile):
    rw_hi = rw.T.astype(BF16)
    rw_lo = (rw.T - rw_hi.astype(F32)).astype(BF16)
    rw_both = jnp.concatenate([rw_hi, rw_lo], axis=0)
    n = x.shape[0]
    rowblk = lambda w: pl.BlockSpec((tile, w), lambda i: (i, 0))
    colblk = pl.BlockSpec((TOP_K, tile), lambda i: (0, i))
    return pl.pallas_call(
        functools.partial(_out_kernel, tile=tile),
        grid=(n // tile,),
        in_specs=[rowblk(D_MODEL), rowblk(D_MODEL), _resident((D_MODEL, D_MODEL)), _resident((1, D_MODEL)),
                  _resident((2 * N_EXPERTS, D_MODEL)), _resident((N_EXPERTS, D_MODEL)), _resident((N_EXPERTS, 1))],
        out_specs=[rowblk(D_MODEL), rowblk(D_MODEL), colblk, colblk],
        out_shape=[jax.ShapeDtypeStruct((n, D_MODEL), F32), jax.ShapeDtypeStruct((n, D_MODEL), BF16),
                   jax.ShapeDtypeStruct((TOP_K, n), I32), jax.ShapeDtypeStruct((TOP_K, n), F32)],
        compiler_params=_params(48, ("arbitrary",)),
        name="out_router",
    )(yab, x, wo_bf16, g2, rw_both, rw_hi, rb)


def _for_segments(tile, seg_len_ref, max_len, fn):
    def per_expert(e, c):
        s = tile * N_EXPERTS + e
        length = seg_len_ref[s]
        size = max_len
        while size >= BF16_ROWS:
            @pl.when((length // size) % 2 == 1)
            def _(size=size):
                fn(s, length // (2 * size) * (2 * size), size)
            size //= 2
        return c
    lax.fori_loop(0, N_EXPERTS, per_expert, 0)


def _seg_rows(ref, start, rows):
    return ref.at[pl.ds(pl.multiple_of(start, BF16_ROWS), rows)]


def _wait_rows(total, max_rows, copy_of_rows):
    size = 1 << (max_rows.bit_length() - 1)
    while size >= BF16_ROWS:
        @pl.when((total // size) % 2 == 1)
        def _(size=size):
            copy_of_rows(size).wait()
        size //= 2


def _dispatch_kernel(seg_loc_ref, seg_dst_ref, seg_len_ref, tile_rows_ref, tail_dst_ref, tail_len_ref,
                     xn_ref, dest_ref, *rest, tile0, kt, zero_tails):
    xs_hbm, cb, zb, sem, zsem = rest[-5:]
    i = pl.program_id(0)
    n_steps = pl.num_programs(0)
    slot = i % 2
    tile = tile0 + i
    tt = xn_ref.shape[0]

    def seg_copy(sl, s, off, rows):
        return pltpu.make_async_copy(_seg_rows(cb.at[sl], seg_loc_ref[s] + off, rows),
                                     _seg_rows(xs_hbm, seg_dst_ref[s] + off, rows), sem.at[sl])

    def start_segments(tl, sl):
        _for_segments(tl, seg_len_ref, tt, lambda s, off, rows: seg_copy(sl, s, off, rows).start())

    def wait_segments(tl, sl):
        _wait_rows(tile_rows_ref[tl], kt, lambda rows: pltpu.make_async_copy(
            cb.at[sl, pl.ds(0, rows)], xs_hbm.at[pl.ds(0, rows)], sem.at[sl]))

    @pl.when(i >= 2)
    def _():
        wait_segments(tile - 2, slot)

    x = xn_ref[...]
    d = [dest_ref[k:k + 1, :] for k in range(TOP_K)]
    for c0 in range(0, kt, PERM_CHUNK):
        r = c0 + lax.broadcasted_iota(I32, (PERM_CHUNK, tt), 0)
        p = jnp.zeros((PERM_CHUNK, tt), F32)
        for k in range(TOP_K):
            p = jnp.where(d[k] == r, 1.0, p)
        rows = jnp.dot(p.astype(BF16), x, preferred_element_type=F32)
        cb[slot, c0:c0 + PERM_CHUNK, :] = rows.astype(BF16)

    start_segments(tile, slot)

    @pl.when(i == n_steps - 1)
    def _():
        @pl.when(i >= 1)
        def _():
            wait_segments(tile - 1, 1 - slot)
        wait_segments(tile, slot)

    if zero_tails:
        zb[...] = jnp.zeros(zb.shape, BF16)

        def tail_copy(e, j):
            return pltpu.make_async_copy(zb, _seg_rows(xs_hbm, tail_dst_ref[e] + j * BF16_ROWS, BF16_ROWS),
                                         zsem.at[0])

        def tails(method):
            def per_expert(e, c):
                def per_chunk(j, c2):
                    getattr(tail_copy(e, j), method)()
                    return c2
                lax.fori_loop(0, tail_len_ref[e] // BF16_ROWS, per_chunk, 0)
                return c
            lax.fori_loop(0, N_EXPERTS, per_expert, 0)

        @pl.when(i == n_steps - 1)
        def _():
            tails("start")
            tails("wait")


def _dispatch(tables, xn, dest, xs, tile0, tt, kt, xs_rows, zero_tails):
    n = xn.shape[0]
    in_specs = [pl.BlockSpec((tt, D_MODEL), lambda i, *_: (i, 0)),
                pl.BlockSpec((TOP_K, tt), lambda i, *_: (0, i))]
    args = [*tables, xn, dest]
    aliases = {}
    if xs is not None:
        in_specs.append(pl.BlockSpec(memory_space=pl.ANY))
        aliases = {len(args): 0}
        args.append(xs)
    grid_spec = pltpu.PrefetchScalarGridSpec(
        num_scalar_prefetch=len(tables),
        grid=(n // tt,),
        in_specs=in_specs,
        out_specs=pl.BlockSpec(memory_space=pl.ANY),
        scratch_shapes=[pltpu.VMEM((2, kt, D_MODEL), BF16), pltpu.VMEM((BF16_ROWS, D_MODEL), BF16),
                        pltpu.SemaphoreType.DMA((2,)), pltpu.SemaphoreType.DMA((1,))],
    )
    return pl.pallas_call(
        functools.partial(_dispatch_kernel, tile0=tile0, kt=kt, zero_tails=zero_tails),
        grid_spec=grid_spec,
        out_shape=jax.ShapeDtypeStruct((xs_rows, D_MODEL), BF16),
        input_output_aliases=aliases,
        compiler_params=_params(48, ("arbitrary",)),
        name="dispatch",
    )(*args)


def _moe_kernel(ge_ref, grow_ref, gnu_ref, gtile_ref, ng_ref,
                xs_hbm, wg_hbm, wu_hbm, wd_hbm, bg_ref, bu_ref, bd_ref,
                ys_hbm, xb, acc, ypk, wgf, wuf, wdf, xsem, wsem, osem, *, n_f):
    g = pl.program_id(0)
    n_groups = ng_ref[0]
    xslot = g % 2

    def unit_rows(j):
        return pl.ds(pl.multiple_of(j * MOE_UNIT, MOE_UNIT), MOE_UNIT)

    def hbm_unit(ref, grp, j):
        return ref.at[pl.ds(pl.multiple_of(grow_ref[grp] + j * MOE_UNIT, MOE_UNIT), MOE_UNIT)]

    def copy_in(grp, sl, j):
        return pltpu.make_async_copy(hbm_unit(xs_hbm, grp, j), xb.at[sl, unit_rows(j)], xsem.at[sl])

    def copy_out(grp, j):
        return pltpu.make_async_copy(ypk.at[unit_rows(j)], hbm_unit(ys_hbm, grp, j), osem.at[0])

    def for_units(grp, fn):
        def body(j, c):
            fn(grp, j)
            return c
        lax.fori_loop(0, gnu_ref[grp], body, 0)

    def weight_copies(grp, f, ws):
        e = ge_ref[grp]
        cols = pl.ds(pl.multiple_of(f * MOE_TF, MOE_TF), MOE_TF)
        return (pltpu.make_async_copy(wg_hbm.at[e, :, cols], wgf.at[ws], wsem.at[ws, 0]),
                pltpu.make_async_copy(wu_hbm.at[e, :, cols], wuf.at[ws], wsem.at[ws, 1]),
                pltpu.make_async_copy(wd_hbm.at[e, cols, :], wdf.at[ws], wsem.at[ws, 2]))

    @pl.when(g < n_groups)
    def _():
        @pl.when(g == 0)
        def _():
            for_units(0, lambda grp, j: copy_in(grp, 0, j).start())
            for cp in weight_copies(0, 0, 0):
                cp.start()
        for_units(g, lambda grp, j: copy_in(grp, xslot, j).wait())

        @pl.when(g + 1 < n_groups)
        def _():
            for_units(g + 1, lambda grp, j: copy_in(grp, 1 - xslot, j).start())

        def init(grp, j):
            acc[unit_rows(j), :] = jnp.broadcast_to(bd_ref[...], (MOE_UNIT, D_MODEL))
        for_units(g, init)
        n_units = gnu_ref[g]
        tile_rows = gtile_ref[g]

        def chunk(f, c):
            ws = f % 2
            for cp in weight_copies(g, f, ws):
                cp.wait()

            @pl.when(f + 1 < n_f)
            def _():
                for cp in weight_copies(g, f + 1, 1 - ws):
                    cp.start()

            @pl.when(jnp.logical_and(f + 1 == n_f, g + 1 < n_groups))
            def _():
                for cp in weight_copies(g + 1, 0, 1 - ws):
                    cp.start()

            bg = bg_ref[pl.ds(f, 1), :]
            bu = bu_ref[pl.ds(f, 1), :]

            def ffn(start, size):
                x = xb[xslot, pl.ds(start, size), :]
                gg = jnp.dot(x, wgf[ws].astype(BF16), preferred_element_type=F32) + bg
                uu = jnp.dot(x, wuf[ws].astype(BF16), preferred_element_type=F32) + bu
                gg = jnp.minimum(gg, SWIGLU_LIMIT)
                uu = jnp.clip(uu, -SWIGLU_LIMIT, SWIGLU_LIMIT)
                hdn = gg * jax.nn.sigmoid(SWIGLU_ALPHA * gg) * (uu + 1.0)
                acc[pl.ds(start, size), :] += jnp.dot(hdn.astype(BF16), wdf[ws].astype(BF16),
                                                      preferred_element_type=F32)

            whole = jnp.bool_(False)
            for rows in MOE_WHOLE_GROUP_ROWS:
                whole = jnp.logical_or(whole, tile_rows == rows)

                @pl.when(tile_rows == rows)
                def _(rows=rows):
                    ffn(0, rows)

            @pl.when(jnp.logical_not(whole))
            def _():
                def full_tile(j, c2):
                    ffn(pl.multiple_of(j * MOE_TM, MOE_TM), MOE_TM)
                    return c2
                lax.fori_loop(0, n_units // MOE_TM_UNITS, full_tile, 0)
                part = MOE_TM_UNITS // 2
                while part >= 1:
                    @pl.when((n_units // part) % 2 == 1)
                    def _(part=part):
                        done = n_units // (2 * part) * (2 * part)
                        ffn(pl.multiple_of(done * MOE_UNIT, MOE_UNIT), part * MOE_UNIT)
                    part //= 2
            return c
        lax.fori_loop(0, n_f, chunk, 0)

        @pl.when(g > 0)
        def _():
            for_units(g - 1, lambda grp, j: copy_out(grp, j).wait())

        def finish(grp, j):
            ypk[unit_rows(j), :] = acc[unit_rows(j), :].astype(BF16)
            copy_out(grp, j).start()
        for_units(g, finish)

    @pl.when(g == pl.num_programs(0) - 1)
    def _():
        for_units(n_groups - 1, lambda grp, j: copy_out(grp, j).wait())


def _moe(ge, grow, gnu, gtile, ng, xs, wg, wu, wd, bg, bu, bd, n_groups_max):
    n_f = D_FF // MOE_TF
    assert n_f % 2 == 0
    per_expert = lambda shape: pl.BlockSpec((None,) + shape, lambda g, ge, *_: (ge[g], 0, 0))
    any_spec = pl.BlockSpec(memory_space=pl.ANY)
    grid_spec = pltpu.PrefetchScalarGridSpec(
        num_scalar_prefetch=5,
        grid=(n_groups_max,),
        in_specs=[any_spec, any_spec, any_spec, any_spec,
                  per_expert((n_f, MOE_TF)), per_expert((n_f, MOE_TF)), per_expert((1, D_MODEL))],
        out_specs=any_spec,
        scratch_shapes=[
            pltpu.VMEM((2, MOE_GROUP_ROWS, D_MODEL), BF16),
            pltpu.VMEM((MOE_GROUP_ROWS, D_MODEL), F32),
            pltpu.VMEM((MOE_GROUP_ROWS, D_MODEL), BF16),
            pltpu.VMEM((2, D_MODEL, MOE_TF), F32),
            pltpu.VMEM((2, D_MODEL, MOE_TF), F32),
            pltpu.VMEM((2, MOE_TF, D_MODEL), F32),
            pltpu.SemaphoreType.DMA((2,)),
            pltpu.SemaphoreType.DMA((2, 3)),
            pltpu.SemaphoreType.DMA((1,)),
        ],
    )
    return pl.pallas_call(
        functools.partial(_moe_kernel, n_f=n_f),
        grid_spec=grid_spec,
        out_shape=jax.ShapeDtypeStruct(xs.shape, BF16),
        compiler_params=_params(56, ("arbitrary",)),
        name="moe_ffn",
    )(ge, grow, gnu, gtile, ng, xs, wg, wu, wd,
      bg.reshape(N_EXPERTS, n_f, MOE_TF), bu.reshape(N_EXPERTS, n_f, MOE_TF), bd.reshape(N_EXPERTS, 1, D_MODEL))


def _combine_kernel(seg_loc_ref, seg_src_ref, seg_len_ref, tile_rows_ref,
                    ys_hbm, h1_ref, dest_ref, w_ref, fn_ref, y_ref, sb, wbuf, sem, *, tile0, kt):
    i = pl.program_id(0)
    n_steps = pl.num_programs(0)
    slot = i % 2
    tile = tile0 + i
    tt = h1_ref.shape[0]

    def seg_copy(sl, s, off, rows):
        return pltpu.make_async_copy(_seg_rows(ys_hbm, seg_src_ref[s] + off, rows),
                                     _seg_rows(sb.at[sl], seg_loc_ref[s] + off, rows), sem.at[sl])

    def start_segments(tl, sl):
        _for_segments(tl, seg_len_ref, tt, lambda s, off, rows: seg_copy(sl, s, off, rows).start())

    @pl.when(i == 0)
    def _():
        sb[...] = jnp.zeros(sb.shape, BF16)
        start_segments(tile, slot)

    @pl.when(i + 1 < n_steps)
    def _():
        start_segments(tile + 1, 1 - slot)

    _wait_rows(tile_rows_ref[tile], kt, lambda rows: pltpu.make_async_copy(
        ys_hbm.at[pl.ds(0, rows)], sb.at[slot, pl.ds(0, rows)], sem.at[slot]))

    acc = h1_ref[...]
    d = [jnp.broadcast_to(dest_ref[:, k:k + 1], (tt, PERM_CHUNK)) for k in range(TOP_K)]
    w = [jnp.broadcast_to(w_ref[:, k:k + 1], (tt, PERM_CHUNK)) for k in range(TOP_K)]
    half = kt // 2
    for h in range(2):
        for c0 in range(0, half, PERM_CHUNK):
            col = h * half + c0 + lax.broadcasted_iota(I32, (tt, PERM_CHUNK), 1)
            wm = jnp.zeros((tt, PERM_CHUNK), F32)
            for k in range(TOP_K):
                wm = jnp.where(d[k] == col, w[k], wm)
            wbuf[h, :, c0:c0 + PERM_CHUNK] = wm.astype(BF16)
        acc = acc + jnp.dot(wbuf[h], sb[slot, h * half:(h + 1) * half, :], preferred_element_type=F32)
    y_ref[...] = _rmsnorm(acc, fn_ref[...])


def _combine(tables, ys, h1, dest_t, wgt_t, fn, tile0, tt, kt):
    n = h1.shape[0]
    grid_spec = pltpu.PrefetchScalarGridSpec(
        num_scalar_prefetch=len(tables),
        grid=(n // tt,),
        in_specs=[pl.BlockSpec(memory_space=pl.ANY),
                  pl.BlockSpec((tt, D_MODEL), lambda i, *_: (i, 0)),
                  pl.BlockSpec((tt, TOP_K), lambda i, *_: (i, 0)),
                  pl.BlockSpec((tt, TOP_K), lambda i, *_: (i, 0)),
                  pl.BlockSpec((1, D_MODEL), lambda i, *_: (0, 0))],
        out_specs=pl.BlockSpec((tt, D_MODEL), lambda i, *_: (i, 0)),
        scratch_shapes=[pltpu.VMEM((2, kt, D_MODEL), BF16), pltpu.VMEM((2, tt, kt // 2), BF16),
                        pltpu.SemaphoreType.DMA((2,))],
    )
    return pl.pallas_call(
        functools.partial(_combine_kernel, tile0=tile0, kt=kt),
        grid_spec=grid_spec,
        out_shape=jax.ShapeDtypeStruct((n, D_MODEL), F32),
        compiler_params=_params(56, ("arbitrary",)),
        name="combine",
    )(*tables, ys, h1, dest_t, wgt_t, fn)


def _ceil_to(x, m):
    return (x + m - 1) // m * m


def _routing_tables(eid, tile_lens, n_groups_max):
    n = eid.shape[1]
    nt = len(tile_lens)
    starts = np.concatenate([[0], np.cumsum(tile_lens)[:-1]]).astype(np.int64)
    onehot = (eid[:, :, None] == jnp.arange(N_EXPERTS, dtype=I32)[None, None, :]).astype(I32)
    member = jnp.sum(onehot, axis=0)
    incl = jnp.cumsum(member, axis=0)
    before = incl - member
    rank0 = jnp.stack([before[int(s)] for s in starts])
    cnt = jnp.stack([incl[int(s) + int(l) - 1] for s, l in zip(starts, tile_lens)]) - rank0
    c8 = _ceil_to(cnt, BF16_ROWS)
    seg_loc = jnp.cumsum(c8, axis=1) - c8
    rows_e = jnp.sum(c8, axis=0)
    region_e = _ceil_to(rows_e, MOE_UNIT)
    start_e = jnp.cumsum(region_e) - region_e
    seg_pos = start_e[None, :] + jnp.cumsum(c8, axis=0) - c8

    def per_token(tab):
        return jnp.concatenate([jnp.broadcast_to(tab[t], (int(l), N_EXPERTS)) for t, l in enumerate(tile_lens)])
    base = per_token(seg_loc - rank0) + before
    dest = jnp.sum(onehot * base[None], axis=2).astype(I32)

    ntiles_e = region_e // MOE_UNIT
    groups_e = (ntiles_e + MOE_GROUP_UNITS - 1) // MOE_GROUP_UNITS
    gend_e = jnp.cumsum(groups_e)
    gstart_e = gend_e - groups_e
    n_groups = gend_e[-1]
    j = jnp.arange(n_groups_max, dtype=I32)
    j_act = jnp.minimum(j, n_groups - 1)
    e_j = jnp.minimum(jnp.sum((gend_e[None, :] <= j_act[:, None]).astype(I32), axis=1), N_EXPERTS - 1)
    sel = (e_j[:, None] == jnp.arange(N_EXPERTS, dtype=I32)[None, :]).astype(I32)
    pick = lambda v: jnp.sum(sel * v[None, :], axis=1)
    local = j_act - pick(gstart_e)
    grow = pick(start_e) + local * MOE_GROUP_ROWS
    gnt = jnp.where(j < n_groups, jnp.clip(pick(ntiles_e) - local * MOE_GROUP_UNITS, 0, MOE_GROUP_UNITS), 0)
    grows = jnp.where(j < n_groups, jnp.clip(pick(rows_e) - local * MOE_GROUP_ROWS, 0, MOE_GROUP_ROWS), 0)
    gtile = _ceil_to(grows, MOE_TILE_STEP)

    flat = lambda a: a.reshape(-1).astype(I32)
    seg_tabs = (flat(seg_loc), flat(seg_pos), flat(c8), jnp.sum(c8, axis=1).astype(I32))
    tail_tabs = ((start_e + rows_e).astype(I32), (region_e - rows_e).astype(I32))
    group_tabs = (e_j.astype(I32), grow.astype(I32), gnt.astype(I32), gtile.astype(I32),
                  n_groups.reshape(1).astype(I32))
    return dest, seg_tabs, tail_tabs, group_tabs


def _block_diag(w, per_block):
    h, d, _ = w.shape
    nb = h // per_block
    eye = jnp.eye(per_block, dtype=w.dtype)
    w4 = w.reshape(nb, per_block, d, d)
    out = jnp.einsum("bpij,pq->bpiqj", w4, eye)
    return out.reshape(nb, per_block * d, per_block * d)


def kernel(x_prompt, x_sample, state_conv, state_rglru, state_pool, meta_tokens, norm1, w_in, conv_w, conv_b, lru_wa, lru_ba, lru_wx, lru_bx, lru_lambda, pool_w, pool_scale, w_out, norm2, router_w, router_b, exp_wg, exp_bg, exp_wu, exp_bu, exp_wd, exp_bd, final_norm):
    batch, seq, _ = x_prompt.shape
    dec = x_sample.shape[0]
    n_prompt = batch * seq
    n_tok = n_prompt + dec
    l = 0
    row = lambda v: v.reshape(1, -1)

    xp = x_prompt.reshape(n_prompt, D_MODEL)
    xs_tok = x_sample.reshape(dec, D_MODEL)
    w_in_b = w_in[l].astype(BF16)
    w_out_b = w_out[l].astype(BF16)
    heads_per_block = V7X_MXU_DIM // LRU_HEAD_DIM
    mixw = (conv_w[l], row(conv_b[l]),
            _block_diag(lru_wa[l], heads_per_block).astype(BF16), row(lru_ba[l]),
            _block_diag(lru_wx[l], heads_per_block).astype(BF16), row(lru_bx[l]),
            row(lru_lambda[l]), pool_w[l].astype(BF16), row(pool_scale[l]))
    g1 = row(norm1[l])

    zeros = lambda r: jnp.zeros((1, r, LRU_WIDTH), F32)
    _, h_meta, conv_meta, pool_meta = _mix_seq(meta_tokens, g1, w_in_b, 1, N_META, N_META, 0,
                                               zeros(SUBLANES), zeros(SUBLANES), zeros(2 * SUBLANES), mixw)
    per_seq = lambda v: jnp.broadcast_to(v, (batch,) + v.shape[1:])
    yab_p, h_p, conv_tail, pool_tail = _mix_seq(xp, g1, w_in_b, batch, seq, TIME_TILE, N_META,
                                                per_seq(conv_meta), per_seq(h_meta), per_seq(pool_meta), mixw)
    proj_s = _proj(xs_tok, g1, w_in_b, dec)
    yab_s, h_s = _mix_step(proj_s, jnp.swapaxes(state_conv[l], 0, 1), state_rglru[l],
                           jnp.swapaxes(state_pool[l], 0, 1), mixw)

    out_w = (w_out_b, row(norm2[l]), router_w[l], router_b[l].reshape(N_EXPERTS, 1))
    h1_p, xn_p, eid_p, wgt_p = _out(yab_p, xp, *out_w, PROMPT_TILE)
    h1_s, xn_s, eid_s, wgt_s = _out(yab_s, xs_tok, *out_w, dec)

    n_ptiles = n_prompt // PROMPT_TILE
    tile_lens = [PROMPT_TILE] * n_ptiles + [dec]
    seg_pad = (BF16_ROWS - 1) * N_EXPERTS
    kt_p = _ceil_to(TOP_K * PROMPT_TILE + seg_pad, PERM_CHUNK)
    kt_s = _ceil_to(TOP_K * dec + seg_pad, PERM_CHUNK)
    units_max = (TOP_K * n_tok + len(tile_lens) * seg_pad) // MOE_UNIT + N_EXPERTS
    xs_rows = units_max * MOE_UNIT
    n_groups_max = (units_max + N_EXPERTS * (MOE_GROUP_UNITS - 1)) // MOE_GROUP_UNITS

    eid = jnp.concatenate([eid_p, eid_s], axis=1)
    dest, seg_tabs, tail_tabs, group_tabs = _routing_tables(eid, tile_lens, n_groups_max)
    disp_tabs = seg_tabs + tail_tabs
    dest_p, dest_s = dest[:, :n_prompt], dest[:, n_prompt:]

    xs = _dispatch(disp_tabs, xn_p, dest_p, None, 0, PROMPT_TILE, kt_p, xs_rows, False)
    xs = _dispatch(disp_tabs, xn_s, dest_s, xs, n_ptiles, dec, kt_s, xs_rows, True)
    ys = _moe(*group_tabs, xs, exp_wg[l], exp_wu[l], exp_wd[l], exp_bg[l], exp_bu[l], exp_bd[l], n_groups_max)
    fn = row(final_norm)
    y_p = _combine(seg_tabs, ys, h1_p, dest_p.T, wgt_p.T, fn, 0, PROMPT_TILE, kt_p)
    y_s = _combine(seg_tabs, ys, h1_s, dest_s.T, wgt_s.T, fn, n_ptiles, dec, kt_s)

    conv_p = conv_tail[:, SUBLANES - (CONV_WIDTH - 1):]
    pool_p = pool_tail[:, 2 * SUBLANES - POOL_BUF:]
    conv_s = jnp.concatenate([state_conv[l][:, 1:], proj_s[:, None, :LRU_WIDTH]], axis=1)
    pool_s = jnp.concatenate([state_pool[l][:, 1:], proj_s[:, None, 2 * LRU_WIDTH:]], axis=1)
    return (y_p.reshape(batch, seq, D_MODEL), y_s.reshape(dec, 1, D_MODEL),
            conv_p[None], h_p[None, :, SUBLANES - 1], pool_p[None],
            conv_s[None], h_s[None], pool_s[None])
```

```python
import functools

import numpy as np
import jax
import jax.numpy as jnp
from jax import lax
from jax.experimental import pallas as pl
from jax.experimental.pallas import tpu as pltpu

F32 = jnp.float32
BF16 = jnp.bfloat16
I32 = jnp.int32
U32 = jnp.uint32

D_MODEL = 2048
N_META = 16
LRU_WIDTH = 1024
LRU_HEADS = 16
LRU_HEAD_DIM = LRU_WIDTH // LRU_HEADS
CONV_WIDTH = 4
LRU_C = 8.0
POOL_WIDTH = D_MODEL - LRU_WIDTH
POOL_WINDOWS = (2, 4, 8, 16)
POOL_GROUP_DIM = POOL_WIDTH // len(POOL_WINDOWS)
POOL_BUF = max(POOL_WINDOWS) - 1
IN_WIDTH = 2 * LRU_WIDTH + POOL_WIDTH
N_EXPERTS = 32
TOP_K = 4
D_FF = D_MODEL
SWIGLU_ALPHA = 1.702
SWIGLU_LIMIT = 7.0
RMS_EPS = 1e-6

V7X_MXU_DIM = 256
SUBLANES = 8
BF16_ROWS = 16

PROMPT_TILE = 512
TIME_TILE = 256
MOE_UNIT = 128
MOE_TM_UNITS = 8
MOE_TM = MOE_TM_UNITS * MOE_UNIT
MOE_GROUP_UNITS = 11
MOE_GROUP_ROWS = MOE_UNIT * MOE_GROUP_UNITS
MOE_TILE_STEP = 64
MOE_WHOLE_GROUP_ROWS = (1088, 1152, 1216, 1280, 1344, 1408)
MOE_TF = V7X_MXU_DIM
MOE_W_SPLIT = 2
PERM_CHUNK = V7X_MXU_DIM
ROUTE_CHUNK = 512


def _params(vmem_mb, sem):
    return pltpu.CompilerParams(dimension_semantics=sem, vmem_limit_bytes=vmem_mb << 20)


def _rmsnorm(x, g):
    return x * lax.rsqrt(jnp.mean(x * x, axis=-1, keepdims=True) + RMS_EPS) * g


def _resident(shape):
    return pl.BlockSpec(shape, lambda *_: (0,) * len(shape), pipeline_mode=pl.Buffered(1))


def _proj_kernel(x_ref, g_ref, w_ref, o_ref):
    xn = _rmsnorm(x_ref[...], g_ref[...])
    o_ref[...] = jnp.dot(xn.astype(BF16), w_ref[...], preferred_element_type=F32)


def _proj(x, g, w_bf16, tile):
    rows = x.shape[0]
    return pl.pallas_call(
        _proj_kernel,
        grid=(rows // tile,),
        in_specs=[pl.BlockSpec((tile, D_MODEL), lambda i: (i, 0)),
                  _resident((1, D_MODEL)), _resident((D_MODEL, IN_WIDTH))],
        out_specs=pl.BlockSpec((tile, IN_WIDTH), lambda i: (i, 0)),
        out_shape=jax.ShapeDtypeStruct((rows, IN_WIDTH), F32),
        compiler_params=_params(48, ("arbitrary",)),
        name="proj",
    )(x, g, w_bf16)


def _lru_coeffs(xc, wa_ref, ba_ref, wx_ref, bx_ref, lam_ref):
    xcb = xc.astype(BF16)
    nq = LRU_WIDTH // V7X_MXU_DIM
    rs, gs = [], []
    for q in range(nq):
        blk = xcb[:, q * V7X_MXU_DIM:(q + 1) * V7X_MXU_DIM]
        rs.append(jnp.dot(blk, wa_ref[q], preferred_element_type=F32))
        gs.append(jnp.dot(blk, wx_ref[q], preferred_element_type=F32))
    r = jax.nn.sigmoid(jnp.concatenate(rs, axis=-1) + ba_ref[...])
    i = jax.nn.sigmoid(jnp.concatenate(gs, axis=-1) + bx_ref[...])
    log_a = -LRU_C * r * jax.nn.softplus(-lam_ref[...])
    a = jnp.exp(log_a)
    th = jnp.tanh(log_a)
    beta = jnp.sqrt(-2.0 * th / (1.0 - th))
    return a, beta * i * xc


def _pool_project(wins, u, inv_cnt, pw_ref, ps_ref):
    outs = []
    for g in range(len(POOL_WINDOWS)):
        sl = slice(g * POOL_GROUP_DIM, (g + 1) * POOL_GROUP_DIM)
        d = (wins[g] * inv_cnt[g] - u[:, sl]).astype(BF16)
        outs.append(jnp.dot(d, pw_ref[g], preferred_element_type=F32))
    return jnp.concatenate(outs, axis=-1) * ps_ref[...]


def _mix_seq_kernel(x_ref, g1_ref, win_ref, conv0_ref, h0_ref, pool0_ref,
                    cw_ref, cb_ref, wa_ref, ba_ref, wx_ref, bx_ref, lam_ref, pw_ref, ps_ref,
                    yab_ref, ht_ref, ctail_ref, ptail_ref, cext, pext, hcar, *, tt, start):
    t = pl.program_id(1)
    hist = 2 * SUBLANES

    @pl.when(t == 0)
    def _():
        cext[0:SUBLANES, :] = conv0_ref[...]
        pext[0:hist, :] = pool0_ref[...]
        hcar[...] = h0_ref[...]

    xnb = _rmsnorm(x_ref[...], g1_ref[...]).astype(BF16)
    xa = jnp.dot(xnb, win_ref[:, :LRU_WIDTH], preferred_element_type=F32)
    ga = jnp.dot(xnb, win_ref[:, LRU_WIDTH:2 * LRU_WIDTH], preferred_element_type=F32)
    u = jnp.dot(xnb, win_ref[:, 2 * LRU_WIDTH:], preferred_element_type=F32)

    cext[SUBLANES:SUBLANES + tt, :] = xa
    cw = cw_ref[...]
    xc = cb_ref[...] + cw[CONV_WIDTH - 1:CONV_WIDTH] * xa
    for k in range(1, CONV_WIDTH):
        xc = xc + cw[CONV_WIDTH - 1 - k:CONV_WIDTH - k] * cext[SUBLANES - k:SUBLANES - k + tt, :]
    cext[0:SUBLANES, :] = cext[tt:tt + SUBLANES, :]
    ctail_ref[...] = cext[0:SUBLANES, :]

    a, b = _lru_coeffs(xc, wa_ref, ba_ref, wx_ref, bx_ref, lam_ref)

    groups = tt // SUBLANES
    a = a.reshape(groups, SUBLANES, LRU_WIDTH)
    b = b.reshape(groups, SUBLANES, LRU_WIDTH)
    sub = lax.broadcasted_iota(I32, (groups, SUBLANES, LRU_WIDTH), 1)
    s = 1
    while s < SUBLANES:
        keep = sub >= s
        a_prev = jnp.where(keep, pltpu.roll(a, s, 1), 1.0)
        b_prev = jnp.where(keep, pltpu.roll(b, s, 1), 0.0)
        b = b + a * b_prev
        a = a * a_prev
        s *= 2
    h = hcar[SUBLANES - 1:SUBLANES, :]
    hrows = []
    for g in range(groups):
        hg = a[g] * h + b[g]
        hrows.append(hg)
        h = hg[SUBLANES - 1:SUBLANES, :]
    hs = jnp.concatenate(hrows, axis=0)
    hcar[...] = hrows[-1]
    ht_ref[...] = hrows[-1]
    ya = hs * jax.nn.gelu(ga)

    pext[hist:hist + tt, :] = u
    e = pext[...]
    wins = []
    shift = 1
    for g in range(len(POOL_WINDOWS)):
        e = e + pltpu.roll(e, shift, 0)
        wins.append(e[hist:hist + tt, :POOL_GROUP_DIM])
        if g + 1 < len(POOL_WINDOWS):
            e = e[:, POOL_GROUP_DIM:]
        shift *= 2
    pext[0:hist, :] = pext[tt:tt + hist, :]
    ptail_ref[...] = pext[0:hist, :]

    if start >= POOL_BUF:
        inv_cnt = [1.0 / w for w in POOL_WINDOWS]
    else:
        pos = start + t * tt + lax.broadcasted_iota(I32, (tt, 1), 0)
        inv_cnt = [1.0 / jnp.minimum(w, pos + 1).astype(F32) for w in POOL_WINDOWS]
    yb = _pool_project(wins, u, inv_cnt, pw_ref, ps_ref)

    yab_ref[:, :LRU_WIDTH] = ya.astype(BF16)
    yab_ref[:, LRU_WIDTH:] = yb.astype(BF16)


def _mix_seq(x, g1, w_in_bf16, batch, seq, tt, start, conv0, h0, pool0, mixw):
    nt = seq // tt
    hist = 2 * SUBLANES
    rows = lambda w: pl.BlockSpec((tt, w), lambda b, t: (b * nt + t, 0))
    state = lambda r: pl.BlockSpec((None, r, LRU_WIDTH), lambda b, t: (b, 0, 0))
    tail = lambda r: jax.ShapeDtypeStruct((batch, r, LRU_WIDTH), F32)
    return pl.pallas_call(
        functools.partial(_mix_seq_kernel, tt=tt, start=start),
        grid=(batch, nt),
        in_specs=[rows(D_MODEL), _resident((1, D_MODEL)), _resident((D_MODEL, IN_WIDTH)),
                  state(SUBLANES), state(SUBLANES), state(hist)] + [_resident(w.shape) for w in mixw],
        out_specs=[rows(D_MODEL), state(SUBLANES), state(SUBLANES), state(hist)],
        out_shape=[jax.ShapeDtypeStruct((batch * seq, D_MODEL), BF16), tail(SUBLANES), tail(SUBLANES), tail(hist)],
        scratch_shapes=[pltpu.VMEM((tt + SUBLANES, LRU_WIDTH), F32),
                        pltpu.VMEM((tt + hist, POOL_WIDTH), F32),
                        pltpu.VMEM((SUBLANES, LRU_WIDTH), F32)],
        compiler_params=_params(56, ("arbitrary", "arbitrary")),
        name="mix_seq",
    )(x, g1, w_in_bf16, conv0, h0, pool0, *mixw)


def _mix_step_kernel(xa_ref, ga_ref, ub_ref, sconv_ref, sh_ref, spool_ref,
                     cw_ref, cb_ref, wa_ref, ba_ref, wx_ref, bx_ref, lam_ref, pw_ref, ps_ref,
                     yab_ref, h_ref):
    xa = xa_ref[...]
    cw = cw_ref[...]
    xc = cb_ref[...] + cw[CONV_WIDTH - 1:CONV_WIDTH] * xa
    for k in range(1, CONV_WIDTH):
        xc = xc + cw[CONV_WIDTH - 1 - k:CONV_WIDTH - k] * sconv_ref[CONV_WIDTH - 1 - k]
    a, b = _lru_coeffs(xc, wa_ref, ba_ref, wx_ref, bx_ref, lam_ref)
    h = a * sh_ref[...] + b
    h_ref[...] = h
    ya = h * jax.nn.gelu(ga_ref[...])

    u = ub_ref[...]
    wins = []
    for g, w in enumerate(POOL_WINDOWS):
        sl = slice(g * POOL_GROUP_DIM, (g + 1) * POOL_GROUP_DIM)
        acc = u[:, sl]
        for k in range(1, w):
            acc = acc + spool_ref[POOL_BUF - k, :, sl]
        wins.append(acc)
    yb = _pool_project(wins, u, [1.0 / w for w in POOL_WINDOWS], pw_ref, ps_ref)
    yab_ref[:, :LRU_WIDTH] = ya.astype(BF16)
    yab_ref[:, LRU_WIDTH:] = yb.astype(BF16)


def _mix_step(proj, sconv_t, sh, spool_t, mixw):
    rows = proj.shape[0]
    col = lambda c: pl.BlockSpec((rows, LRU_WIDTH), lambda i: (0, c))
    full = lambda a: pl.BlockSpec(a.shape, lambda i: (0,) * a.ndim)
    return pl.pallas_call(
        _mix_step_kernel,
        grid=(1,),
        in_specs=[col(0), col(1), col(2), full(sconv_t), full(sh), full(spool_t)] + [full(w) for w in mixw],
        out_specs=[pl.BlockSpec((rows, D_MODEL), lambda i: (0, 0)),
                   pl.BlockSpec((rows, LRU_WIDTH), lambda i: (0, 0))],
        out_shape=[jax.ShapeDtypeStruct((rows, D_MODEL), BF16),
                   jax.ShapeDtypeStruct((rows, LRU_WIDTH), F32)],
        compiler_params=_params(48, ("arbitrary",)),
        name="mix_step",
    )(proj, proj, proj, sconv_t, sh, spool_t, *mixw)


def _out_kernel(yab_ref, x_ref, wo_ref, g2_ref, rw_both_ref, rw_hi_ref, rb_ref,
                h1_ref, xn_ref, eid_ref, wgt_ref, *, tile):
    h1_ref[...] = x_ref[...] + jnp.dot(yab_ref[...], wo_ref[...], preferred_element_type=F32)

    chunk = min(ROUTE_CHUNK, tile)

    def route(c, carry):
        rows = pl.ds(pl.multiple_of(c * chunk, chunk), chunk)
        xn = _rmsnorm(h1_ref[rows, :], g2_ref[...])
        xn_hi = xn.astype(BF16)
        xn_ref[rows, :] = xn_hi
        xn_lo = (xn - xn_hi.astype(F32)).astype(BF16)
        nt_dims = (((1,), (1,)), ((), ()))
        both = lax.dot_general(rw_both_ref[...], xn_hi, nt_dims, preferred_element_type=F32)
        logits = (both[:N_EXPERTS] + both[N_EXPERTS:]
                  + lax.dot_general(rw_hi_ref[...], xn_lo, nt_dims, preferred_element_type=F32)) + rb_ref[...]
        eidx = lax.broadcasted_iota(I32, (N_EXPERTS, chunk), 0)
        vals, ids = [], []
        for _ in range(TOP_K):
            m = jnp.max(logits, axis=0, keepdims=True)
            idx = jnp.min(jnp.where(logits == m, eidx, N_EXPERTS), axis=0, keepdims=True)
            vals.append(m)
            ids.append(idx)
            logits = jnp.where(eidx == idx, -jnp.inf, logits)
        ex = [jnp.exp(v - vals[0]) for v in vals]
        tot = ex[0]
        for e_ in ex[1:]:
            tot = tot + e_
        eid_ref[:, rows] = jnp.concatenate(ids, axis=0)
        wgt_ref[:, rows] = jnp.concatenate([e_ / tot for e_ in ex], axis=0)
        return carry
    lax.fori_loop(0, tile // chunk, route, 0)


def _out(yab, x, wo_bf16, g2, rw, rb, tile):
    rw_hi = rw.T.astype(BF16)
    rw_lo = (rw.T - rw_hi.astype(F32)).astype(BF16)
    rw_both = jnp.concatenate([rw_hi, rw_lo], axis=0)
    n = x.shape[0]
    rowblk = lambda w: pl.BlockSpec((tile, w), lambda i: (i, 0))
    colblk = pl.BlockSpec((TOP_K, tile), lambda i: (0, i))
    return pl.pallas_call(
        functools.partial(_out_kernel, tile=tile),
        grid=(n // tile,),
        in_specs=[rowblk(D_MODEL), rowblk(D_MODEL), _resident((D_MODEL, D_MODEL)), _resident((1, D_MODEL)),
                  _resident((2 * N_EXPERTS, D_MODEL)), _resident((N_EXPERTS, D_MODEL)), _resident((N_EXPERTS, 1))],
        out_specs=[rowblk(D_MODEL), rowblk(D_MODEL), colblk, colblk],
        out_shape=[jax.ShapeDtypeStruct((n, D_MODEL), F32), jax.ShapeDtypeStruct((n, D_MODEL), BF16),
                   jax.ShapeDtypeStruct((TOP_K, n), I32), jax.ShapeDtypeStruct((TOP_K, n), F32)],
        compiler_params=_params(48, ("arbitrary",)),
        name="out_router",
    )(yab, x, wo_bf16, g2, rw_both, rw_hi, rb)


def _for_segments(tile, seg_len_ref, max_len, fn):
    def per_expert(e, c):
        s = tile * N_EXPERTS + e
        length = seg_len_ref[s]
        size = max_len
        while size >= BF16_ROWS:
            @pl.when((length // size) % 2 == 1)
            def _(size=size):
                fn(s, length // (2 * size) * (2 * size), size)
            size //= 2
        return c
    lax.fori_loop(0, N_EXPERTS, per_expert, 0)


def _seg_rows(ref, start, rows):
    return ref.at[pl.ds(pl.multiple_of(start, BF16_ROWS), rows)]


def _wait_rows(total, max_rows, copy_of_rows):
    size = 1 << (max_rows.bit_length() - 1)
    while size >= BF16_ROWS:
        @pl.when((total // size) % 2 == 1)
        def _(size=size):
            copy_of_rows(size).wait()
        size //= 2


def _dispatch_kernel(seg_loc_ref, seg_dst_ref, seg_len_ref, tile_rows_ref, tail_dst_ref, tail_len_ref,
                     xn_ref, dest_ref, *rest, tile0, kt, zero_tails):
    xs_hbm, cb, zb, sem, zsem = rest[-5:]
    i = pl.program_id(0)
    n_steps = pl.num_programs(0)
    slot = i % 2
    tile = tile0 + i
    tt = xn_ref.shape[0]

    def seg_copy(sl, s, off, rows):
        return pltpu.make_async_copy(_seg_rows(cb.at[sl], seg_loc_ref[s] + off, rows),
                                     _seg_rows(xs_hbm, seg_dst_ref[s] + off, rows), sem.at[sl])

    def start_segments(tl, sl):
        _for_segments(tl, seg_len_ref, tt, lambda s, off, rows: seg_copy(sl, s, off, rows).start())

    def wait_segments(tl, sl):
        _wait_rows(tile_rows_ref[tl], kt, lambda rows: pltpu.make_async_copy(
            cb.at[sl, pl.ds(0, rows)], xs_hbm.at[pl.ds(0, rows)], sem.at[sl]))

    @pl.when(i >= 2)
    def _():
        wait_segments(tile - 2, slot)

    x = xn_ref[...]
    d = [dest_ref[k:k + 1, :] for k in range(TOP_K)]
    for c0 in range(0, kt, PERM_CHUNK):
        r = c0 + lax.broadcasted_iota(I32, (PERM_CHUNK, tt), 0)
        p = jnp.zeros((PERM_CHUNK, tt), F32)
        for k in range(TOP_K):
            p = jnp.where(d[k] == r, 1.0, p)
        rows = jnp.dot(p.astype(BF16), x, preferred_element_type=F32)
        cb[slot, c0:c0 + PERM_CHUNK, :] = rows.astype(BF16)

    start_segments(tile, slot)

    @pl.when(i == n_steps - 1)
    def _():
        @pl.when(i >= 1)
        def _():
            wait_segments(tile - 1, 1 - slot)
        wait_segments(tile, slot)

    if zero_tails:
        zb[...] = jnp.zeros(zb.shape, BF16)

        def tail_copy(e, j):
            return pltpu.make_async_copy(zb, _seg_rows(xs_hbm, tail_dst_ref[e] + j * BF16_ROWS, BF16_ROWS),
                                         zsem.at[0])

        def tails(method):
            def per_expert(e, c):
                def per_chunk(j, c2):
                    getattr(tail_copy(e, j), method)()
                    return c2
                lax.fori_loop(0, tail_len_ref[e] // BF16_ROWS, per_chunk, 0)
                return c
            lax.fori_loop(0, N_EXPERTS, per_expert, 0)

        @pl.when(i == n_steps - 1)
        def _():
            tails("start")
            tails("wait")


def _dispatch(tables, xn, dest, xs, tile0, tt, kt, xs_rows, zero_tails):
    n = xn.shape[0]
    in_specs = [pl.BlockSpec((tt, D_MODEL), lambda i, *_: (i, 0)),
                pl.BlockSpec((TOP_K, tt), lambda i, *_: (0, i))]
    args = [*tables, xn, dest]
    aliases = {}
    if xs is not None:
        in_specs.append(pl.BlockSpec(memory_space=pl.ANY))
        aliases = {len(args): 0}
        args.append(xs)
    grid_spec = pltpu.PrefetchScalarGridSpec(
        num_scalar_prefetch=len(tables),
        grid=(n // tt,),
        in_specs=in_specs,
        out_specs=pl.BlockSpec(memory_space=pl.ANY),
        scratch_shapes=[pltpu.VMEM((2, kt, D_MODEL), BF16), pltpu.VMEM((BF16_ROWS, D_MODEL), BF16),
                        pltpu.SemaphoreType.DMA((2,)), pltpu.SemaphoreType.DMA((1,))],
    )
    return pl.pallas_call(
        functools.partial(_dispatch_kernel, tile0=tile0, kt=kt, zero_tails=zero_tails),
        grid_spec=grid_spec,
        out_shape=jax.ShapeDtypeStruct((xs_rows, D_MODEL), BF16),
        input_output_aliases=aliases,
        compiler_params=_params(48, ("arbitrary",)),
        name="dispatch",
    )(*args)


def _moe_kernel(ge_ref, grow_ref, gnu_ref, gtile_ref, ng_ref,
                xs_hbm, wg_hbm, wu_hbm, wd_hbm, bg_ref, bu_ref, bd_ref,
                ys_hbm, xb, acc, ypk, wgf, wuf, wdf, xsem, wsem, osem, *, n_f):
    g = pl.program_id(0)
    n_groups = ng_ref[0]
    xslot = g % 2

    def unit_rows(j):
        return pl.ds(pl.multiple_of(j * MOE_UNIT, MOE_UNIT), MOE_UNIT)

    def hbm_unit(ref, grp, j):
        return ref.at[pl.ds(pl.multiple_of(grow_ref[grp] + j * MOE_UNIT, MOE_UNIT), MOE_UNIT)]

    def copy_in(grp, sl, j):
        return pltpu.make_async_copy(hbm_unit(xs_hbm, grp, j), xb.at[sl, unit_rows(j)], xsem.at[sl])

    def copy_out(grp, j):
        return pltpu.make_async_copy(ypk.at[unit_rows(j)], hbm_unit(ys_hbm, grp, j), osem.at[0])

    def for_units(grp, fn):
        def body(j, c):
            fn(grp, j)
            return c
        lax.fori_loop(0, gnu_ref[grp], body, 0)

    def weight_copies(grp, f, ws):
        e = ge_ref[grp]
        cols = pl.ds(pl.multiple_of(f * MOE_TF, MOE_TF), MOE_TF)
        copies = []
        for p in range(MOE_W_SPLIT):
            band = pl.ds(p * (D_MODEL // MOE_W_SPLIT), D_MODEL // MOE_W_SPLIT)
            sub = pl.ds(p * (MOE_TF // MOE_W_SPLIT), MOE_TF // MOE_W_SPLIT)
            down = pl.ds(pl.multiple_of(f * MOE_TF + p * (MOE_TF // MOE_W_SPLIT), MOE_TF // MOE_W_SPLIT),
                         MOE_TF // MOE_W_SPLIT)
            copies += [
                pltpu.make_async_copy(wg_hbm.at[e, band, cols], wgf.at[ws, band], wsem.at[ws, 3 * p]),
                pltpu.make_async_copy(wu_hbm.at[e, band, cols], wuf.at[ws, band], wsem.at[ws, 3 * p + 1]),
                pltpu.make_async_copy(wd_hbm.at[e, down, :], wdf.at[ws, sub], wsem.at[ws, 3 * p + 2]),
            ]
        return copies

    @pl.when(g < n_groups)
    def _():
        @pl.when(g == 0)
        def _():
            for_units(0, lambda grp, j: copy_in(grp, 0, j).start())
            for cp in weight_copies(0, 0, 0):
                cp.start()
        for_units(g, lambda grp, j: copy_in(grp, xslot, j).wait())

        @pl.when(g + 1 < n_groups)
        def _():
            for_units(g + 1, lambda grp, j: copy_in(grp, 1 - xslot, j).start())

        def init(grp, j):
            acc[unit_rows(j), :] = jnp.broadcast_to(bd_ref[...], (MOE_UNIT, D_MODEL))
        for_units(g, init)
        n_units = gnu_ref[g]
        tile_rows = gtile_ref[g]

        def chunk(f, c):
            ws = f % 2
            for cp in weight_copies(g, f, ws):
                cp.wait()

            @pl.when(f + 1 < n_f)
            def _():
                for cp in weight_copies(g, f + 1, 1 - ws):
                    cp.start()

            @pl.when(jnp.logical_and(f + 1 == n_f, g + 1 < n_groups))
            def _():
                for cp in weight_copies(g + 1, 0, 1 - ws):
                    cp.start()

            bg = bg_ref[pl.ds(f, 1), :]
            bu = bu_ref[pl.ds(f, 1), :]

            def ffn(start, size):
                x = xb[xslot, pl.ds(start, size), :]
                gg = jnp.dot(x, wgf[ws].astype(BF16), preferred_element_type=F32) + bg
                uu = jnp.dot(x, wuf[ws].astype(BF16), preferred_element_type=F32) + bu
                gg = jnp.minimum(gg, SWIGLU_LIMIT)
                uu = jnp.clip(uu, -SWIGLU_LIMIT, SWIGLU_LIMIT)
                hdn = gg * jax.nn.sigmoid(SWIGLU_ALPHA * gg) * (uu + 1.0)
                acc[pl.ds(start, size), :] += jnp.dot(hdn.astype(BF16), wdf[ws].astype(BF16),
                                                      preferred_element_type=F32)

            whole = jnp.bool_(False)
            for rows in MOE_WHOLE_GROUP_ROWS:
                whole = jnp.logical_or(whole, tile_rows == rows)

                @pl.when(tile_rows == rows)
                def _(rows=rows):
                    ffn(0, rows)

            @pl.when(jnp.logical_not(whole))
            def _():
                def full_tile(j, c2):
                    ffn(pl.multiple_of(j * MOE_TM, MOE_TM), MOE_TM)
                    return c2
                lax.fori_loop(0, n_units // MOE_TM_UNITS, full_tile, 0)
                part = MOE_TM_UNITS // 2
                while part >= 1:
                    @pl.when((n_units // part) % 2 == 1)
                    def _(part=part):
                        done = n_units // (2 * part) * (2 * part)
                        ffn(pl.multiple_of(done * MOE_UNIT, MOE_UNIT), part * MOE_UNIT)
                    part //= 2
            return c
        lax.fori_loop(0, n_f, chunk, 0)

        @pl.when(g > 0)
        def _():
            for_units(g - 1, lambda grp, j: copy_out(grp, j).wait())

        def finish(grp, j):
            ypk[unit_rows(j), :] = acc[unit_rows(j), :].astype(BF16)
            copy_out(grp, j).start()
        for_units(g, finish)

    @pl.when(g == pl.num_programs(0) - 1)
    def _():
        for_units(n_groups - 1, lambda grp, j: copy_out(grp, j).wait())


def _moe(ge, grow, gnu, gtile, ng, xs, wg, wu, wd, bg, bu, bd, n_groups_max):
    n_f = D_FF // MOE_TF
    assert n_f % 2 == 0
    per_expert = lambda shape: pl.BlockSpec((None,) + shape, lambda g, ge, *_: (ge[g], 0, 0))
    any_spec = pl.BlockSpec(memory_space=pl.ANY)
    grid_spec = pltpu.PrefetchScalarGridSpec(
        num_scalar_prefetch=5,
        grid=(n_groups_max,),
        in_specs=[any_spec, any_spec, any_spec, any_spec,
                  per_expert((n_f, MOE_TF)), per_expert((n_f, MOE_TF)), per_expert((1, D_MODEL))],
        out_specs=any_spec,
        scratch_shapes=[
            pltpu.VMEM((2, MOE_GROUP_ROWS, D_MODEL), BF16),
            pltpu.VMEM((MOE_GROUP_ROWS, D_MODEL), F32),
            pltpu.VMEM((MOE_GROUP_ROWS, D_MODEL), BF16),
            pltpu.VMEM((2, D_MODEL, MOE_TF), F32),
            pltpu.VMEM((2, D_MODEL, MOE_TF), F32),
            pltpu.VMEM((2, MOE_TF, D_MODEL), F32),
            pltpu.SemaphoreType.DMA((2,)),
            pltpu.SemaphoreType.DMA((2, 3 * MOE_W_SPLIT)),
            pltpu.SemaphoreType.DMA((1,)),
        ],
    )
    return pl.pallas_call(
        functools.partial(_moe_kernel, n_f=n_f),
        grid_spec=grid_spec,
        out_shape=jax.ShapeDtypeStruct(xs.shape, BF16),
        compiler_params=_params(56, ("arbitrary",)),
        name="moe_ffn",
    )(ge, grow, gnu, gtile, ng, xs, wg, wu, wd,
      bg.reshape(N_EXPERTS, n_f, MOE_TF), bu.reshape(N_EXPERTS, n_f, MOE_TF), bd.reshape(N_EXPERTS, 1, D_MODEL))


def _combine_kernel(seg_loc_ref, seg_src_ref, seg_len_ref, tile_rows_ref,
                    ys_hbm, h1_ref, dest_ref, w_ref, fn_ref, y_ref, sb, wbuf, sem, *, tile0, kt):
    i = pl.program_id(0)
    n_steps = pl.num_programs(0)
    slot = i % 2
    tile = tile0 + i
    tt = h1_ref.shape[0]

    def seg_copy(sl, s, off, rows):
        return pltpu.make_async_copy(_seg_rows(ys_hbm, seg_src_ref[s] + off, rows),
                                     _seg_rows(sb.at[sl], seg_loc_ref[s] + off, rows), sem.at[sl])

    def start_segments(tl, sl):
        _for_segments(tl, seg_len_ref, tt, lambda s, off, rows: seg_copy(sl, s, off, rows).start())

    @pl.when(i == 0)
    def _():
        sb[...] = jnp.zeros(sb.shape, BF16)
        start_segments(tile, slot)

    @pl.when(i + 1 < n_steps)
    def _():
        start_segments(tile + 1, 1 - slot)

    _wait_rows(tile_rows_ref[tile], kt, lambda rows: pltpu.make_async_copy(
        ys_hbm.at[pl.ds(0, rows)], sb.at[slot, pl.ds(0, rows)], sem.at[slot]))

    acc = h1_ref[...]
    d = [jnp.broadcast_to(dest_ref[:, k:k + 1], (tt, PERM_CHUNK)) for k in range(TOP_K)]
    w = [jnp.broadcast_to(w_ref[:, k:k + 1], (tt, PERM_CHUNK)) for k in range(TOP_K)]
    half = kt // 2
    for h in range(2):
        for c0 in range(0, half, PERM_CHUNK):
            col = h * half + c0 + lax.broadcasted_iota(I32, (tt, PERM_CHUNK), 1)
            wm = jnp.zeros((tt, PERM_CHUNK), F32)
            for k in range(TOP_K):
                wm = jnp.where(d[k] == col, w[k], wm)
            wbuf[h, :, c0:c0 + PERM_CHUNK] = wm.astype(BF16)
        acc = acc + jnp.dot(wbuf[h], sb[slot, h * half:(h + 1) * half, :], preferred_element_type=F32)
    y_ref[...] = _rmsnorm(acc, fn_ref[...])


def _combine(tables, ys, h1, dest_t, wgt_t, fn, tile0, tt, kt):
    n = h1.shape[0]
    grid_spec = pltpu.PrefetchScalarGridSpec(
        num_scalar_prefetch=len(tables),
        grid=(n // tt,),
        in_specs=[pl.BlockSpec(memory_space=pl.ANY),
                  pl.BlockSpec((tt, D_MODEL), lambda i, *_: (i, 0)),
                  pl.BlockSpec((tt, TOP_K), lambda i, *_: (i, 0)),
                  pl.BlockSpec((tt, TOP_K), lambda i, *_: (i, 0)),
                  pl.BlockSpec((1, D_MODEL), lambda i, *_: (0, 0))],
        out_specs=pl.BlockSpec((tt, D_MODEL), lambda i, *_: (i, 0)),
        scratch_shapes=[pltpu.VMEM((2, kt, D_MODEL), BF16), pltpu.VMEM((2, tt, kt // 2), BF16),
                        pltpu.SemaphoreType.DMA((2,))],
    )
    return pl.pallas_call(
        functools.partial(_combine_kernel, tile0=tile0, kt=kt),
        grid_spec=grid_spec,
        out_shape=jax.ShapeDtypeStruct((n, D_MODEL), F32),
        compiler_params=_params(56, ("arbitrary",)),
        name="combine",
    )(*tables, ys, h1, dest_t, wgt_t, fn)


def _ceil_to(x, m):
    return (x + m - 1) // m * m


def _routing_tables(eid, tile_lens, n_groups_max):
    n = eid.shape[1]
    nt = len(tile_lens)
    starts = np.concatenate([[0], np.cumsum(tile_lens)[:-1]]).astype(np.int64)
    onehot = (eid[:, :, None] == jnp.arange(N_EXPERTS, dtype=I32)[None, None, :]).astype(I32)
    member = jnp.sum(onehot, axis=0)
    incl = jnp.cumsum(member, axis=0)
    before = incl - member
    runs = []
    for t, l in enumerate(tile_lens):
        if runs and runs[-1][3] == l:
            runs[-1][2] += 1
        else:
            runs.append([t, int(starts[t]), 1, int(l)])
    rank0 = jnp.concatenate([before[s0:s0 + c * l:l] for _, s0, c, l in runs])
    cnt = jnp.concatenate([incl[s0 + l - 1:s0 + c * l:l] for _, s0, c, l in runs]) - rank0
    c8 = _ceil_to(cnt, BF16_ROWS)
    seg_loc = jnp.cumsum(c8, axis=1) - c8
    rows_e = jnp.sum(c8, axis=0)
    region_e = _ceil_to(rows_e, MOE_UNIT)
    start_e = jnp.cumsum(region_e) - region_e
    seg_pos = start_e[None, :] + jnp.cumsum(c8, axis=0) - c8

    def per_token(tab):
        return jnp.concatenate([jnp.broadcast_to(tab[t0:t0 + c, None, :], (c, l, N_EXPERTS)).reshape(c * l, N_EXPERTS)
                                for t0, _, c, l in runs])
    base = per_token(seg_loc - rank0) + before
    dest = jnp.sum(onehot * base[None], axis=2).astype(I32)

    ntiles_e = region_e // MOE_UNIT
    groups_e = (ntiles_e + MOE_GROUP_UNITS - 1) // MOE_GROUP_UNITS
    gend_e = jnp.cumsum(groups_e)
    gstart_e = gend_e - groups_e
    n_groups = gend_e[-1]
    j = jnp.arange(n_groups_max, dtype=I32)
    j_act = jnp.minimum(j, n_groups - 1)
    e_j = jnp.minimum(jnp.sum((gend_e[None, :] <= j_act[:, None]).astype(I32), axis=1), N_EXPERTS - 1)
    sel = (e_j[:, None] == jnp.arange(N_EXPERTS, dtype=I32)[None, :]).astype(I32)
    pick = lambda v: jnp.sum(sel * v[None, :], axis=1)
    local = j_act - pick(gstart_e)
    grow = pick(start_e) + local * MOE_GROUP_ROWS
    gnt = jnp.where(j < n_groups, jnp.clip(pick(ntiles_e) - local * MOE_GROUP_UNITS, 0, MOE_GROUP_UNITS), 0)
    grows = jnp.where(j < n_groups, jnp.clip(pick(rows_e) - local * MOE_GROUP_ROWS, 0, MOE_GROUP_ROWS), 0)
    gtile = _ceil_to(grows, MOE_TILE_STEP)

    flat = lambda a: a.reshape(-1).astype(I32)
    seg_tabs = (flat(seg_loc), flat(seg_pos), flat(c8), jnp.sum(c8, axis=1).astype(I32))
    tail_tabs = ((start_e + rows_e).astype(I32), (region_e - rows_e).astype(I32))
    group_tabs = (e_j.astype(I32), grow.astype(I32), gnt.astype(I32), gtile.astype(I32),
                  n_groups.reshape(1).astype(I32))
    return dest, seg_tabs, tail_tabs, group_tabs


def _block_diag(w, per_block):
    h, d, _ = w.shape
    nb = h // per_block
    eye = jnp.eye(per_block, dtype=w.dtype)
    w4 = w.reshape(nb, per_block, d, d)
    out = jnp.einsum("bpij,pq->bpiqj", w4, eye)
    return out.reshape(nb, per_block * d, per_block * d)


def kernel(x_prompt, x_sample, state_conv, state_rglru, state_pool, meta_tokens, norm1, w_in, conv_w, conv_b, lru_wa, lru_ba, lru_wx, lru_bx, lru_lambda, pool_w, pool_scale, w_out, norm2, router_w, router_b, exp_wg, exp_bg, exp_wu, exp_bu, exp_wd, exp_bd, final_norm):
    batch, seq, _ = x_prompt.shape
    dec = x_sample.shape[0]
    n_prompt = batch * seq
    n_tok = n_prompt + dec
    l = 0
    row = lambda v: v.reshape(1, -1)

    xp = x_prompt.reshape(n_prompt, D_MODEL)
    xs_tok = x_sample.reshape(dec, D_MODEL)
    w_in_b = w_in[l].astype(BF16)
    w_out_b = w_out[l].astype(BF16)
    heads_per_block = V7X_MXU_DIM // LRU_HEAD_DIM
    mixw = (conv_w[l], row(conv_b[l]),
            _block_diag(lru_wa[l], heads_per_block).astype(BF16), row(lru_ba[l]),
            _block_diag(lru_wx[l], heads_per_block).astype(BF16), row(lru_bx[l]),
            row(lru_lambda[l]), pool_w[l].astype(BF16), row(pool_scale[l]))
    g1 = row(norm1[l])

    zeros = lambda r: jnp.zeros((1, r, LRU_WIDTH), F32)
    _, h_meta, conv_meta, pool_meta = _mix_seq(meta_tokens, g1, w_in_b, 1, N_META, N_META, 0,
                                               zeros(SUBLANES), zeros(SUBLANES), zeros(2 * SUBLANES), mixw)
    per_seq = lambda v: jnp.broadcast_to(v, (batch,) + v.shape[1:])
    yab_p, h_p, conv_tail, pool_tail = _mix_seq(xp, g1, w_in_b, batch, seq, TIME_TILE, N_META,
                                                per_seq(conv_meta), per_seq(h_meta), per_seq(pool_meta), mixw)
    proj_s = _proj(xs_tok, g1, w_in_b, dec)
    yab_s, h_s = _mix_step(proj_s, jnp.swapaxes(state_conv[l], 0, 1), state_rglru[l],
                           jnp.swapaxes(state_pool[l], 0, 1), mixw)

    out_w = (w_out_b, row(norm2[l]), router_w[l], router_b[l].reshape(N_EXPERTS, 1))
    h1_p, xn_p, eid_p, wgt_p = _out(yab_p, xp, *out_w, PROMPT_TILE)
    h1_s, xn_s, eid_s, wgt_s = _out(yab_s, xs_tok, *out_w, dec)

    n_ptiles = n_prompt // PROMPT_TILE
    tile_lens = [PROMPT_TILE] * n_ptiles + [dec]
    seg_pad = (BF16_ROWS - 1) * N_EXPERTS
    kt_p = _ceil_to(TOP_K * PROMPT_TILE + seg_pad, PERM_CHUNK)
    kt_s = _ceil_to(TOP_K * dec + seg_pad, PERM_CHUNK)
    units_max = (TOP_K * n_tok + len(tile_lens) * seg_pad) // MOE_UNIT + N_EXPERTS
    xs_rows = units_max * MOE_UNIT
    n_groups_max = (units_max + N_EXPERTS * (MOE_GROUP_UNITS - 1)) // MOE_GROUP_UNITS

    eid = jnp.concatenate([eid_p, eid_s], axis=1)
    dest, seg_tabs, tail_tabs, group_tabs = _routing_tables(eid, tile_lens, n_groups_max)
    disp_tabs = seg_tabs + tail_tabs
    dest_p, dest_s = dest[:, :n_prompt], dest[:, n_prompt:]

    xs = _dispatch(disp_tabs, xn_p, dest_p, None, 0, PROMPT_TILE, kt_p, xs_rows, False)
    xs = _dispatch(disp_tabs, xn_s, dest_s, xs, n_ptiles, dec, kt_s, xs_rows, True)
    ys = _moe(*group_tabs, xs, exp_wg[l], exp_wu[l], exp_wd[l], exp_bg[l], exp_bu[l], exp_bd[l], n_groups_max)
    fn = row(final_norm)
    y_p = _combine(seg_tabs, ys, h1_p, dest_p.T, wgt_p.T, fn, 0, PROMPT_TILE, kt_p)
    y_s = _combine(seg_tabs, ys, h1_s, dest_s.T, wgt_s.T, fn, n_ptiles, dec, kt_s)

    conv_p = conv_tail[:, SUBLANES - (CONV_WIDTH - 1):]
    pool_p = pool_tail[:, 2 * SUBLANES - POOL_BUF:]
    conv_s = jnp.concatenate([state_conv[l][:, 1:], proj_s[:, None, :LRU_WIDTH]], axis=1)
    pool_s = jnp.concatenate([state_pool[l][:, 1:], proj_s[:, None, 2 * LRU_WIDTH:]], axis=1)
    return (y_p.reshape(batch, seq, D_MODEL), y_s.reshape(dec, 1, D_MODEL),
            conv_p[None], h_p[None, :, SUBLANES - 1], pool_p[None],
            conv_s[None], h_s[None], pool_s[None])
```

```python
import functools

import numpy as np
import jax
import jax.numpy as jnp
from jax import lax
from jax.experimental import pallas as pl
from jax.experimental.pallas import tpu as pltpu

F32 = jnp.float32
BF16 = jnp.bfloat16
I32 = jnp.int32
U32 = jnp.uint32

D_MODEL = 2048
N_META = 16
LRU_WIDTH = 1024
LRU_HEADS = 16
LRU_HEAD_DIM = LRU_WIDTH // LRU_HEADS
CONV_WIDTH = 4
LRU_C = 8.0
POOL_WIDTH = D_MODEL - LRU_WIDTH
POOL_WINDOWS = (2, 4, 8, 16)
POOL_GROUP_DIM = POOL_WIDTH // len(POOL_WINDOWS)
POOL_BUF = max(POOL_WINDOWS) - 1
IN_WIDTH = 2 * LRU_WIDTH + POOL_WIDTH
N_EXPERTS = 32
TOP_K = 4
D_FF = D_MODEL
SWIGLU_ALPHA = 1.702
SWIGLU_LIMIT = 7.0
RMS_EPS = 1e-6

V7X_MXU_DIM = 256
SUBLANES = 8
BF16_ROWS = 16

PROMPT_TILE = 512
TIME_TILE = 256
MOE_UNIT = 128
MOE_TM_UNITS = 8
MOE_TM = MOE_TM_UNITS * MOE_UNIT
MOE_GROUP_UNITS = 11
MOE_GROUP_ROWS = MOE_UNIT * MOE_GROUP_UNITS
MOE_TILE_STEP = 64
MOE_WHOLE_GROUP_ROWS = (1088, 1152, 1216, 1280, 1344, 1408)
MOE_TF = V7X_MXU_DIM
MOE_W_SPLIT = 1
PERM_CHUNK = V7X_MXU_DIM
ROUTE_CHUNK = 512


def _params(vmem_mb, sem):
    return pltpu.CompilerParams(dimension_semantics=sem, vmem_limit_bytes=vmem_mb << 20)


def _rmsnorm(x, g):
    return x * lax.rsqrt(jnp.mean(x * x, axis=-1, keepdims=True) + RMS_EPS) * g


def _resident(shape):
    return pl.BlockSpec(shape, lambda *_: (0,) * len(shape), pipeline_mode=pl.Buffered(1))


def _proj_kernel(x_ref, g_ref, w_ref, o_ref):
    xn = _rmsnorm(x_ref[...], g_ref[...])
    o_ref[...] = jnp.dot(xn.astype(BF16), w_ref[...], preferred_element_type=F32)


def _proj(x, g, w_bf16, tile):
    rows = x.shape[0]
    return pl.pallas_call(
        _proj_kernel,
        grid=(rows // tile,),
        in_specs=[pl.BlockSpec((tile, D_MODEL), lambda i: (i, 0)),
                  _resident((1, D_MODEL)), _resident((D_MODEL, IN_WIDTH))],
        out_specs=pl.BlockSpec((tile, IN_WIDTH), lambda i: (i, 0)),
        out_shape=jax.ShapeDtypeStruct((rows, IN_WIDTH), F32),
        compiler_params=_params(48, ("arbitrary",)),
        name="proj",
    )(x, g, w_bf16)


def _lru_coeffs(xc, wa_ref, ba_ref, wx_ref, bx_ref, lam_ref):
    xcb = xc.astype(BF16)
    nq = LRU_WIDTH // V7X_MXU_DIM
    rs, gs = [], []
    for q in range(nq):
        blk = xcb[:, q * V7X_MXU_DIM:(q + 1) * V7X_MXU_DIM]
        rs.append(jnp.dot(blk, wa_ref[q], preferred_element_type=F32))
        gs.append(jnp.dot(blk, wx_ref[q], preferred_element_type=F32))
    r = jax.nn.sigmoid(jnp.concatenate(rs, axis=-1) + ba_ref[...])
    i = jax.nn.sigmoid(jnp.concatenate(gs, axis=-1) + bx_ref[...])
    log_a = -LRU_C * r * jax.nn.softplus(-lam_ref[...])
    a = jnp.exp(log_a)
    th = jnp.tanh(log_a)
    beta = jnp.sqrt(-2.0 * th / (1.0 - th))
    return a, beta * i * xc


def _pool_project(wins, u, inv_cnt, pw_ref, ps_ref):
    outs = []
    for g in range(len(POOL_WINDOWS)):
        sl = slice(g * POOL_GROUP_DIM, (g + 1) * POOL_GROUP_DIM)
        d = (wins[g] * inv_cnt[g] - u[:, sl]).astype(BF16)
        outs.append(jnp.dot(d, pw_ref[g], preferred_element_type=F32))
    return jnp.concatenate(outs, axis=-1) * ps_ref[...]


def _mix_seq_kernel(x_ref, g1_ref, win_ref, conv0_ref, h0_ref, pool0_ref,
                    cw_ref, cb_ref, wa_ref, ba_ref, wx_ref, bx_ref, lam_ref, pw_ref, ps_ref,
                    yab_ref, ht_ref, ctail_ref, ptail_ref, cext, pext, hcar, *, tt, start):
    t = pl.program_id(1)
    hist = 2 * SUBLANES

    @pl.when(t == 0)
    def _():
        cext[0:SUBLANES, :] = conv0_ref[...]
        pext[0:hist, :] = pool0_ref[...]
        hcar[...] = h0_ref[...]

    xnb = _rmsnorm(x_ref[...], g1_ref[...]).astype(BF16)
    xa = jnp.dot(xnb, win_ref[:, :LRU_WIDTH], preferred_element_type=F32)

    cext[SUBLANES:SUBLANES + tt, :] = xa
    cw = cw_ref[...]
    xc = cb_ref[...] + cw[CONV_WIDTH - 1:CONV_WIDTH] * xa
    for k in range(1, CONV_WIDTH):
        xc = xc + cw[CONV_WIDTH - 1 - k:CONV_WIDTH - k] * cext[SUBLANES - k:SUBLANES - k + tt, :]
    cext[0:SUBLANES, :] = cext[tt:tt + SUBLANES, :]
    ctail_ref[...] = cext[0:SUBLANES, :]

    a, b = _lru_coeffs(xc, wa_ref, ba_ref, wx_ref, bx_ref, lam_ref)

    groups = tt // SUBLANES
    a = a.reshape(groups, SUBLANES, LRU_WIDTH)
    b = b.reshape(groups, SUBLANES, LRU_WIDTH)
    sub = lax.broadcasted_iota(I32, (groups, SUBLANES, LRU_WIDTH), 1)
    s = 1
    while s < SUBLANES:
        keep = sub >= s
        a_prev = jnp.where(keep, pltpu.roll(a, s, 1), 1.0)
        b_prev = jnp.where(keep, pltpu.roll(b, s, 1), 0.0)
        b = b + a * b_prev
        a = a * a_prev
        s *= 2
    h = hcar[SUBLANES - 1:SUBLANES, :]
    hrows = []
    for g in range(groups):
        hg = a[g] * h + b[g]
        hrows.append(hg)
        h = hg[SUBLANES - 1:SUBLANES, :]
    hs = jnp.concatenate(hrows, axis=0)
    hcar[...] = hrows[-1]
    ht_ref[...] = hrows[-1]
    ga = jnp.dot(xnb, win_ref[:, LRU_WIDTH:2 * LRU_WIDTH], preferred_element_type=F32)
    ya = hs * jax.nn.gelu(ga)

    u = jnp.dot(xnb, win_ref[:, 2 * LRU_WIDTH:], preferred_element_type=F32)
    pext[hist:hist + tt, :] = u
    e = pext[...]
    wins = []
    shift = 1
    for g in range(len(POOL_WINDOWS)):
        e = e + pltpu.roll(e, shift, 0)
        wins.append(e[hist:hist + tt, :POOL_GROUP_DIM])
        if g + 1 < len(POOL_WINDOWS):
            e = e[:, POOL_GROUP_DIM:]
        shift *= 2
    pext[0:hist, :] = pext[tt:tt + hist, :]
    ptail_ref[...] = pext[0:hist, :]

    if start >= POOL_BUF:
        inv_cnt = [1.0 / w for w in POOL_WINDOWS]
    else:
        pos = start + t * tt + lax.broadcasted_iota(I32, (tt, 1), 0)
        inv_cnt = [1.0 / jnp.minimum(w, pos + 1).astype(F32) for w in POOL_WINDOWS]
    yb = _pool_project(wins, u, inv_cnt, pw_ref, ps_ref)

    yab_ref[:, :LRU_WIDTH] = ya.astype(BF16)
    yab_ref[:, LRU_WIDTH:] = yb.astype(BF16)


def _mix_seq(x, g1, w_in_bf16, batch, seq, tt, start, conv0, h0, pool0, mixw):
    nt = seq // tt
    hist = 2 * SUBLANES
    rows = lambda w: pl.BlockSpec((tt, w), lambda b, t: (b * nt + t, 0))
    state = lambda r: pl.BlockSpec((None, r, LRU_WIDTH), lambda b, t: (b, 0, 0))
    tail = lambda r: jax.ShapeDtypeStruct((batch, r, LRU_WIDTH), F32)
    return pl.pallas_call(
        functools.partial(_mix_seq_kernel, tt=tt, start=start),
        grid=(batch, nt),
        in_specs=[rows(D_MODEL), _resident((1, D_MODEL)), _resident((D_MODEL, IN_WIDTH)),
                  state(SUBLANES), state(SUBLANES), state(hist)] + [_resident(w.shape) for w in mixw],
        out_specs=[rows(D_MODEL), state(SUBLANES), state(SUBLANES), state(hist)],
        out_shape=[jax.ShapeDtypeStruct((batch * seq, D_MODEL), BF16), tail(SUBLANES), tail(SUBLANES), tail(hist)],
        scratch_shapes=[pltpu.VMEM((tt + SUBLANES, LRU_WIDTH), F32),
                        pltpu.VMEM((tt + hist, POOL_WIDTH), F32),
                        pltpu.VMEM((SUBLANES, LRU_WIDTH), F32)],
        compiler_params=_params(56, ("arbitrary", "arbitrary")),
        name="mix_seq",
    )(x, g1, w_in_bf16, conv0, h0, pool0, *mixw)


def _mix_step_kernel(xa_ref, ga_ref, ub_ref, sconv_ref, sh_ref, spool_ref,
                     cw_ref, cb_ref, wa_ref, ba_ref, wx_ref, bx_ref, lam_ref, pw_ref, ps_ref,
                     yab_ref, h_ref):
    xa = xa_ref[...]
    cw = cw_ref[...]
    xc = cb_ref[...] + cw[CONV_WIDTH - 1:CONV_WIDTH] * xa
    for k in range(1, CONV_WIDTH):
        xc = xc + cw[CONV_WIDTH - 1 - k:CONV_WIDTH - k] * sconv_ref[CONV_WIDTH - 1 - k]
    a, b = _lru_coeffs(xc, wa_ref, ba_ref, wx_ref, bx_ref, lam_ref)
    h = a * sh_ref[...] + b
    h_ref[...] = h
    ya = h * jax.nn.gelu(ga_ref[...])

    u = ub_ref[...]
    wins = []
    for g, w in enumerate(POOL_WINDOWS):
        sl = slice(g * POOL_GROUP_DIM, (g + 1) * POOL_GROUP_DIM)
        acc = u[:, sl]
        for k in range(1, w):
            acc = acc + spool_ref[POOL_BUF - k, :, sl]
        wins.append(acc)
    yb = _pool_project(wins, u, [1.0 / w for w in POOL_WINDOWS], pw_ref, ps_ref)
    yab_ref[:, :LRU_WIDTH] = ya.astype(BF16)
    yab_ref[:, LRU_WIDTH:] = yb.astype(BF16)


def _mix_step(proj, sconv_t, sh, spool_t, mixw):
    rows = proj.shape[0]
    col = lambda c: pl.BlockSpec((rows, LRU_WIDTH), lambda i: (0, c))
    full = lambda a: pl.BlockSpec(a.shape, lambda i: (0,) * a.ndim)
    return pl.pallas_call(
        _mix_step_kernel,
        grid=(1,),
        in_specs=[col(0), col(1), col(2), full(sconv_t), full(sh), full(spool_t)] + [full(w) for w in mixw],
        out_specs=[pl.BlockSpec((rows, D_MODEL), lambda i: (0, 0)),
                   pl.BlockSpec((rows, LRU_WIDTH), lambda i: (0, 0))],
        out_shape=[jax.ShapeDtypeStruct((rows, D_MODEL), BF16),
                   jax.ShapeDtypeStruct((rows, LRU_WIDTH), F32)],
        compiler_params=_params(48, ("arbitrary",)),
        name="mix_step",
    )(proj, proj, proj, sconv_t, sh, spool_t, *mixw)


def _out_kernel(yab_ref, x_ref, wo_ref, g2_ref, rw_both_ref, rw_hi_ref, rb_ref,
                h1_ref, xn_ref, eid_ref, wgt_ref, *, tile):
    h1_ref[...] = x_ref[...] + jnp.dot(yab_ref[...], wo_ref[...], preferred_element_type=F32)

    chunk = min(ROUTE_CHUNK, tile)

    def route(c, carry):
        rows = pl.ds(pl.multiple_of(c * chunk, chunk), chunk)
        xn = _rmsnorm(h1_ref[rows, :], g2_ref[...])
        xn_hi = xn.astype(BF16)
        xn_ref[rows, :] = xn_hi
        xn_lo = (xn - xn_hi.astype(F32)).astype(BF16)
        nt_dims = (((1,), (1,)), ((), ()))
        both = lax.dot_general(rw_both_ref[...], xn_hi, nt_dims, preferred_element_type=F32)
        logits = (both[:N_EXPERTS] + both[N_EXPERTS:]
                  + lax.dot_general(rw_hi_ref[...], xn_lo, nt_dims, preferred_element_type=F32)) + rb_ref[...]
        eidx = lax.broadcasted_iota(I32, (N_EXPERTS, chunk), 0)
        vals, ids = [], []
        for _ in range(TOP_K):
            m = jnp.max(logits, axis=0, keepdims=True)
            idx = jnp.min(jnp.where(logits == m, eidx, N_EXPERTS), axis=0, keepdims=True)
            vals.append(m)
            ids.append(idx)
            logits = jnp.where(eidx == idx, -jnp.inf, logits)
        ex = [jnp.exp(v - vals[0]) for v in vals]
        tot = ex[0]
        for e_ in ex[1:]:
            tot = tot + e_
        eid_ref[:, rows] = jnp.concatenate(ids, axis=0)
        wgt_ref[:, rows] = jnp.concatenate([e_ / tot for e_ in ex], axis=0)
        return carry
    lax.fori_loop(0, tile // chunk, route, 0)


def _out(yab, x, wo_bf16, g2, rw, rb, tile):
    rw_hi = rw.T.astype(BF16)
    rw_lo = (rw.T - rw_hi.astype(F32)).astype(BF16)
    rw_both = jnp.concatenate([rw_hi, rw_lo], axis=0)
    n = x.shape[0]
    rowblk = lambda w: pl.BlockSpec((tile, w), lambda i: (i, 0))
    colblk = pl.BlockSpec((TOP_K, tile), lambda i: (0, i))
    return pl.pallas_call(
        functools.partial(_out_kernel, tile=tile),
        grid=(n // tile,),
        in_specs=[rowblk(D_MODEL), rowblk(D_MODEL), _resident((D_MODEL, D_MODEL)), _resident((1, D_MODEL)),
                  _resident((2 * N_EXPERTS, D_MODEL)), _resident((N_EXPERTS, D_MODEL)), _resident((N_EXPERTS, 1))],
        out_specs=[rowblk(D_MODEL), rowblk(D_MODEL), colblk, colblk],
        out_shape=[jax.ShapeDtypeStruct((n, D_MODEL), F32), jax.ShapeDtypeStruct((n, D_MODEL), BF16),
                   jax.ShapeDtypeStruct((TOP_K, n), I32), jax.ShapeDtypeStruct((TOP_K, n), F32)],
        compiler_params=_params(48, ("arbitrary",)),
        name="out_router",
    )(yab, x, wo_bf16, g2, rw_both, rw_hi, rb)


def _for_segments(tile, seg_len_ref, max_len, fn):
    def per_expert(e, c):
        s = tile * N_EXPERTS + e
        length = seg_len_ref[s]
        size = max_len
        while size >= BF16_ROWS:
            @pl.when((length // size) % 2 == 1)
            def _(size=size):
                fn(s, length // (2 * size) * (2 * size), size)
            size //= 2
        return c
    lax.fori_loop(0, N_EXPERTS, per_expert, 0)


def _seg_rows(ref, start, rows):
    return ref.at[pl.ds(pl.multiple_of(start, BF16_ROWS), rows)]


def _wait_rows(total, max_rows, copy_of_rows):
    size = 1 << (max_rows.bit_length() - 1)
    while size >= BF16_ROWS:
        @pl.when((total // size) % 2 == 1)
        def _(size=size):
            copy_of_rows(size).wait()
        size //= 2


def _dispatch_kernel(seg_loc_ref, seg_dst_ref, seg_len_ref, tile_rows_ref, tail_dst_ref, tail_len_ref,
                     xn_ref, dest_ref, *rest, tile0, kt, zero_tails):
    xs_hbm, cb, zb, sem, zsem = rest[-5:]
    i = pl.program_id(0)
    n_steps = pl.num_programs(0)
    slot = i % 2
    tile = tile0 + i
    tt = xn_ref.shape[0]

    def seg_copy(sl, s, off, rows):
        return pltpu.make_async_copy(_seg_rows(cb.at[sl], seg_loc_ref[s] + off, rows),
                                     _seg_rows(xs_hbm, seg_dst_ref[s] + off, rows), sem.at[sl])

    def start_segments(tl, sl):
        _for_segments(tl, seg_len_ref, tt, lambda s, off, rows: seg_copy(sl, s, off, rows).start())

    def wait_segments(tl, sl):
        _wait_rows(tile_rows_ref[tl], kt, lambda rows: pltpu.make_async_copy(
            cb.at[sl, pl.ds(0, rows)], xs_hbm.at[pl.ds(0, rows)], sem.at[sl]))

    @pl.when(i >= 2)
    def _():
        wait_segments(tile - 2, slot)

    x = xn_ref[...]
    d = [dest_ref[k:k + 1, :] for k in range(TOP_K)]
    for c0 in range(0, kt, PERM_CHUNK):
        r = c0 + lax.broadcasted_iota(I32, (PERM_CHUNK, tt), 0)
        p = jnp.zeros((PERM_CHUNK, tt), F32)
        for k in range(TOP_K):
            p = jnp.where(d[k] == r, 1.0, p)
        rows = jnp.dot(p.astype(BF16), x, preferred_element_type=F32)
        cb[slot, c0:c0 + PERM_CHUNK, :] = rows.astype(BF16)

    start_segments(tile, slot)

    @pl.when(i == n_steps - 1)
    def _():
        @pl.when(i >= 1)
        def _():
            wait_segments(tile - 1, 1 - slot)
        wait_segments(tile, slot)

    if zero_tails:
        zb[...] = jnp.zeros(zb.shape, BF16)

        def tail_copy(e, j):
            return pltpu.make_async_copy(zb, _seg_rows(xs_hbm, tail_dst_ref[e] + j * BF16_ROWS, BF16_ROWS),
                                         zsem.at[0])

        def tails(method):
            def per_expert(e, c):
                def per_chunk(j, c2):
                    getattr(tail_copy(e, j), method)()
                    return c2
                lax.fori_loop(0, tail_len_ref[e] // BF16_ROWS, per_chunk, 0)
                return c
            lax.fori_loop(0, N_EXPERTS, per_expert, 0)

        @pl.when(i == n_steps - 1)
        def _():
            tails("start")
            tails("wait")


def _dispatch(tables, xn, dest, xs, tile0, tt, kt, xs_rows, zero_tails):
    n = xn.shape[0]
    in_specs = [pl.BlockSpec((tt, D_MODEL), lambda i, *_: (i, 0)),
                pl.BlockSpec((TOP_K, tt), lambda i, *_: (0, i))]
    args = [*tables, xn, dest]
    aliases = {}
    if xs is not None:
        in_specs.append(pl.BlockSpec(memory_space=pl.ANY))
        aliases = {len(args): 0}
        args.append(xs)
    grid_spec = pltpu.PrefetchScalarGridSpec(
        num_scalar_prefetch=len(tables),
        grid=(n // tt,),
        in_specs=in_specs,
        out_specs=pl.BlockSpec(memory_space=pl.ANY),
        scratch_shapes=[pltpu.VMEM((2, kt, D_MODEL), BF16), pltpu.VMEM((BF16_ROWS, D_MODEL), BF16),
                        pltpu.SemaphoreType.DMA((2,)), pltpu.SemaphoreType.DMA((1,))],
    )
    return pl.pallas_call(
        functools.partial(_dispatch_kernel, tile0=tile0, kt=kt, zero_tails=zero_tails),
        grid_spec=grid_spec,
        out_shape=jax.ShapeDtypeStruct((xs_rows, D_MODEL), BF16),
        input_output_aliases=aliases,
        compiler_params=_params(48, ("arbitrary",)),
        name="dispatch",
    )(*args)


def _moe_kernel(ge_ref, grow_ref, gnu_ref, gtile_ref, ng_ref,
                xs_hbm, wg_hbm, wu_hbm, wd_hbm, bg_ref, bu_ref, bd_ref,
                ys_hbm, xb, acc, ypk, wgf, wuf, wdf, xsem, wsem, osem, *, n_f):
    g = pl.program_id(0)
    n_groups = ng_ref[0]
    xslot = g % 2

    def unit_rows(j):
        return pl.ds(pl.multiple_of(j * MOE_UNIT, MOE_UNIT), MOE_UNIT)

    def hbm_unit(ref, grp, j):
        return ref.at[pl.ds(pl.multiple_of(grow_ref[grp] + j * MOE_UNIT, MOE_UNIT), MOE_UNIT)]

    def copy_in(grp, sl, j):
        return pltpu.make_async_copy(hbm_unit(xs_hbm, grp, j), xb.at[sl, unit_rows(j)], xsem.at[sl])

    def copy_out(grp, j):
        return pltpu.make_async_copy(ypk.at[unit_rows(j)], hbm_unit(ys_hbm, grp, j), osem.at[0])

    def for_units(grp, fn):
        def body(j, c):
            fn(grp, j)
            return c
        lax.fori_loop(0, gnu_ref[grp], body, 0)

    def weight_copies(grp, f, ws):
        e = ge_ref[grp]
        cols = pl.ds(pl.multiple_of(f * MOE_TF, MOE_TF), MOE_TF)
        copies = []
        for p in range(MOE_W_SPLIT):
            band = pl.ds(p * (D_MODEL // MOE_W_SPLIT), D_MODEL // MOE_W_SPLIT)
            sub = pl.ds(p * (MOE_TF // MOE_W_SPLIT), MOE_TF // MOE_W_SPLIT)
            down = pl.ds(pl.multiple_of(f * MOE_TF + p * (MOE_TF // MOE_W_SPLIT), MOE_TF // MOE_W_SPLIT),
                         MOE_TF // MOE_W_SPLIT)
            copies += [
                pltpu.make_async_copy(wg_hbm.at[e, band, cols], wgf.at[ws, band], wsem.at[ws, 3 * p]),
                pltpu.make_async_copy(wu_hbm.at[e, band, cols], wuf.at[ws, band], wsem.at[ws, 3 * p + 1]),
                pltpu.make_async_copy(wd_hbm.at[e, down, :], wdf.at[ws, sub], wsem.at[ws, 3 * p + 2]),
            ]
        return copies

    @pl.when(g < n_groups)
    def _():
        @pl.when(g == 0)
        def _():
            for_units(0, lambda grp, j: copy_in(grp, 0, j).start())
            for cp in weight_copies(0, 0, 0):
                cp.start()
        for_units(g, lambda grp, j: copy_in(grp, xslot, j).wait())

        @pl.when(g + 1 < n_groups)
        def _():
            for_units(g + 1, lambda grp, j: copy_in(grp, 1 - xslot, j).start())

        def init(grp, j):
            acc[unit_rows(j), :] = jnp.broadcast_to(bd_ref[...], (MOE_UNIT, D_MODEL))
        for_units(g, init)
        n_units = gnu_ref[g]
        tile_rows = gtile_ref[g]

        def chunk(f, c):
            ws = f % 2
            for cp in weight_copies(g, f, ws):
                cp.wait()

            @pl.when(f + 1 < n_f)
            def _():
                for cp in weight_copies(g, f + 1, 1 - ws):
                    cp.start()

            @pl.when(jnp.logical_and(f + 1 == n_f, g + 1 < n_groups))
            def _():
                for cp in weight_copies(g + 1, 0, 1 - ws):
                    cp.start()

            bg = bg_ref[pl.ds(f, 1), :]
            bu = bu_ref[pl.ds(f, 1), :]

            def ffn(start, size):
                x = xb[xslot, pl.ds(start, size), :]
                gg = jnp.dot(x, wgf[ws].astype(BF16), preferred_element_type=F32) + bg
                uu = jnp.dot(x, wuf[ws].astype(BF16), preferred_element_type=F32) + bu
                gg = jnp.minimum(gg, SWIGLU_LIMIT)
                uu = jnp.clip(uu, -SWIGLU_LIMIT, SWIGLU_LIMIT)
                hdn = gg * jax.nn.sigmoid(SWIGLU_ALPHA * gg) * (uu + 1.0)
                acc[pl.ds(start, size), :] += jnp.dot(hdn.astype(BF16), wdf[ws].astype(BF16),
                                                      preferred_element_type=F32)

            whole = jnp.bool_(False)
            for rows in MOE_WHOLE_GROUP_ROWS:
                whole = jnp.logical_or(whole, tile_rows == rows)

                @pl.when(tile_rows == rows)
                def _(rows=rows):
                    ffn(0, rows)

            @pl.when(jnp.logical_not(whole))
            def _():
                def full_tile(j, c2):
                    ffn(pl.multiple_of(j * MOE_TM, MOE_TM), MOE_TM)
                    return c2
                lax.fori_loop(0, n_units // MOE_TM_UNITS, full_tile, 0)
                part = MOE_TM_UNITS // 2
                while part >= 1:
                    @pl.when((n_units // part) % 2 == 1)
                    def _(part=part):
                        done = n_units // (2 * part) * (2 * part)
                        ffn(pl.multiple_of(done * MOE_UNIT, MOE_UNIT), part * MOE_UNIT)
                    part //= 2
            return c
        lax.fori_loop(0, n_f, chunk, 0)

        @pl.when(g > 0)
        def _():
            for_units(g - 1, lambda grp, j: copy_out(grp, j).wait())

        def finish(grp, j):
            ypk[unit_rows(j), :] = acc[unit_rows(j), :].astype(BF16)
            copy_out(grp, j).start()
        for_units(g, finish)

    @pl.when(g == pl.num_programs(0) - 1)
    def _():
        for_units(n_groups - 1, lambda grp, j: copy_out(grp, j).wait())


def _moe(ge, grow, gnu, gtile, ng, xs, wg, wu, wd, bg, bu, bd, n_groups_max):
    n_f = D_FF // MOE_TF
    assert n_f % 2 == 0
    per_expert = lambda shape: pl.BlockSpec((None,) + shape, lambda g, ge, *_: (ge[g], 0, 0))
    any_spec = pl.BlockSpec(memory_space=pl.ANY)
    grid_spec = pltpu.PrefetchScalarGridSpec(
        num_scalar_prefetch=5,
        grid=(n_groups_max,),
        in_specs=[any_spec, any_spec, any_spec, any_spec,
                  per_expert((n_f, MOE_TF)), per_expert((n_f, MOE_TF)), per_expert((1, D_MODEL))],
        out_specs=any_spec,
        scratch_shapes=[
            pltpu.VMEM((2, MOE_GROUP_ROWS, D_MODEL), BF16),
            pltpu.VMEM((MOE_GROUP_ROWS, D_MODEL), F32),
            pltpu.VMEM((MOE_GROUP_ROWS, D_MODEL), BF16),
            pltpu.VMEM((2, D_MODEL, MOE_TF), F32),
            pltpu.VMEM((2, D_MODEL, MOE_TF), F32),
            pltpu.VMEM((2, MOE_TF, D_MODEL), F32),
            pltpu.SemaphoreType.DMA((2,)),
            pltpu.SemaphoreType.DMA((2, 3 * MOE_W_SPLIT)),
            pltpu.SemaphoreType.DMA((1,)),
        ],
    )
    return pl.pallas_call(
        functools.partial(_moe_kernel, n_f=n_f),
        grid_spec=grid_spec,
        out_shape=jax.ShapeDtypeStruct(xs.shape, BF16),
        compiler_params=_params(56, ("arbitrary",)),
        name="moe_ffn",
    )(ge, grow, gnu, gtile, ng, xs, wg, wu, wd,
      bg.reshape(N_EXPERTS, n_f, MOE_TF), bu.reshape(N_EXPERTS, n_f, MOE_TF), bd.reshape(N_EXPERTS, 1, D_MODEL))


def _combine_kernel(seg_loc_ref, seg_src_ref, seg_len_ref, tile_rows_ref,
                    ys_hbm, h1_ref, dest_ref, w_ref, fn_ref, y_ref, sb, wbuf, sem, *, tile0, kt):
    i = pl.program_id(0)
    n_steps = pl.num_programs(0)
    slot = i % 2
    tile = tile0 + i
    tt = h1_ref.shape[0]

    def seg_copy(sl, s, off, rows):
        return pltpu.make_async_copy(_seg_rows(ys_hbm, seg_src_ref[s] + off, rows),
                                     _seg_rows(sb.at[sl], seg_loc_ref[s] + off, rows), sem.at[sl])

    def start_segments(tl, sl):
        _for_segments(tl, seg_len_ref, tt, lambda s, off, rows: seg_copy(sl, s, off, rows).start())

    @pl.when(i == 0)
    def _():
        sb[...] = jnp.zeros(sb.shape, BF16)
        start_segments(tile, slot)

    @pl.when(i + 1 < n_steps)
    def _():
        start_segments(tile + 1, 1 - slot)

    _wait_rows(tile_rows_ref[tile], kt, lambda rows: pltpu.make_async_copy(
        ys_hbm.at[pl.ds(0, rows)], sb.at[slot, pl.ds(0, rows)], sem.at[slot]))

    acc = h1_ref[...]
    d = [jnp.broadcast_to(dest_ref[:, k:k + 1], (tt, PERM_CHUNK)) for k in range(TOP_K)]
    w = [jnp.broadcast_to(w_ref[:, k:k + 1], (tt, PERM_CHUNK)) for k in range(TOP_K)]
    half = kt // 2
    for h in range(2):
        for c0 in range(0, half, PERM_CHUNK):
            col = h * half + c0 + lax.broadcasted_iota(I32, (tt, PERM_CHUNK), 1)
            wm = jnp.zeros((tt, PERM_CHUNK), F32)
            for k in range(TOP_K):
                wm = jnp.where(d[k] == col, w[k], wm)
            wbuf[h, :, c0:c0 + PERM_CHUNK] = wm.astype(BF16)
        acc = acc + jnp.dot(wbuf[h], sb[slot, h * half:(h + 1) * half, :], preferred_element_type=F32)
    y_ref[...] = _rmsnorm(acc, fn_ref[...])


def _combine(tables, ys, h1, dest_t, wgt_t, fn, tile0, tt, kt):
    n = h1.shape[0]
    grid_spec = pltpu.PrefetchScalarGridSpec(
        num_scalar_prefetch=len(tables),
        grid=(n // tt,),
        in_specs=[pl.BlockSpec(memory_space=pl.ANY),
                  pl.BlockSpec((tt, D_MODEL), lambda i, *_: (i, 0)),
                  pl.BlockSpec((tt, TOP_K), lambda i, *_: (i, 0)),
                  pl.BlockSpec((tt, TOP_K), lambda i, *_: (i, 0)),
                  pl.BlockSpec((1, D_MODEL), lambda i, *_: (0, 0))],
        out_specs=pl.BlockSpec((tt, D_MODEL), lambda i, *_: (i, 0)),
        scratch_shapes=[pltpu.VMEM((2, kt, D_MODEL), BF16), pltpu.VMEM((2, tt, kt // 2), BF16),
                        pltpu.SemaphoreType.DMA((2,))],
    )
    return pl.pallas_call(
        functools.partial(_combine_kernel, tile0=tile0, kt=kt),
        grid_spec=grid_spec,
        out_shape=jax.ShapeDtypeStruct((n, D_MODEL), F32),
        compiler_params=_params(56, ("arbitrary",)),
        name="combine",
    )(*tables, ys, h1, dest_t, wgt_t, fn)


def _ceil_to(x, m):
    return (x + m - 1) // m * m


def _routing_tables(eid, tile_lens, n_groups_max):
    n = eid.shape[1]
    nt = len(tile_lens)
    starts = np.concatenate([[0], np.cumsum(tile_lens)[:-1]]).astype(np.int64)
    onehot = (eid[:, :, None] == jnp.arange(N_EXPERTS, dtype=I32)[None, None, :]).astype(I32)
    member = jnp.sum(onehot, axis=0)
    incl = jnp.cumsum(member, axis=0)
    before = incl - member
    runs = []
    for t, l in enumerate(tile_lens):
        if runs and runs[-1][3] == l:
            runs[-1][2] += 1
        else:
            runs.append([t, int(starts[t]), 1, int(l)])
    rank0 = jnp.concatenate([before[s0:s0 + c * l:l] for _, s0, c, l in runs])
    cnt = jnp.concatenate([incl[s0 + l - 1:s0 + c * l:l] for _, s0, c, l in runs]) - rank0
    c8 = _ceil_to(cnt, BF16_ROWS)
    seg_loc = jnp.cumsum(c8, axis=1) - c8
    rows_e = jnp.sum(c8, axis=0)
    region_e = _ceil_to(rows_e, MOE_UNIT)
    start_e = jnp.cumsum(region_e) - region_e
    seg_pos = start_e[None, :] + jnp.cumsum(c8, axis=0) - c8

    def per_token(tab):
        return jnp.concatenate([jnp.broadcast_to(tab[t0:t0 + c, None, :], (c, l, N_EXPERTS)).reshape(c * l, N_EXPERTS)
                                for t0, _, c, l in runs])
    base = per_token(seg_loc - rank0) + before
    dest = jnp.sum(onehot * base[None], axis=2).astype(I32)

    ntiles_e = region_e // MOE_UNIT
    groups_e = (ntiles_e + MOE_GROUP_UNITS - 1) // MOE_GROUP_UNITS
    gend_e = jnp.cumsum(groups_e)
    gstart_e = gend_e - groups_e
    n_groups = gend_e[-1]
    j = jnp.arange(n_groups_max, dtype=I32)
    j_act = jnp.minimum(j, n_groups - 1)
    e_j = jnp.minimum(jnp.sum((gend_e[None, :] <= j_act[:, None]).astype(I32), axis=1), N_EXPERTS - 1)
    sel = (e_j[:, None] == jnp.arange(N_EXPERTS, dtype=I32)[None, :]).astype(I32)
    pick = lambda v: jnp.sum(sel * v[None, :], axis=1)
    local = j_act - pick(gstart_e)
    grow = pick(start_e) + local * MOE_GROUP_ROWS
    gnt = jnp.where(j < n_groups, jnp.clip(pick(ntiles_e) - local * MOE_GROUP_UNITS, 0, MOE_GROUP_UNITS), 0)
    grows = jnp.where(j < n_groups, jnp.clip(pick(rows_e) - local * MOE_GROUP_ROWS, 0, MOE_GROUP_ROWS), 0)
    gtile = _ceil_to(grows, MOE_TILE_STEP)

    flat = lambda a: a.reshape(-1).astype(I32)
    seg_tabs = (flat(seg_loc), flat(seg_pos), flat(c8), jnp.sum(c8, axis=1).astype(I32))
    tail_tabs = ((start_e + rows_e).astype(I32), (region_e - rows_e).astype(I32))
    group_tabs = (e_j.astype(I32), grow.astype(I32), gnt.astype(I32), gtile.astype(I32),
                  n_groups.reshape(1).astype(I32))
    return dest, seg_tabs, tail_tabs, group_tabs


def _block_diag(w, per_block):
    h, d, _ = w.shape
    nb = h // per_block
    eye = jnp.eye(per_block, dtype=w.dtype)
    w4 = w.reshape(nb, per_block, d, d)
    out = jnp.einsum("bpij,pq->bpiqj", w4, eye)
    return out.reshape(nb, per_block * d, per_block * d)


def kernel(x_prompt, x_sample, state_conv, state_rglru, state_pool, meta_tokens, norm1, w_in, conv_w, conv_b, lru_wa, lru_ba, lru_wx, lru_bx, lru_lambda, pool_w, pool_scale, w_out, norm2, router_w, router_b, exp_wg, exp_bg, exp_wu, exp_bu, exp_wd, exp_bd, final_norm):
    batch, seq, _ = x_prompt.shape
    dec = x_sample.shape[0]
    n_prompt = batch * seq
    n_tok = n_prompt + dec
    l = 0
    row = lambda v: v.reshape(1, -1)

    xp = x_prompt.reshape(n_prompt, D_MODEL)
    xs_tok = x_sample.reshape(dec, D_MODEL)
    w_in_b = w_in[l].astype(BF16)
    w_out_b = w_out[l].astype(BF16)
    heads_per_block = V7X_MXU_DIM // LRU_HEAD_DIM
    mixw = (conv_w[l], row(conv_b[l]),
            _block_diag(lru_wa[l], heads_per_block).astype(BF16), row(lru_ba[l]),
            _block_diag(lru_wx[l], heads_per_block).astype(BF16), row(lru_bx[l]),
            row(lru_lambda[l]), pool_w[l].astype(BF16), row(pool_scale[l]))
    g1 = row(norm1[l])

    zeros = lambda r: jnp.zeros((1, r, LRU_WIDTH), F32)
    _, h_meta, conv_meta, pool_meta = _mix_seq(meta_tokens, g1, w_in_b, 1, N_META, N_META, 0,
                                               zeros(SUBLANES), zeros(SUBLANES), zeros(2 * SUBLANES), mixw)
    per_seq = lambda v: jnp.broadcast_to(v, (batch,) + v.shape[1:])
    yab_p, h_p, conv_tail, pool_tail = _mix_seq(xp, g1, w_in_b, batch, seq, TIME_TILE, N_META,
                                                per_seq(conv_meta), per_seq(h_meta), per_seq(pool_meta), mixw)
    proj_s = _proj(xs_tok, g1, w_in_b, dec)
    yab_s, h_s = _mix_step(proj_s, jnp.swapaxes(state_conv[l], 0, 1), state_rglru[l],
                           jnp.swapaxes(state_pool[l], 0, 1), mixw)

    out_w = (w_out_b, row(norm2[l]), router_w[l], router_b[l].reshape(N_EXPERTS, 1))
    h1_p, xn_p, eid_p, wgt_p = _out(yab_p, xp, *out_w, PROMPT_TILE)
    h1_s, xn_s, eid_s, wgt_s = _out(yab_s, xs_tok, *out_w, dec)

    n_ptiles = n_prompt // PROMPT_TILE
    tile_lens = [PROMPT_TILE] * n_ptiles + [dec]
    seg_pad = (BF16_ROWS - 1) * N_EXPERTS
    kt_p = _ceil_to(TOP_K * PROMPT_TILE + seg_pad, PERM_CHUNK)
    kt_s = _ceil_to(TOP_K * dec + seg_pad, PERM_CHUNK)
    units_max = (TOP_K * n_tok + len(tile_lens) * seg_pad) // MOE_UNIT + N_EXPERTS
    xs_rows = units_max * MOE_UNIT
    n_groups_max = (units_max + N_EXPERTS * (MOE_GROUP_UNITS - 1)) // MOE_GROUP_UNITS

    eid = jnp.concatenate([eid_p, eid_s], axis=1)
    dest, seg_tabs, tail_tabs, group_tabs = _routing_tables(eid, tile_lens, n_groups_max)
    disp_tabs = seg_tabs + tail_tabs
    dest_p, dest_s = dest[:, :n_prompt], dest[:, n_prompt:]

    xs = _dispatch(disp_tabs, xn_p, dest_p, None, 0, PROMPT_TILE, kt_p, xs_rows, False)
    xs = _dispatch(disp_tabs, xn_s, dest_s, xs, n_ptiles, dec, kt_s, xs_rows, True)
    ys = _moe(*group_tabs, xs, exp_wg[l], exp_wu[l], exp_wd[l], exp_bg[l], exp_bu[l], exp_bd[l], n_groups_max)
    fn = row(final_norm)
    y_p = _combine(seg_tabs, ys, h1_p, dest_p.T, wgt_p.T, fn, 0, PROMPT_TILE, kt_p)
    y_s = _combine(seg_tabs, ys, h1_s, dest_s.T, wgt_s.T, fn, n_ptiles, dec, kt_s)

    conv_p = conv_tail[:, SUBLANES - (CONV_WIDTH - 1):]
    pool_p = pool_tail[:, 2 * SUBLANES - POOL_BUF:]
    conv_s = jnp.concatenate([state_conv[l][:, 1:], proj_s[:, None, :LRU_WIDTH]], axis=1)
    pool_s = jnp.concatenate([state_pool[l][:, 1:], proj_s[:, None, 2 * LRU_WIDTH:]], axis=1)
    return (y_p.reshape(batch, seq, D_MODEL), y_s.reshape(dec, 1, D_MODEL),
            conv_p[None], h_p[None, :, SUBLANES - 1], pool_p[None],
            conv_s[None], h_s[None], pool_s[None])
```

```python
import functools

import numpy as np
import jax
import jax.numpy as jnp
from jax import lax
from jax.experimental import pallas as pl
from jax.experimental.pallas import tpu as pltpu

F32 = jnp.float32
BF16 = jnp.bfloat16
I32 = jnp.int32

D_MODEL = 2048
N_META = 16
LRU_WIDTH = 1024
LRU_HEADS = 16
LRU_HEAD_DIM = LRU_WIDTH // LRU_HEADS
CONV_WIDTH = 4
LRU_C = 8.0
POOL_WIDTH = D_MODEL - LRU_WIDTH
POOL_WINDOWS = (2, 4, 8, 16)
POOL_GROUP_DIM = POOL_WIDTH // len(POOL_WINDOWS)
POOL_BUF = max(POOL_WINDOWS) - 1
IN_WIDTH = 2 * LRU_WIDTH + POOL_WIDTH
N_EXPERTS = 32
TOP_K = 4
D_FF = D_MODEL
SWIGLU_ALPHA = 1.702
SWIGLU_LIMIT = 7.0
RMS_EPS = 1e-6

V7X_MXU_DIM = 256
SUBLANES = 8
BF16_ROWS = 16

PROMPT_TILE = 512
TIME_TILE = 256
MOE_UNIT = 128
MOE_TM_UNITS = 8
MOE_TM = MOE_TM_UNITS * MOE_UNIT
MOE_GROUP_UNITS = 11
MOE_GROUP_ROWS = MOE_UNIT * MOE_GROUP_UNITS
MOE_TILE_STEP = 64
MOE_WHOLE_GROUP_ROWS = (1088, 1152, 1216, 1280, 1344, 1408)
MOE_TF = V7X_MXU_DIM
PERM_CHUNK = V7X_MXU_DIM
ROUTE_CHUNK = 512


VMEM_MB_ROW_TILES = 48
VMEM_MB_STAGED = 56


def _params(vmem_mb, sem):
    return pltpu.CompilerParams(dimension_semantics=sem, vmem_limit_bytes=vmem_mb << 20)


def _rmsnorm(x, g):
    return x * lax.rsqrt(jnp.mean(x * x, axis=-1, keepdims=True) + RMS_EPS) * g


def _resident(shape):
    return pl.BlockSpec(shape, lambda *_: (0,) * len(shape), pipeline_mode=pl.Buffered(1))


def _proj_kernel(x_ref, g_ref, w_ref, o_ref):
    xn = _rmsnorm(x_ref[...], g_ref[...])
    o_ref[...] = jnp.dot(xn.astype(BF16), w_ref[...], preferred_element_type=F32)


def _proj(x, g, w_bf16, tile):
    rows = x.shape[0]
    return pl.pallas_call(
        _proj_kernel,
        grid=(rows // tile,),
        in_specs=[pl.BlockSpec((tile, D_MODEL), lambda i: (i, 0)),
                  _resident((1, D_MODEL)), _resident((D_MODEL, IN_WIDTH))],
        out_specs=pl.BlockSpec((tile, IN_WIDTH), lambda i: (i, 0)),
        out_shape=jax.ShapeDtypeStruct((rows, IN_WIDTH), F32),
        compiler_params=_params(VMEM_MB_ROW_TILES, ("arbitrary",)),
        name="proj",
    )(x, g, w_bf16)


def _lru_coeffs(xc, wa_ref, ba_ref, wx_ref, bx_ref, lam_ref):
    xcb = xc.astype(BF16)
    nq = LRU_WIDTH // V7X_MXU_DIM
    rs, gs = [], []
    for q in range(nq):
        blk = xcb[:, q * V7X_MXU_DIM:(q + 1) * V7X_MXU_DIM]
        rs.append(jnp.dot(blk, wa_ref[q], preferred_element_type=F32))
        gs.append(jnp.dot(blk, wx_ref[q], preferred_element_type=F32))
    r = jax.nn.sigmoid(jnp.concatenate(rs, axis=-1) + ba_ref[...])
    i = jax.nn.sigmoid(jnp.concatenate(gs, axis=-1) + bx_ref[...])
    log_a = -LRU_C * r * jax.nn.softplus(-lam_ref[...])
    a = jnp.exp(log_a)
    th = jnp.tanh(log_a)
    beta = jnp.sqrt(-2.0 * th / (1.0 - th))
    return a, beta * i * xc


def _pool_project(wins, u, inv_cnt, pw_ref, ps_ref):
    outs = []
    for g in range(len(POOL_WINDOWS)):
        sl = slice(g * POOL_GROUP_DIM, (g + 1) * POOL_GROUP_DIM)
        d = (wins[g] * inv_cnt[g] - u[:, sl]).astype(BF16)
        outs.append(jnp.dot(d, pw_ref[g], preferred_element_type=F32))
    return jnp.concatenate(outs, axis=-1) * ps_ref[...]


def _mix_seq_kernel(x_ref, g1_ref, win_ref, conv0_ref, h0_ref, pool0_ref,
                    cw_ref, cb_ref, wa_ref, ba_ref, wx_ref, bx_ref, lam_ref, pw_ref, ps_ref,
                    yab_ref, ht_ref, ctail_ref, ptail_ref, cext, pext, hcar, *, tt, start):
    t = pl.program_id(1)
    hist = 2 * SUBLANES

    @pl.when(t == 0)
    def _():
        cext[0:SUBLANES, :] = conv0_ref[...]
        pext[0:hist, :] = pool0_ref[...]
        hcar[...] = h0_ref[...]

    xnb = _rmsnorm(x_ref[...], g1_ref[...]).astype(BF16)
    xa = jnp.dot(xnb, win_ref[:, :LRU_WIDTH], preferred_element_type=F32)
    ga = jnp.dot(xnb, win_ref[:, LRU_WIDTH:2 * LRU_WIDTH], preferred_element_type=F32)
    u = jnp.dot(xnb, win_ref[:, 2 * LRU_WIDTH:], preferred_element_type=F32)

    cext[SUBLANES:SUBLANES + tt, :] = xa
    cw = cw_ref[...]
    xc = cb_ref[...] + cw[CONV_WIDTH - 1:CONV_WIDTH] * xa
    for k in range(1, CONV_WIDTH):
        xc = xc + cw[CONV_WIDTH - 1 - k:CONV_WIDTH - k] * cext[SUBLANES - k:SUBLANES - k + tt, :]
    cext[0:SUBLANES, :] = cext[tt:tt + SUBLANES, :]
    ctail_ref[...] = cext[0:SUBLANES, :]

    a, b = _lru_coeffs(xc, wa_ref, ba_ref, wx_ref, bx_ref, lam_ref)

    groups = tt // SUBLANES
    a = a.reshape(groups, SUBLANES, LRU_WIDTH)
    b = b.reshape(groups, SUBLANES, LRU_WIDTH)
    sub = lax.broadcasted_iota(I32, (groups, SUBLANES, LRU_WIDTH), 1)
    s = 1
    while s < SUBLANES:
        keep = sub >= s
        a_prev = jnp.where(keep, pltpu.roll(a, s, 1), 1.0)
        b_prev = jnp.where(keep, pltpu.roll(b, s, 1), 0.0)
        b = b + a * b_prev
        a = a * a_prev
        s *= 2
    h = hcar[SUBLANES - 1:SUBLANES, :]
    hrows = []
    for g in range(groups):
        hg = a[g] * h + b[g]
        hrows.append(hg)
        h = hg[SUBLANES - 1:SUBLANES, :]
    hs = jnp.concatenate(hrows, axis=0)
    hcar[...] = hrows[-1]
    ht_ref[...] = hrows[-1]
    ya = hs * jax.nn.gelu(ga)

    pext[hist:hist + tt, :] = u
    e = pext[...]
    wins = []
    shift = 1
    for g in range(len(POOL_WINDOWS)):
        e = e + pltpu.roll(e, shift, 0)
        wins.append(e[hist:hist + tt, :POOL_GROUP_DIM])
        if g + 1 < len(POOL_WINDOWS):
            e = e[:, POOL_GROUP_DIM:]
        shift *= 2
    pext[0:hist, :] = pext[tt:tt + hist, :]
    ptail_ref[...] = pext[0:hist, :]

    if start >= POOL_BUF:
        inv_cnt = [1.0 / w for w in POOL_WINDOWS]
    else:
        pos = start + t * tt + lax.broadcasted_iota(I32, (tt, 1), 0)
        inv_cnt = [1.0 / jnp.minimum(w, pos + 1).astype(F32) for w in POOL_WINDOWS]
    yb = _pool_project(wins, u, inv_cnt, pw_ref, ps_ref)

    yab_ref[:, :LRU_WIDTH] = ya.astype(BF16)
    yab_ref[:, LRU_WIDTH:] = yb.astype(BF16)


def _mix_seq(x, g1, w_in_bf16, batch, seq, tt, start, conv0, h0, pool0, mixw):
    nt = seq // tt
    hist = 2 * SUBLANES
    rows = lambda w: pl.BlockSpec((tt, w), lambda b, t: (b * nt + t, 0))
    state = lambda r: pl.BlockSpec((None, r, LRU_WIDTH), lambda b, t: (b, 0, 0))
    tail = lambda r: jax.ShapeDtypeStruct((batch, r, LRU_WIDTH), F32)
    return pl.pallas_call(
        functools.partial(_mix_seq_kernel, tt=tt, start=start),
        grid=(batch, nt),
        in_specs=[rows(D_MODEL), _resident((1, D_MODEL)), _resident((D_MODEL, IN_WIDTH)),
                  state(SUBLANES), state(SUBLANES), state(hist)] + [_resident(w.shape) for w in mixw],
        out_specs=[rows(D_MODEL), state(SUBLANES), state(SUBLANES), state(hist)],
        out_shape=[jax.ShapeDtypeStruct((batch * seq, D_MODEL), BF16), tail(SUBLANES), tail(SUBLANES), tail(hist)],
        scratch_shapes=[pltpu.VMEM((tt + SUBLANES, LRU_WIDTH), F32),
                        pltpu.VMEM((tt + hist, POOL_WIDTH), F32),
                        pltpu.VMEM((SUBLANES, LRU_WIDTH), F32)],
        compiler_params=_params(VMEM_MB_STAGED, ("arbitrary", "arbitrary")),
        name="mix_seq",
    )(x, g1, w_in_bf16, conv0, h0, pool0, *mixw)


def _mix_step_kernel(xa_ref, ga_ref, ub_ref, sconv_ref, sh_ref, spool_ref,
                     cw_ref, cb_ref, wa_ref, ba_ref, wx_ref, bx_ref, lam_ref, pw_ref, ps_ref,
                     yab_ref, h_ref):
    xa = xa_ref[...]
    cw = cw_ref[...]
    xc = cb_ref[...] + cw[CONV_WIDTH - 1:CONV_WIDTH] * xa
    for k in range(1, CONV_WIDTH):
        xc = xc + cw[CONV_WIDTH - 1 - k:CONV_WIDTH - k] * sconv_ref[CONV_WIDTH - 1 - k]
    a, b = _lru_coeffs(xc, wa_ref, ba_ref, wx_ref, bx_ref, lam_ref)
    h = a * sh_ref[...] + b
    h_ref[...] = h
    ya = h * jax.nn.gelu(ga_ref[...])

    u = ub_ref[...]
    wins = []
    for g, w in enumerate(POOL_WINDOWS):
        sl = slice(g * POOL_GROUP_DIM, (g + 1) * POOL_GROUP_DIM)
        acc = u[:, sl]
        for k in range(1, w):
            acc = acc + spool_ref[POOL_BUF - k, :, sl]
        wins.append(acc)
    yb = _pool_project(wins, u, [1.0 / w for w in POOL_WINDOWS], pw_ref, ps_ref)
    yab_ref[:, :LRU_WIDTH] = ya.astype(BF16)
    yab_ref[:, LRU_WIDTH:] = yb.astype(BF16)


def _mix_step(proj, sconv_t, sh, spool_t, mixw):
    rows = proj.shape[0]
    col = lambda c: pl.BlockSpec((rows, LRU_WIDTH), lambda i: (0, c))
    full = lambda a: pl.BlockSpec(a.shape, lambda i: (0,) * a.ndim)
    return pl.pallas_call(
        _mix_step_kernel,
        grid=(1,),
        in_specs=[col(0), col(1), col(2), full(sconv_t), full(sh), full(spool_t)] + [full(w) for w in mixw],
        out_specs=[pl.BlockSpec((rows, D_MODEL), lambda i: (0, 0)),
                   pl.BlockSpec((rows, LRU_WIDTH), lambda i: (0, 0))],
        out_shape=[jax.ShapeDtypeStruct((rows, D_MODEL), BF16),
                   jax.ShapeDtypeStruct((rows, LRU_WIDTH), F32)],
        compiler_params=_params(VMEM_MB_ROW_TILES, ("arbitrary",)),
        name="mix_step",
    )(proj, proj, proj, sconv_t, sh, spool_t, *mixw)


def _out_kernel(yab_ref, x_ref, wo_ref, g2_ref, rw_both_ref, rw_hi_ref, rb_ref,
                h1_ref, xn_ref, eid_ref, wgt_ref, *, tile):
    h1_ref[...] = x_ref[...] + jnp.dot(yab_ref[...], wo_ref[...], preferred_element_type=F32)

    chunk = min(ROUTE_CHUNK, tile)

    def route(c, carry):
        rows = pl.ds(pl.multiple_of(c * chunk, chunk), chunk)
        xn = _rmsnorm(h1_ref[rows, :], g2_ref[...])
        xn_hi = xn.astype(BF16)
        xn_ref[rows, :] = xn_hi
        xn_lo = (xn - xn_hi.astype(F32)).astype(BF16)
        nt_dims = (((1,), (1,)), ((), ()))
        both = lax.dot_general(rw_both_ref[...], xn_hi, nt_dims, preferred_element_type=F32)
        logits = (both[:N_EXPERTS] + both[N_EXPERTS:]
                  + lax.dot_general(rw_hi_ref[...], xn_lo, nt_dims, preferred_element_type=F32)) + rb_ref[...]
        eidx = lax.broadcasted_iota(I32, (N_EXPERTS, chunk), 0)
        vals, ids = [], []
        for _ in range(TOP_K):
            m = jnp.max(logits, axis=0, keepdims=True)
            idx = jnp.min(jnp.where(logits == m, eidx, N_EXPERTS), axis=0, keepdims=True)
            vals.append(m)
            ids.append(idx)
            logits = jnp.where(eidx == idx, -jnp.inf, logits)
        ex = [jnp.exp(v - vals[0]) for v in vals]
        tot = ex[0]
        for e_ in ex[1:]:
            tot = tot + e_
        eid_ref[:, rows] = jnp.concatenate(ids, axis=0)
        wgt_ref[:, rows] = jnp.concatenate([e_ / tot for e_ in ex], axis=0)
        return carry
    lax.fori_loop(0, tile // chunk, route, 0)


def _out(yab, x, wo_bf16, g2, rw, rb, tile):
    rw_hi = rw.T.astype(BF16)
    rw_lo = (rw.T - rw_hi.astype(F32)).astype(BF16)
    rw_both = jnp.concatenate([rw_hi, rw_lo], axis=0)
    n = x.shape[0]
    rowblk = lambda w: pl.BlockSpec((tile, w), lambda i: (i, 0))
    colblk = pl.BlockSpec((TOP_K, tile), lambda i: (0, i))
    return pl.pallas_call(
        functools.partial(_out_kernel, tile=tile),
        grid=(n // tile,),
        in_specs=[rowblk(D_MODEL), rowblk(D_MODEL), _resident((D_MODEL, D_MODEL)), _resident((1, D_MODEL)),
                  _resident((2 * N_EXPERTS, D_MODEL)), _resident((N_EXPERTS, D_MODEL)), _resident((N_EXPERTS, 1))],
        out_specs=[rowblk(D_MODEL), rowblk(D_MODEL), colblk, colblk],
        out_shape=[jax.ShapeDtypeStruct((n, D_MODEL), F32), jax.ShapeDtypeStruct((n, D_MODEL), BF16),
                   jax.ShapeDtypeStruct((TOP_K, n), I32), jax.ShapeDtypeStruct((TOP_K, n), F32)],
        compiler_params=_params(VMEM_MB_ROW_TILES, ("arbitrary",)),
        name="out_router",
    )(yab, x, wo_bf16, g2, rw_both, rw_hi, rb)


def _for_segments(tile, seg_len_ref, max_len, fn):
    def per_expert(e, c):
        s = tile * N_EXPERTS + e
        length = seg_len_ref[s]
        size = max_len
        while size >= BF16_ROWS:
            @pl.when((length // size) % 2 == 1)
            def _(size=size):
                fn(s, length // (2 * size) * (2 * size), size)
            size //= 2
        return c
    lax.fori_loop(0, N_EXPERTS, per_expert, 0)


def _seg_rows(ref, start, rows):
    return ref.at[pl.ds(pl.multiple_of(start, BF16_ROWS), rows)]


def _wait_rows(total, max_rows, copy_of_rows):
    size = 1 << (max_rows.bit_length() - 1)
    while size >= BF16_ROWS:
        @pl.when((total // size) % 2 == 1)
        def _(size=size):
            copy_of_rows(size).wait()
        size //= 2


def _dispatch_kernel(seg_loc_ref, seg_dst_ref, seg_len_ref, tile_rows_ref, tail_dst_ref, tail_len_ref,
                     xn_ref, dest_ref, *rest, tile0, kt, zero_tails):
    xs_hbm, cb, zb, sem, zsem = rest[-5:]
    i = pl.program_id(0)
    n_steps = pl.num_programs(0)
    slot = i % 2
    tile = tile0 + i
    tt = xn_ref.shape[0]

    def seg_copy(sl, s, off, rows):
        return pltpu.make_async_copy(_seg_rows(cb.at[sl], seg_loc_ref[s] + off, rows),
                                     _seg_rows(xs_hbm, seg_dst_ref[s] + off, rows), sem.at[sl])

    def start_segments(tl, sl):
        _for_segments(tl, seg_len_ref, tt, lambda s, off, rows: seg_copy(sl, s, off, rows).start())

    def wait_segments(tl, sl):
        _wait_rows(tile_rows_ref[tl], kt, lambda rows: pltpu.make_async_copy(
            cb.at[sl, pl.ds(0, rows)], xs_hbm.at[pl.ds(0, rows)], sem.at[sl]))

    @pl.when(i >= 2)
    def _():
        wait_segments(tile - 2, slot)

    x = xn_ref[...]
    d = [dest_ref[k:k + 1, :] for k in range(TOP_K)]
    for c0 in range(0, kt, PERM_CHUNK):
        r = c0 + lax.broadcasted_iota(I32, (PERM_CHUNK, tt), 0)
        p = jnp.zeros((PERM_CHUNK, tt), F32)
        for k in range(TOP_K):
            p = jnp.where(d[k] == r, 1.0, p)
        rows = jnp.dot(p.astype(BF16), x, preferred_element_type=F32)
        cb[slot, c0:c0 + PERM_CHUNK, :] = rows.astype(BF16)

    start_segments(tile, slot)

    @pl.when(i == n_steps - 1)
    def _():
        @pl.when(i >= 1)
        def _():
            wait_segments(tile - 1, 1 - slot)
        wait_segments(tile, slot)

    if zero_tails:
        zb[...] = jnp.zeros(zb.shape, BF16)

        def tail_copy(e, j):
            return pltpu.make_async_copy(zb, _seg_rows(xs_hbm, tail_dst_ref[e] + j * BF16_ROWS, BF16_ROWS),
                                         zsem.at[0])

        def tails(method):
            def per_expert(e, c):
                def per_chunk(j, c2):
                    getattr(tail_copy(e, j), method)()
                    return c2
                lax.fori_loop(0, tail_len_ref[e] // BF16_ROWS, per_chunk, 0)
                return c
            lax.fori_loop(0, N_EXPERTS, per_expert, 0)

        @pl.when(i == n_steps - 1)
        def _():
            tails("start")
            tails("wait")


def _dispatch(tables, xn, dest, xs, tile0, tt, kt, xs_rows, zero_tails):
    n = xn.shape[0]
    in_specs = [pl.BlockSpec((tt, D_MODEL), lambda i, *_: (i, 0)),
                pl.BlockSpec((TOP_K, tt), lambda i, *_: (0, i))]
    args = [*tables, xn, dest]
    aliases = {}
    if xs is not None:
        in_specs.append(pl.BlockSpec(memory_space=pl.ANY))
        aliases = {len(args): 0}
        args.append(xs)
    grid_spec = pltpu.PrefetchScalarGridSpec(
        num_scalar_prefetch=len(tables),
        grid=(n // tt,),
        in_specs=in_specs,
        out_specs=pl.BlockSpec(memory_space=pl.ANY),
        scratch_shapes=[pltpu.VMEM((2, kt, D_MODEL), BF16), pltpu.VMEM((BF16_ROWS, D_MODEL), BF16),
                        pltpu.SemaphoreType.DMA((2,)), pltpu.SemaphoreType.DMA((1,))],
    )
    return pl.pallas_call(
        functools.partial(_dispatch_kernel, tile0=tile0, kt=kt, zero_tails=zero_tails),
        grid_spec=grid_spec,
        out_shape=jax.ShapeDtypeStruct((xs_rows, D_MODEL), BF16),
        input_output_aliases=aliases,
        compiler_params=_params(VMEM_MB_ROW_TILES, ("arbitrary",)),
        name="dispatch",
    )(*args)


def _moe_kernel(ge_ref, grow_ref, gnu_ref, gtile_ref, ng_ref,
                xs_hbm, wg_hbm, wu_hbm, wd_hbm, bg_ref, bu_ref, bd_ref,
                ys_hbm, xb, acc, ypk, wgf, wuf, wdf, xsem, wsem, osem, *, n_f):
    g = pl.program_id(0)
    n_groups = ng_ref[0]
    xslot = g % 2

    def unit_rows(j):
        return pl.ds(pl.multiple_of(j * MOE_UNIT, MOE_UNIT), MOE_UNIT)

    def hbm_unit(ref, grp, j):
        return ref.at[pl.ds(pl.multiple_of(grow_ref[grp] + j * MOE_UNIT, MOE_UNIT), MOE_UNIT)]

    def copy_in(grp, sl, j):
        return pltpu.make_async_copy(hbm_unit(xs_hbm, grp, j), xb.at[sl, unit_rows(j)], xsem.at[sl])

    def copy_out(grp, j):
        return pltpu.make_async_copy(ypk.at[unit_rows(j)], hbm_unit(ys_hbm, grp, j), osem.at[0])

    def for_units(grp, fn):
        def body(j, c):
            fn(grp, j)
            return c
        lax.fori_loop(0, gnu_ref[grp], body, 0)

    def weight_copies(grp, f, ws):
        e = ge_ref[grp]
        cols = pl.ds(pl.multiple_of(f * MOE_TF, MOE_TF), MOE_TF)
        return (pltpu.make_async_copy(wg_hbm.at[e, :, cols], wgf.at[ws], wsem.at[ws, 0]),
                pltpu.make_async_copy(wu_hbm.at[e, :, cols], wuf.at[ws], wsem.at[ws, 1]),
                pltpu.make_async_copy(wd_hbm.at[e, cols, :], wdf.at[ws], wsem.at[ws, 2]))

    @pl.when(g < n_groups)
    def _():
        @pl.when(g == 0)
        def _():
            for_units(0, lambda grp, j: copy_in(grp, 0, j).start())
            for cp in weight_copies(0, 0, 0):
                cp.start()
        for_units(g, lambda grp, j: copy_in(grp, xslot, j).wait())

        @pl.when(g + 1 < n_groups)
        def _():
            for_units(g + 1, lambda grp, j: copy_in(grp, 1 - xslot, j).start())

        def init(grp, j):
            acc[unit_rows(j), :] = jnp.broadcast_to(bd_ref[...], (MOE_UNIT, D_MODEL))
        for_units(g, init)
        n_units = gnu_ref[g]
        tile_rows = gtile_ref[g]

        def chunk(f, c):
            ws = f % 2
            for cp in weight_copies(g, f, ws):
                cp.wait()

            @pl.when(f + 1 < n_f)
            def _():
                for cp in weight_copies(g, f + 1, 1 - ws):
                    cp.start()

            @pl.when(jnp.logical_and(f + 1 == n_f, g + 1 < n_groups))
            def _():
                for cp in weight_copies(g + 1, 0, 1 - ws):
                    cp.start()

            bg = bg_ref[pl.ds(f, 1), :]
            bu = bu_ref[pl.ds(f, 1), :]

            def ffn(start, size):
                x = xb[xslot, pl.ds(start, size), :]
                gg = jnp.dot(x, wgf[ws].astype(BF16), preferred_element_type=F32) + bg
                uu = jnp.dot(x, wuf[ws].astype(BF16), preferred_element_type=F32) + bu
                gg = jnp.minimum(gg, SWIGLU_LIMIT)
                uu = jnp.clip(uu, -SWIGLU_LIMIT, SWIGLU_LIMIT)
                hdn = gg * jax.nn.sigmoid(SWIGLU_ALPHA * gg) * (uu + 1.0)
                acc[pl.ds(start, size), :] += jnp.dot(hdn.astype(BF16), wdf[ws].astype(BF16),
                                                      preferred_element_type=F32)

            whole = jnp.bool_(False)
            for rows in MOE_WHOLE_GROUP_ROWS:
                whole = jnp.logical_or(whole, tile_rows == rows)

                @pl.when(tile_rows == rows)
                def _(rows=rows):
                    ffn(0, rows)

            @pl.when(jnp.logical_not(whole))
            def _():
                def full_tile(j, c2):
                    ffn(pl.multiple_of(j * MOE_TM, MOE_TM), MOE_TM)
                    return c2
                lax.fori_loop(0, n_units // MOE_TM_UNITS, full_tile, 0)
                part = MOE_TM_UNITS // 2
                while part >= 1:
                    @pl.when((n_units // part) % 2 == 1)
                    def _(part=part):
                        done = n_units // (2 * part) * (2 * part)
                        ffn(pl.multiple_of(done * MOE_UNIT, MOE_UNIT), part * MOE_UNIT)
                    part //= 2
            return c
        lax.fori_loop(0, n_f, chunk, 0)

        @pl.when(g > 0)
        def _():
            for_units(g - 1, lambda grp, j: copy_out(grp, j).wait())

        def finish(grp, j):
            ypk[unit_rows(j), :] = acc[unit_rows(j), :].astype(BF16)
            copy_out(grp, j).start()
        for_units(g, finish)

    @pl.when(g == pl.num_programs(0) - 1)
    def _():
        for_units(n_groups - 1, lambda grp, j: copy_out(grp, j).wait())


def _moe(ge, grow, gnu, gtile, ng, xs, wg, wu, wd, bg, bu, bd, n_groups_max):
    n_f = D_FF // MOE_TF
    assert n_f % 2 == 0
    per_expert = lambda shape: pl.BlockSpec((None,) + shape, lambda g, ge, *_: (ge[g], 0, 0))
    any_spec = pl.BlockSpec(memory_space=pl.ANY)
    grid_spec = pltpu.PrefetchScalarGridSpec(
        num_scalar_prefetch=5,
        grid=(n_groups_max,),
        in_specs=[any_spec, any_spec, any_spec, any_spec,
                  per_expert((n_f, MOE_TF)), per_expert((n_f, MOE_TF)), per_expert((1, D_MODEL))],
        out_specs=any_spec,
        scratch_shapes=[
            pltpu.VMEM((2, MOE_GROUP_ROWS, D_MODEL), BF16),
            pltpu.VMEM((MOE_GROUP_ROWS, D_MODEL), F32),
            pltpu.VMEM((MOE_GROUP_ROWS, D_MODEL), BF16),
            pltpu.VMEM((2, D_MODEL, MOE_TF), F32),
            pltpu.VMEM((2, D_MODEL, MOE_TF), F32),
            pltpu.VMEM((2, MOE_TF, D_MODEL), F32),
            pltpu.SemaphoreType.DMA((2,)),
            pltpu.SemaphoreType.DMA((2, 3)),
            pltpu.SemaphoreType.DMA((1,)),
        ],
    )
    return pl.pallas_call(
        functools.partial(_moe_kernel, n_f=n_f),
        grid_spec=grid_spec,
        out_shape=jax.ShapeDtypeStruct(xs.shape, BF16),
        compiler_params=_params(VMEM_MB_STAGED, ("arbitrary",)),
        name="moe_ffn",
    )(ge, grow, gnu, gtile, ng, xs, wg, wu, wd,
      bg.reshape(N_EXPERTS, n_f, MOE_TF), bu.reshape(N_EXPERTS, n_f, MOE_TF), bd.reshape(N_EXPERTS, 1, D_MODEL))


def _combine_kernel(seg_loc_ref, seg_src_ref, seg_len_ref, tile_rows_ref,
                    ys_hbm, h1_ref, dest_ref, w_ref, fn_ref, y_ref, sb, wbuf, sem, *, tile0, kt):
    i = pl.program_id(0)
    n_steps = pl.num_programs(0)
    slot = i % 2
    tile = tile0 + i
    tt = h1_ref.shape[0]

    def seg_copy(sl, s, off, rows):
        return pltpu.make_async_copy(_seg_rows(ys_hbm, seg_src_ref[s] + off, rows),
                                     _seg_rows(sb.at[sl], seg_loc_ref[s] + off, rows), sem.at[sl])

    def start_segments(tl, sl):
        _for_segments(tl, seg_len_ref, tt, lambda s, off, rows: seg_copy(sl, s, off, rows).start())

    @pl.when(i == 0)
    def _():
        sb[...] = jnp.zeros(sb.shape, BF16)
        start_segments(tile, slot)

    @pl.when(i + 1 < n_steps)
    def _():
        start_segments(tile + 1, 1 - slot)

    _wait_rows(tile_rows_ref[tile], kt, lambda rows: pltpu.make_async_copy(
        ys_hbm.at[pl.ds(0, rows)], sb.at[slot, pl.ds(0, rows)], sem.at[slot]))

    acc = h1_ref[...]
    d = [jnp.broadcast_to(dest_ref[:, k:k + 1], (tt, PERM_CHUNK)) for k in range(TOP_K)]
    w = [jnp.broadcast_to(w_ref[:, k:k + 1], (tt, PERM_CHUNK)) for k in range(TOP_K)]
    half = kt // 2
    for h in range(2):
        for c0 in range(0, half, PERM_CHUNK):
            col = h * half + c0 + lax.broadcasted_iota(I32, (tt, PERM_CHUNK), 1)
            wm = jnp.zeros((tt, PERM_CHUNK), F32)
            for k in range(TOP_K):
                wm = jnp.where(d[k] == col, w[k], wm)
            wbuf[h, :, c0:c0 + PERM_CHUNK] = wm.astype(BF16)
        acc = acc + jnp.dot(wbuf[h], sb[slot, h * half:(h + 1) * half, :], preferred_element_type=F32)
    y_ref[...] = _rmsnorm(acc, fn_ref[...])


def _combine(tables, ys, h1, dest_t, wgt_t, fn, tile0, tt, kt):
    n = h1.shape[0]
    grid_spec = pltpu.PrefetchScalarGridSpec(
        num_scalar_prefetch=len(tables),
        grid=(n // tt,),
        in_specs=[pl.BlockSpec(memory_space=pl.ANY),
                  pl.BlockSpec((tt, D_MODEL), lambda i, *_: (i, 0)),
                  pl.BlockSpec((tt, TOP_K), lambda i, *_: (i, 0)),
                  pl.BlockSpec((tt, TOP_K), lambda i, *_: (i, 0)),
                  pl.BlockSpec((1, D_MODEL), lambda i, *_: (0, 0))],
        out_specs=pl.BlockSpec((tt, D_MODEL), lambda i, *_: (i, 0)),
        scratch_shapes=[pltpu.VMEM((2, kt, D_MODEL), BF16), pltpu.VMEM((2, tt, kt // 2), BF16),
                        pltpu.SemaphoreType.DMA((2,))],
    )
    return pl.pallas_call(
        functools.partial(_combine_kernel, tile0=tile0, kt=kt),
        grid_spec=grid_spec,
        out_shape=jax.ShapeDtypeStruct((n, D_MODEL), F32),
        compiler_params=_params(VMEM_MB_STAGED, ("arbitrary",)),
        name="combine",
    )(*tables, ys, h1, dest_t, wgt_t, fn)


def _ceil_to(x, m):
    return (x + m - 1) // m * m


def _routing_tables(eid, tile_lens, n_groups_max):
    starts =np.concatenate([[0], np.cumsum(tile_lens)[:-1]]).astype(np.int64)
    onehot = (eid[:, :, None] == jnp.arange(N_EXPERTS, dtype=I32)[None, None, :]).astype(I32)
    member = jnp.sum(onehot, axis=0)
    incl = jnp.cumsum(member, axis=0)
    before = incl - member
    runs = []
    for t, l in enumerate(tile_lens):
        if runs and runs[-1][3] == l:
            runs[-1][2] += 1
        else:
            runs.append([t, int(starts[t]), 1, int(l)])
    rank0 = jnp.concatenate([before[s0:s0 + c * l:l] for _, s0, c, l in runs])
    cnt = jnp.concatenate([incl[s0 + l - 1:s0 + c * l:l] for _, s0, c, l in runs]) - rank0
    c8 = _ceil_to(cnt, BF16_ROWS)
    seg_loc = jnp.cumsum(c8, axis=1) - c8
    rows_e = jnp.sum(c8, axis=0)
    region_e = _ceil_to(rows_e, MOE_UNIT)
    start_e = jnp.cumsum(region_e) - region_e
    seg_pos = start_e[None, :] + jnp.cumsum(c8, axis=0) - c8

    def per_token(tab):
        return jnp.concatenate([jnp.broadcast_to(tab[t0:t0 + c, None, :], (c, l, N_EXPERTS)).reshape(c * l, N_EXPERTS)
                                for t0, _, c, l in runs])
    base = per_token(seg_loc - rank0) + before
    dest = jnp.sum(onehot * base[None], axis=2).astype(I32)

    ntiles_e = region_e // MOE_UNIT
    groups_e = (ntiles_e + MOE_GROUP_UNITS - 1) // MOE_GROUP_UNITS
    gend_e = jnp.cumsum(groups_e)
    gstart_e = gend_e - groups_e
    n_groups = gend_e[-1]
    j = jnp.arange(n_groups_max, dtype=I32)
    j_act = jnp.minimum(j, n_groups - 1)
    e_j = jnp.minimum(jnp.sum((gend_e[None, :] <= j_act[:, None]).astype(I32), axis=1), N_EXPERTS - 1)
    sel = (e_j[:, None] == jnp.arange(N_EXPERTS, dtype=I32)[None, :]).astype(I32)
    pick = lambda v: jnp.sum(sel * v[None, :], axis=1)
    local = j_act - pick(gstart_e)
    grow = pick(start_e) + local * MOE_GROUP_ROWS
    gnt = jnp.where(j < n_groups, jnp.clip(pick(ntiles_e) - local * MOE_GROUP_UNITS, 0, MOE_GROUP_UNITS), 0)
    grows = jnp.where(j < n_groups, jnp.clip(pick(rows_e) - local * MOE_GROUP_ROWS, 0, MOE_GROUP_ROWS), 0)
    gtile = _ceil_to(grows, MOE_TILE_STEP)

    flat = lambda a: a.reshape(-1).astype(I32)
    seg_tabs = (flat(seg_loc), flat(seg_pos), flat(c8), jnp.sum(c8, axis=1).astype(I32))
    tail_tabs = ((start_e + rows_e).astype(I32), (region_e - rows_e).astype(I32))
    group_tabs = (e_j.astype(I32), grow.astype(I32), gnt.astype(I32), gtile.astype(I32),
                  n_groups.reshape(1).astype(I32))
    return dest, seg_tabs, tail_tabs, group_tabs


def _block_diag(w, per_block):
    h, d, _ = w.shape
    nb = h // per_block
    eye = jnp.eye(per_block, dtype=w.dtype)
    w4 = w.reshape(nb, per_block, d, d)
    out = jnp.einsum("bpij,pq->bpiqj", w4, eye)
    return out.reshape(nb, per_block * d, per_block * d)


def kernel(x_prompt, x_sample, state_conv, state_rglru, state_pool, meta_tokens, norm1, w_in, conv_w, conv_b, lru_wa, lru_ba, lru_wx, lru_bx, lru_lambda, pool_w, pool_scale, w_out, norm2, router_w, router_b, exp_wg, exp_bg, exp_wu, exp_bu, exp_wd, exp_bd, final_norm):
    batch, seq, _ = x_prompt.shape
    dec = x_sample.shape[0]
    n_prompt = batch * seq
    n_tok = n_prompt + dec
    l = 0
    row = lambda v: v.reshape(1, -1)

    xp = x_prompt.reshape(n_prompt, D_MODEL)
    xs_tok = x_sample.reshape(dec, D_MODEL)
    w_in_b = w_in[l].astype(BF16)
    w_out_b = w_out[l].astype(BF16)
    heads_per_block = V7X_MXU_DIM // LRU_HEAD_DIM
    mixw = (conv_w[l], row(conv_b[l]),
            _block_diag(lru_wa[l], heads_per_block).astype(BF16), row(lru_ba[l]),
            _block_diag(lru_wx[l], heads_per_block).astype(BF16), row(lru_bx[l]),
            row(lru_lambda[l]), pool_w[l].astype(BF16), row(pool_scale[l]))
    g1 = row(norm1[l])

    zeros = lambda r: jnp.zeros((1, r, LRU_WIDTH), F32)
    _, h_meta, conv_meta, pool_meta = _mix_seq(meta_tokens, g1, w_in_b, 1, N_META, N_META, 0,
                                               zeros(SUBLANES), zeros(SUBLANES), zeros(2 * SUBLANES), mixw)
    per_seq = lambda v: jnp.broadcast_to(v, (batch,) + v.shape[1:])
    yab_p, h_p, conv_tail, pool_tail = _mix_seq(xp, g1, w_in_b, batch, seq, TIME_TILE, N_META,
                                                per_seq(conv_meta), per_seq(h_meta), per_seq(pool_meta), mixw)
    proj_s = _proj(xs_tok, g1, w_in_b, dec)
    yab_s, h_s = _mix_step(proj_s, jnp.swapaxes(state_conv[l], 0, 1), state_rglru[l],
                           jnp.swapaxes(state_pool[l], 0, 1), mixw)

    out_w = (w_out_b, row(norm2[l]), router_w[l], router_b[l].reshape(N_EXPERTS, 1))
    h1_p, xn_p, eid_p, wgt_p = _out(yab_p, xp, *out_w, PROMPT_TILE)
    h1_s, xn_s, eid_s, wgt_s = _out(yab_s, xs_tok, *out_w, dec)

    n_ptiles = n_prompt // PROMPT_TILE
    tile_lens = [PROMPT_TILE] * n_ptiles + [dec]
    seg_pad = (BF16_ROWS - 1) * N_EXPERTS
    kt_p = _ceil_to(TOP_K * PROMPT_TILE + seg_pad, PERM_CHUNK)
    kt_s = _ceil_to(TOP_K * dec + seg_pad, PERM_CHUNK)
    units_max = (TOP_K * n_tok + len(tile_lens) * seg_pad) // MOE_UNIT + N_EXPERTS
    xs_rows = units_max * MOE_UNIT
    n_groups_max = (units_max + N_EXPERTS * (MOE_GROUP_UNITS - 1)) // MOE_GROUP_UNITS

    eid = jnp.concatenate([eid_p, eid_s], axis=1)
    dest, seg_tabs, tail_tabs, group_tabs = _routing_tables(eid, tile_lens, n_groups_max)
    disp_tabs = seg_tabs + tail_tabs
    dest_p, dest_s = dest[:, :n_prompt], dest[:, n_prompt:]

    xs = _dispatch(disp_tabs, xn_p, dest_p, None, 0, PROMPT_TILE, kt_p, xs_rows, False)
    xs = _dispatch(disp_tabs, xn_s, dest_s, xs, n_ptiles, dec, kt_s, xs_rows, True)
    ys = _moe(*group_tabs, xs, exp_wg[l], exp_wu[l], exp_wd[l], exp_bg[l], exp_bu[l], exp_bd[l], n_groups_max)
    fn = row(final_norm)
    y_p = _combine(seg_tabs, ys, h1_p, dest_p.T, wgt_p.T, fn, 0, PROMPT_TILE, kt_p)
    y_s = _combine(seg_tabs, ys, h1_s, dest_s.T, wgt_s.T, fn, n_ptiles, dec, kt_s)

    conv_p = conv_tail[:, SUBLANES - (CONV_WIDTH - 1):]
    pool_p = pool_tail[:, 2 * SUBLANES - POOL_BUF:]
    conv_s = jnp.concatenate([state_conv[l][:, 1:], proj_s[:, None, :LRU_WIDTH]], axis=1)
    pool_s = jnp.concatenate([state_pool[l][:, 1:], proj_s[:, None, 2 * LRU_WIDTH:]], axis=1)
    return (y_p.reshape(batch, seq, D_MODEL), y_s.reshape(dec, 1, D_MODEL),
            conv_p[None], h_p[None, :, SUBLANES - 1], pool_p[None],
            conv_s[None], h_s[None], pool_s[None])
```
